```python
import math
import jax, jax.numpy as jnp
from jax import lax
import numpy as np

D_MODEL = 4096
BATCH = 2
SEQ = 4096
DEPTH = 2

N_META = 16
D_MIX = D_MODEL
RW_HEAD = 64
RW_WIDTH = 3 * D_MIX // 8
RW_HEADS = RW_WIDTH // RW_HEAD
RW_DECAY_LORA = 128
RW_AAA_LORA = 128
RW_GATE_LORA = 480
GN_EPS = 64e-5
AT_HEAD = 128
AT_WIDTH = 3 * D_MIX // 8
AT_HEADS = AT_WIDTH // AT_HEAD
AT_Q_LORA = 512
IDX_HEADS = 8
IDX_HEAD = 128
TOPK_MAX = 256
Q_BLOCK = 64
ROPE_THETA = 10000.0
POOL_WIDTH = D_MIX - RW_WIDTH - AT_WIDTH
POOL_WINDOWS = (2, 4, 8, 16)
POOL_GROUP = POOL_WIDTH // len(POOL_WINDOWS)
N_GROUPS = 8
EXPERTS_PER_GROUP = 8
N_EXPERTS = N_GROUPS * EXPERTS_PER_GROUP
TOP_K = 2
D_EXPERT = 512
MOE_BLOCK = 128
DN_ALPHA = (2 * DEPTH) ** 0.25
DN_BETA = (8 * DEPTH) ** -0.25
LN_EPS = 1e-5
RW_SIZES = (RW_WIDTH, RW_WIDTH, RW_WIDTH, RW_DECAY_LORA, RW_AAA_LORA, RW_GATE_LORA)
OTHER_SIZES = (AT_Q_LORA, AT_WIDTH, AT_WIDTH, IDX_HEAD, IDX_HEADS, POOL_WIDTH)
RW_COLS = sum(RW_SIZES)
D_IN = RW_COLS + sum(OTHER_SIZES)

kernel_name = "hymba_rwkv7_dsa_pool_hmoe_deepnorm"


def _layernorm(x, g, b, eps=LN_EPS):
    xf = x.astype(jnp.float32)
    mu = jnp.mean(xf, axis=-1, keepdims=True)
    var = jnp.mean(jnp.square(xf - mu), axis=-1, keepdims=True)
    return ((xf - mu) * lax.rsqrt(var + eps) * g + b).astype(x.dtype)


def _rmsnorm(x, g, eps=1e-6):
    xf = x.astype(jnp.float32)
    return (xf * lax.rsqrt(jnp.mean(jnp.square(xf), axis=-1, keepdims=True) + eps) * g).astype(x.dtype)


def _rope(x, pos):
    d = x.shape[-1]
    inv = ROPE_THETA ** (-jnp.arange(0, d, 2, dtype=jnp.float32) / d)
    ang = pos.astype(jnp.float32)[:, None] * inv[None, :]
    cos = jnp.cos(ang)[None, :, None, :]
    sin = jnp.sin(ang)[None, :, None, :]
    xf = x.astype(jnp.float32)
    x1, x2 = xf[..., : d // 2], xf[..., d // 2:]
    return jnp.concatenate([x1 * cos - x2 * sin, x2 * cos + x1 * sin], axis=-1).astype(x.dtype)


def _split(h, sizes):
    idx = np.cumsum(sizes)[:-1].tolist()
    return jnp.split(h, idx, axis=-1)


def _rwkv7_step(S, inp):
    r_t, w_t, k_t, v_t, kk_t, a_t = inp
    sa = jnp.einsum('bhij,bhj->bhi', S, -kk_t)
    S = S * w_t[:, :, None, :] + sa[..., None] * (kk_t * a_t)[:, :, None, :] + v_t[..., None] * k_t[:, :, None, :]
    y = jnp.einsum('bhij,bhj->bhi', S, r_t)
    return S, y


def rwkv7_mix(r, k, v, dw, da, dg, w2, w0, a2, a0, g2, k_k, k_a, r_k, gn_g, gn_b):
    f32 = jnp.float32
    bsz, t_len, _ = r.shape
    dt = r.dtype
    r, k, v = r.astype(f32), k.astype(f32), v.astype(f32)
    w = -jax.nn.softplus(-(w0 + jnp.tanh(dw.astype(f32)) @ w2.astype(f32))) - 0.5
    a = jax.nn.sigmoid(a0 + da.astype(f32) @ a2.astype(f32))
    g = jax.nn.sigmoid(dg.astype(f32)) @ g2.astype(f32)
    hs = lambda z: z.reshape(bsz, t_len, RW_HEADS, RW_HEAD)
    kk = hs(k * k_k)
    kk = kk / jnp.maximum(jnp.sqrt(jnp.sum(kk * kk, axis=-1, keepdims=True)), 1e-12)
    k = k * (1.0 + (a - 1.0) * k_a)
    decay = jnp.exp(-jnp.exp(w))
    r_h, k_h, v_h, a_h, d_h = hs(r), hs(k), hs(v), hs(a), hs(decay)
    seq_in = tuple(jnp.moveaxis(z, 1, 0) for z in (r_h, d_h, k_h, v_h, kk, a_h))
    s0 = jnp.zeros((bsz, RW_HEADS, RW_HEAD, RW_HEAD), f32)
    _, y = lax.scan(_rwkv7_step, s0, seq_in)
    y = jnp.moveaxis(y, 0, 1)
    mu = jnp.mean(y, axis=-1, keepdims=True)
    var = jnp.mean(jnp.square(y - mu), axis=-1, keepdims=True)
    yn = ((y - mu) * lax.rsqrt(var + GN_EPS)).reshape(bsz, t_len, RW_WIDTH) * gn_g + gn_b
    bonus = jnp.sum(r_h * k_h * r_k, axis=-1, keepdims=True) * v_h
    out = (yn + bonus.reshape(bsz, t_len, RW_WIDTH)) * g
    return out.astype(dt)


def dsa_attention(cq, k, v, kidx, widx, q_norm_g, w_uq, w_iq, kidx_g, kidx_b, n_sel):
    f32 = jnp.float32
    bsz, t_len, _ = k.shape
    dt = k.dtype
    pos = jnp.arange(t_len)
    cq = _rmsnorm(cq, q_norm_g)
    q = _rope((cq @ w_uq).reshape(bsz, t_len, AT_HEADS, AT_HEAD), pos)
    qi = _rope((cq @ w_iq).reshape(bsz, t_len, IDX_HEADS, IDX_HEAD), pos)
    k = _rope(k.reshape(bsz, t_len, AT_HEADS, AT_HEAD), pos)
    v = v.reshape(bsz, t_len, AT_HEADS, AT_HEAD)
    ki = _rope(_layernorm(kidx, kidx_g, kidx_b)[:, :, None, :], pos)[:, :, 0]
    wq = widx.astype(f32) * (IDX_HEADS ** -0.5) * (IDX_HEAD ** -0.5)
    scale = AT_HEAD ** -0.5
    nb = -(-t_len // Q_BLOCK)
    t_pad = nb * Q_BLOCK
    padq = lambda z: jnp.pad(z, ((0, 0), (0, t_pad - t_len)) + ((0, 0),) * (z.ndim - 2))
    blk = lambda z: jnp.moveaxis(padq(z).reshape((bsz, nb, Q_BLOCK) + z.shape[2:]), 1, 0)
    q_b, qi_b, w_b = blk(q), blk(qi), blk(wq)
    pos_b = jnp.arange(t_pad).reshape(nb, Q_BLOCK)
    gather = jax.vmap(lambda kb, ib: kb[ib])

    def one_block(args):
        qq, qqi, ww, tq = args
        sc = jax.nn.relu(jnp.einsum('bqhd,bsd->bqhs', qqi, ki).astype(f32))
        sc = jnp.einsum('bqhs,bqh->bqs', sc, ww)
        causal = pos[None, :] <= tq[:, None]
        sc = jnp.where(causal[None], sc, -jnp.inf)
        _, sel = lax.top_k(sc, n_sel)
        valid = sel <= tq[None, :, None]
        k_sel = gather(k, sel)
        v_sel = gather(v, sel)
        lg = jnp.einsum('bqhd,bqkhd->bqhk', qq, k_sel).astype(f32) * scale
        lg = jnp.where(valid[:, :, None, :], lg, -jnp.inf)
        p = jax.nn.softmax(lg, axis=-1)
        return jnp.einsum('bqhk,bqkhd->bqhd', p.astype(dt), v_sel)

    o = lax.map(one_block, (q_b, qi_b, w_b, pos_b))
    o = jnp.moveaxis(o, 0, 1).reshape(bsz, t_pad, AT_WIDTH)[:, :t_len]
    return o.astype(dt)


def pool_mix(p, w_pool, pool_scale):
    f32 = jnp.float32
    bsz, t_len, _ = p.shape
    pf = p.astype(f32)
    c = jnp.pad(jnp.cumsum(pf, axis=1), ((0, 0), (1, 0), (0, 0)))
    t = jnp.arange(t_len)
    outs = []
    for gi, win in enumerate(POOL_WINDOWS):
        cg = c[..., gi * POOL_GROUP:(gi + 1) * POOL_GROUP]
        lo = jnp.maximum(t + 1 - win, 0)
        cnt = (t + 1 - lo).astype(f32)
        outs.append((cg[:, 1:] - cg[:, lo]) / cnt[None, :, None])
    pooled = (jnp.concatenate(outs, axis=-1) - pf).reshape(bsz, t_len, len(POOL_WINDOWS), POOL_GROUP)
    y = jnp.einsum('btgc,gcd->btgd', pooled, w_pool.astype(f32)).reshape(bsz, t_len, POOL_WIDTH)
    return (y * pool_scale).astype(p.dtype)


def hier_moe(x2, wg, bg, we, be, w1, w3, w2):
    f32 = jnp.float32
    n, d = x2.shape
    g_prob = jax.nn.softmax((x2 @ wg).astype(f32) + bg, axis=-1)
    g_val, g_idx = lax.top_k(g_prob, 1)
    e_logits = ((x2 @ we).astype(f32) + be).reshape(n, N_GROUPS, EXPERTS_PER_GROUP)
    e_logits = e_logits[jnp.arange(n), g_idx[:, 0]]
    e_val, e_idx = lax.top_k(jax.nn.softmax(e_logits, axis=-1), TOP_K)
    gate = g_val * e_val / jnp.sum(e_val, axis=-1, keepdims=True)
    expert = g_idx * EXPERTS_PER_GROUP + e_idx
    a_tot = n * TOP_K
    e_flat = expert.reshape(-1)
    tok_flat = jnp.repeat(jnp.arange(n), TOP_K)
    order = jnp.argsort(e_flat)
    se, stok, sgate = e_flat[order], tok_flat[order], gate.reshape(-1)[order]
    counts = jnp.bincount(e_flat, length=N_EXPERTS)
    starts = jnp.cumsum(counts) - counts
    pcounts = (counts + MOE_BLOCK - 1) // MOE_BLOCK * MOE_BLOCK
    pends = jnp.cumsum(pcounts)
    pstarts = pends - pcounts
    dest = pstarts[se] + (jnp.arange(a_tot) - starts[se])
    p_rows = -(-(a_tot + N_EXPERTS * (MOE_BLOCK - 1)) // MOE_BLOCK) * MOE_BLOCK
    nb = p_rows // MOE_BLOCK
    row_tok = jnp.full((p_rows,), n, jnp.int32).at[dest].set(stok.astype(jnp.int32))
    row_gate = jnp.zeros((p_rows,), f32).at[dest].set(sgate)
    blk_e = jnp.minimum(jnp.searchsorted(pends, jnp.arange(nb) * MOE_BLOCK, side='right'), N_EXPERTS - 1)
    xpad = jnp.concatenate([x2, jnp.zeros((1, d), x2.dtype)], axis=0)

    def expert_block(args):
        rows, e = args
        xb = xpad[rows]
        h = jax.nn.silu(xb @ w1[e]) * (xb @ w3[e])
        return h @ w2[e]

    yb = lax.map(expert_block, (row_tok.reshape(nb, MOE_BLOCK), blk_e)).reshape(p_rows, d)
    out = jnp.zeros((n + 1, d), x2.dtype).at[row_tok].add(yb * row_gate[:, None].astype(x2.dtype))
    return out[:n]


def setup_inputs(seed: int = 0) -> dict:
    key = jax.random.key(seed)
    ks = jax.random.split(key, 40)
    f32 = jnp.float32
    L = DEPTH
    nrm = lambda k, shape, s: jax.random.normal(k, shape, f32) * s
    return {
        "x": nrm(ks[0], (BATCH, SEQ, D_MODEL), 1.0),
        "meta": nrm(ks[1], (N_META, D_MODEL), 1.0),
        "w_in": nrm(ks[2], (L, D_MODEL, D_IN), D_MODEL ** -0.5),
        "rw_mu": jax.random.uniform(ks[3], (L, RW_COLS), f32),
        "rw_w2": nrm(ks[4], (L, RW_DECAY_LORA, RW_WIDTH), 0.5 * RW_DECAY_LORA ** -0.5),
        "rw_w0": jax.random.uniform(ks[5], (L, RW_WIDTH), f32, -5.0, 1.0),
        "rw_a2": nrm(ks[6], (L, RW_AAA_LORA, RW_WIDTH), 0.5 * RW_AAA_LORA ** -0.5),
        "rw_a0": nrm(ks[7], (L, RW_WIDTH), 0.3),
        "rw_g2": nrm(ks[8], (L, RW_GATE_LORA, RW_WIDTH), RW_GATE_LORA ** -0.5),
        "rw_kk": 0.85 + nrm(ks[9], (L, RW_WIDTH), 0.05),
        "rw_ka": 1.0 + nrm(ks[10], (L, RW_WIDTH), 0.05),
        "rw_rk": nrm(ks[11], (L, RW_HEADS, RW_HEAD), 0.1),
        "rw_gn_g": 1.0 + nrm(ks[12], (L, RW_WIDTH), 0.02),
        "rw_gn_b": nrm(ks[13], (L, RW_WIDTH), 0.02),
        "at_qnorm_g": 1.0 + nrm(ks[14], (L, AT_Q_LORA), 0.02),
        "at_w_uq": nrm(ks[15], (L, AT_Q_LORA, AT_WIDTH), AT_Q_LORA ** -0.5),
        "at_w_iq": nrm(ks[16], (L, AT_Q_LORA, IDX_HEADS * IDX_HEAD), AT_Q_LORA ** -0.5),
        "at_kidx_g": 1.0 + nrm(ks[17], (L, IDX_HEAD), 0.02),
        "at_kidx_b": nrm(ks[18], (L, IDX_HEAD), 0.02),
        "pool_w": nrm(ks[19], (L, len(POOL_WINDOWS), POOL_GROUP, POOL_GROUP), POOL_GROUP ** -0.5),
        "pool_scale": 1.0 + nrm(ks[20], (L, POOL_WIDTH), 0.02),
        "w_out": nrm(ks[21], (L, D_MIX, D_MODEL), DN_BETA * D_MIX ** -0.5),
        "ln1_g": 1.0 + nrm(ks[22], (L, D_MODEL), 0.02),
        "ln1_b": nrm(ks[23], (L, D_MODEL), 0.02),
        "router_g_w": nrm(ks[24], (L, D_MODEL, N_GROUPS), D_MODEL ** -0.5),
        "router_g_b": nrm(ks[25], (L, N_GROUPS), 0.01),
        "router_e_w": nrm(ks[26], (L, D_MODEL, N_EXPERTS), D_MODEL ** -0.5),
        "router_e_b": nrm(ks[27], (L, N_EXPERTS), 0.01),
        "exp_w1": nrm(ks[28], (L, N_EXPERTS, D_MODEL, D_EXPERT), D_MODEL ** -0.5),
        "exp_w3": nrm(ks[29], (L, N_EXPERTS, D_MODEL, D_EXPERT), D_MODEL ** -0.5),
        "exp_w2": nrm(ks[30], (L, N_EXPERTS, D_EXPERT, D_MODEL), DN_BETA * D_EXPERT ** -0.5),
        "ln2_g": 1.0 + nrm(ks[31], (L, D_MODEL), 0.02),
        "ln2_b": nrm(ks[32], (L, D_MODEL), 0.02),
    }


def reference(x, meta, w_in, rw_mu, rw_w2, rw_w0, rw_a2, rw_a0, rw_g2, rw_kk, rw_ka, rw_rk,
              rw_gn_g, rw_gn_b, at_qnorm_g, at_w_uq, at_w_iq, at_kidx_g, at_kidx_b, pool_w,
              pool_scale, w_out, ln1_g, ln1_b, router_g_w, router_g_b, router_e_w, router_e_b,
              exp_w1, exp_w3, exp_w2, ln2_g, ln2_b):
    bsz, seq_len, d = x.shape
    n_sel = min(TOPK_MAX, seq_len // 4)
    h = jnp.concatenate([jnp.broadcast_to(meta.astype(x.dtype)[None], (bsz, N_META, d)), x], axis=1)
    t_len = h.shape[1]
    for l in range(DEPTH):
        proj = h @ w_in[l]
        p_rw, p_rest = proj[..., :RW_COLS], proj[..., RW_COLS:]
        prev = jnp.pad(p_rw, ((0, 0), (1, 0), (0, 0)))[:, :-1]
        p_rw = p_rw + (prev - p_rw) * rw_mu[l]
        r, k, v, dw, da, dg = _split(p_rw, RW_SIZES)
        cq, ka, va, kidx, widx, pin = _split(p_rest, OTHER_SIZES)
        y_rw = rwkv7_mix(r, k, v, dw, da, dg, rw_w2[l], rw_w0[l], rw_a2[l], rw_a0[l], rw_g2[l],
                         rw_kk[l], rw_ka[l], rw_rk[l], rw_gn_g[l], rw_gn_b[l])
        y_at = dsa_attention(cq, ka, va, kidx, widx, at_qnorm_g[l], at_w_uq[l], at_w_iq[l],
                             at_kidx_g[l], at_kidx_b[l], n_sel)
        y_pl = pool_mix(pin, pool_w[l], pool_scale[l])
        mix = jnp.concatenate([y_rw, y_at, y_pl], axis=-1) @ w_out[l]
        h = _layernorm(DN_ALPHA * h + mix, ln1_g[l], ln1_b[l])
        ff = hier_moe(h.reshape(bsz * t_len, d), router_g_w[l], router_g_b[l], router_e_w[l],
                      router_e_b[l], exp_w1[l], exp_w3[l], exp_w2[l]).reshape(bsz, t_len, d)
        h = _layernorm(DN_ALPHA * h + ff, ln2_g[l], ln2_b[l])
    return h[:, N_META:]
```

```python
import functools
import math

import jax
import jax.numpy as jnp
import numpy as np
from jax import lax
from jax.experimental import pallas as pl
from jax.experimental.pallas import tpu as pltpu

F32 = jnp.float32
BF16 = jnp.bfloat16
I32 = jnp.int32

LANE = 128
ROW_TILE = 128
RW_HEAD = 64
AT_HEAD = 128
TOPK_MAX = 256
ROPE_THETA = 10000.0
POOL_WINDOWS = (2, 4, 8, 16)
POOL_HALO = 16
GN_EPS = 64e-5
LN_EPS = 1e-5
RW_CHUNK = 64
MOE_BM = 128
TOP_K = 2
INT_MIN = -(2 ** 31)
VMEM_LIMIT = 56 * 1024 * 1024

HIGHEST = lax.Precision.HIGHEST
NN = (((1,), (0,)), ((), ()))
NT = (((1,), (1,)), ((), ()))
TN = (((0,), (0,)), ((), ()))


def _dot(a, b, dims=NN, precision=None):
    return lax.dot_general(a, b, dims, precision=precision, preferred_element_type=F32)


def _sigmoid(x):
    return 1.0 / (1.0 + jnp.exp(-x))


def _cparams(sem):
    return pltpu.CompilerParams(dimension_semantics=sem, vmem_limit_bytes=VMEM_LIMIT)


def _round_up(x, m):
    return -(-x // m) * m


def _pick_tile(n, candidates):
    for c in candidates:
        if n % c == 0:
            return c
    return n


def _layout(sizes):
    pw = {k: _round_up(v, LANE) for k, v in sizes.items()}
    order = sorted(sizes, key=lambda k: -pw[k])
    off, gaps, cur = {}, [], 0
    for k in order:
        w = pw[k]
        placed = False
        for gi, (g0, g1) in enumerate(gaps):
            s = _round_up(g0, w)
            if s + w <= g1:
                off[k] = s
                new = [(g0, s), (s + w, g1)]
                gaps[gi:gi + 1] = [g for g in new if g[1] > g[0]]
                placed = True
                break
        if not placed:
            s = _round_up(cur, w)
            if s > cur:
                gaps.append((cur, s))
            off[k] = s
            cur = s + w
    return off, pw, cur


def _scatter_cols(w, src_sizes, off, total):
    names = list(src_sizes)
    starts = np.cumsum([0] + [src_sizes[n] for n in names])
    pieces, cur = [], 0
    for n in sorted(names, key=lambda n: off[n]):
        i = names.index(n)
        if off[n] > cur:
            pieces.append(jnp.zeros(w.shape[:-1] + (off[n] - cur,), w.dtype))
        pieces.append(w[..., int(starts[i]):int(starts[i + 1])])
        cur = off[n] + src_sizes[n]
    if total > cur:
        pieces.append(jnp.zeros(w.shape[:-1] + (total - cur,), w.dtype))
    return jnp.concatenate(pieces, axis=-1)


def _mm_kernel(x_ref, w_ref, o_ref):
    o_ref[...] = _dot(x_ref[...], w_ref[...]).astype(o_ref.dtype)


def _matmul(x, w, out_dtype):
    r, k = x.shape
    n = w.shape[1]
    tm = _pick_tile(r, (768, 512, 384, 256, 128))
    tn = _pick_tile(n, (1024, 512, 256, 128))
    return pl.pallas_call(
        _mm_kernel,
        grid=(n // tn, r // tm),
        in_specs=[pl.BlockSpec((tm, k), lambda j, i: (i, 0)),
                  pl.BlockSpec((k, tn), lambda j, i: (0, j))],
        out_specs=pl.BlockSpec((tm, tn), lambda j, i: (i, j)),
        out_shape=jax.ShapeDtypeStruct((r, n), out_dtype),
        compiler_params=_cparams(("arbitrary", "arbitrary")),
    )(x, w)


def _rwkv_prep_kernel(r_ref, k_ref, v_ref, dw_ref, da_ref, dg_ref,
                      rh_ref, kh_ref, vh_ref, dwh_ref, dah_ref, dgh_ref,
                      mur_ref, muk_ref, muv_ref, mudw_ref, muda_ref, mudg_ref,
                      w2_ref, w0_ref, a2_ref, a0_ref, g2_ref, kkp_ref, kap_ref, hsum_ref,
                      r_o, k_o, v_o, kk_o, a_o, lw_o, g_o):
    first = pl.program_id(1) == 0

    def shift(x_ref, h_ref, mu_ref):
        x = x_ref[...]
        prev_row = jnp.where(first, 0.0, h_ref[7:8, :])
        row = lax.broadcasted_iota(I32, x.shape, 0)
        prev = jnp.where(row == 0, prev_row, pltpu.roll(x, 1, axis=0))
        return x + (prev - x) * mu_ref[...]

    r = shift(r_ref, rh_ref, mur_ref)
    k = shift(k_ref, kh_ref, muk_ref)
    v = shift(v_ref, vh_ref, muv_ref)
    dw = shift(dw_ref, dwh_ref, mudw_ref)
    da = shift(da_ref, dah_ref, muda_ref)
    dg = shift(dg_ref, dgh_ref, mudg_ref)

    wl = w0_ref[...] + _dot(jnp.tanh(dw).astype(BF16), w2_ref[...])
    neg = -wl
    softplus = jnp.maximum(neg, 0.0) + jnp.log(1.0 + jnp.exp(-jnp.abs(neg)))
    w = -softplus - 0.5
    a = _sigmoid(a0_ref[...] + _dot(da.astype(BF16), a2_ref[...]))
    g = _dot(_sigmoid(dg).astype(BF16), g2_ref[...])
    kk = k * kkp_ref[...]
    kk2 = kk * kk
    hsum = hsum_ref[...]
    for c in range(kk.shape[1] // LANE):
        sl = slice(c * LANE, (c + 1) * LANE)
        ss = _dot(kk2[:, sl], hsum, precision=HIGHEST)
        kk_o[:, sl] = kk[:, sl] / jnp.maximum(jnp.sqrt(ss), 1e-12)
    r_o[...] = r
    k_o[...] = k * (1.0 + (a - 1.0) * kap_ref[...])
    v_o[...] = v
    a_o[...] = a
    lw_o[...] = -jnp.exp(w)
    g_o[...] = g


def _rwkv_prep(proj, bsz, p_len, off, pw, mu, w2, w0, a2, a0, g2, kkp, kap):
    rw = w2.shape[1]
    tm = _pick_tile(p_len, (256, 128))
    nb = p_len // tm
    names = ("r", "k", "v", "dw", "da", "dg")

    def cur_spec(n):
        w, c = pw[n], off[n] // pw[n]
        return pl.BlockSpec((tm, w), lambda b, i, c=c: (b * nb + i, c))

    def halo_spec(n):
        w, c = pw[n], off[n] // pw[n]
        return pl.BlockSpec((8, w), lambda b, i, c=c: (jnp.maximum((b * nb + i) * (tm // 8) - 1, 0), c))

    def full(a):
        return pl.BlockSpec(a.shape, lambda b, i: (0,) * a.ndim)

    hsum = (np.arange(LANE)[:, None] // RW_HEAD == np.arange(LANE)[None, :] // RW_HEAD).astype(np.float32)
    hsum = jnp.asarray(hsum)
    params = [mu[n] for n in names] + [w2, w0, a2, a0, g2, kkp, kap, hsum]
    out_spec = pl.BlockSpec((tm, rw), lambda b, i: (b * nb + i, 0))
    out_sd = jax.ShapeDtypeStruct((bsz * p_len, rw), F32)
    return pl.pallas_call(
        _rwkv_prep_kernel,
        grid=(bsz, nb),
        in_specs=[cur_spec(n) for n in names] + [halo_spec(n) for n in names] + [full(a) for a in params],
        out_specs=[out_spec] * 7,
        out_shape=[out_sd] * 7,
        compiler_params=_cparams(("arbitrary", "arbitrary")),
    )(*([proj] * 12), *params)


def _rwkv_chunk(r, k, v, kk, a, lw, z):
    c = r.shape[0]
    ti = lax.broadcasted_iota(I32, (c, c), 0)
    si = lax.broadcasted_iota(I32, (c, c), 1)
    strict = si < ti
    incl = si <= ti
    hp = functools.partial(_dot, precision=HIGHEST)

    cl = hp(incl.astype(F32), lw)
    cl_last = cl[c - 1:c, :]
    e_p = jnp.exp(cl)
    e_n = jnp.exp(-cl)
    e_nc = jnp.exp(cl_last - cl)
    kka = kk * a
    at = -kk * jnp.exp(cl - lw)
    bt = kka * e_n
    kt = k * e_n
    rt = r * e_p
    zero = jnp.zeros((c, c), F32)
    l_ba = jnp.where(strict, hp(at, bt, NT), zero)
    l_ka = jnp.where(strict, hp(at, kt, NT), zero)
    m_br = jnp.where(incl, hp(rt, bt, NT), zero)
    m_kr = jnp.where(incl, hp(rt, kt, NT), zero)

    same16 = (ti // 16) == (si // 16)
    same32 = (ti // 32) == (si // 32)
    eye = (ti == si).astype(F32)
    l16 = jnp.where(same16, l_ba, zero)
    t = eye + l16
    lp = l16
    for _ in range(3):
        lp = hp(lp, lp)
        t = t + hp(lp, t)
    size = 32
    same_lo = same16
    while size <= c:
        same_hi = (ti // size) == (si // size)
        l_off = jnp.where(same_hi, jnp.where(same_lo, zero, l_ba), zero)
        t = t + hp(hp(t, l_off), t)
        same_lo = same_hi
        size *= 2
    del same32

    a_z = hp(t, at)
    w_u = hp(t, hp(l_ka, v))
    q_z = rt + hp(m_br, a_z)
    y = hp(q_z, z) + hp(m_br, w_u) + hp(m_kr, v)
    bc = kka * e_nc
    kc = k * e_nc
    p_col = jnp.exp(hp(lw, jnp.ones((c, z.shape[1]), F32), TN))
    z_new = p_col * z + hp(hp(bc, a_z, TN), z) + hp(bc, w_u, TN) + hp(kc, v, TN)
    return y, z_new


def _rwkv_scan_kernel(r_ref, k_ref, v_ref, kk_ref, a_ref, lw_ref, g_ref, rk_ref, gg_ref, gb_ref,
                      o_ref, z_ref):
    @pl.when(pl.program_id(2) == 0)
    def _():
        z_ref[...] = jnp.zeros_like(z_ref)

    for hh in range(LANE // RW_HEAD):
        sl = slice(hh * RW_HEAD, (hh + 1) * RW_HEAD)
        r, k, v = r_ref[:, sl], k_ref[:, sl], v_ref[:, sl]
        y, z_new = _rwkv_chunk(r, k, v, kk_ref[:, sl], a_ref[:, sl], lw_ref[:, sl], z_ref[hh])
        z_ref[hh] = z_new
        mu = jnp.mean(y, axis=-1, keepdims=True)
        var = jnp.mean(jnp.square(y - mu), axis=-1, keepdims=True)
        yn = (y - mu) * lax.rsqrt(var + GN_EPS) * gg_ref[:, sl] + gb_ref[:, sl]
        bonus = jnp.sum(r * k * rk_ref[:, sl], axis=-1, keepdims=True) * v
        o_ref[:, sl] = ((yn + bonus) * g_ref[:, sl]).astype(o_ref.dtype)


def _rwkv_scan(seqs, rk, gn_g, gn_b, bsz, p_len):
    rw = rk.shape[1]
    npair = rw // LANE
    nch = p_len // RW_CHUNK
    seqs = [s.reshape(bsz, p_len, rw) for s in seqs]
    seq_spec = pl.BlockSpec((None, RW_CHUNK, LANE), lambda b, p, c: (b, c, p))
    par_spec = pl.BlockSpec((1, LANE), lambda b, p, c: (0, p))
    out = pl.pallas_call(
        _rwkv_scan_kernel,
        grid=(bsz, npair, nch),
        in_specs=[seq_spec] * 7 + [par_spec] * 3,
        out_specs=seq_spec,
        out_shape=jax.ShapeDtypeStruct((bsz, p_len, rw), BF16),
        scratch_shapes=[pltpu.VMEM((LANE // RW_HEAD, RW_HEAD, RW_HEAD), F32)],
        compiler_params=_cparams(("arbitrary", "arbitrary", "arbitrary")),
    )(*seqs, rk, gn_g, gn_b)
    return out.reshape(bsz * p_len, rw)


def _rope_tiles(x, cosf, sinf):
    outs = []
    for c in range(x.shape[1] // LANE):
        xt = x[:, c * LANE:(c + 1) * LANE]
        outs.append(xt * cosf + pltpu.roll(xt, LANE // 2, axis=1) * sinf)
    return outs


def _dsa_prep_kernel(cq_ref, ka_ref, kidx_ref, widx_ref, cos_ref, sin_ref, qg_ref, wuq_ref, wiq_ref,
                     kg_ref, kb_ref, q_o, qi_o, k_o, ki_o, wq_o, *, wq_scale):
    cosf, sinf = cos_ref[...], sin_ref[...]
    cq = cq_ref[...]
    cqn = cq * lax.rsqrt(jnp.mean(jnp.square(cq), axis=-1, keepdims=True) + 1e-6) * qg_ref[...]
    cqb = cqn.astype(BF16)
    for c, t in enumerate(_rope_tiles(_dot(cqb, wuq_ref[...]), cosf, sinf)):
        q_o[:, c * LANE:(c + 1) * LANE] = t.astype(q_o.dtype)
    for c, t in enumerate(_rope_tiles(_dot(cqb, wiq_ref[...]), cosf, sinf)):
        qi_o[:, c * LANE:(c + 1) * LANE] = t.astype(qi_o.dtype)
    for c, t in enumerate(_rope_tiles(ka_ref[...], cosf, sinf)):
        k_o[:, c * LANE:(c + 1) * LANE] = t.astype(k_o.dtype)
    kx = kidx_ref[...]
    mu = jnp.mean(kx, axis=-1, keepdims=True)
    var = jnp.mean(jnp.square(kx - mu), axis=-1, keepdims=True)
    kn = (kx - mu) * lax.rsqrt(var + LN_EPS) * kg_ref[...] + kb_ref[...]
    ki_o[...] = _rope_tiles(kn, cosf, sinf)[0].astype(ki_o.dtype)
    wq_o[...] = widx_ref[...] * wq_scale


def _dsa_prep(proj, bsz, p_len, off, pw, cosf, sinf, qg, wuq, wiq, kg, kb, idx_heads):
    tm = _pick_tile(p_len, (256, 128))
    nb = p_len // tm
    rows = bsz * p_len
    aw, iw = wuq.shape[1], wiq.shape[1]

    def seg(n):
        w, c = pw[n], off[n] // pw[n]
        return pl.BlockSpec((tm, w), lambda i, c=c: (i, c))

    def full(a):
        return pl.BlockSpec(a.shape, lambda i: (0,) * a.ndim)

    pos_spec = pl.BlockSpec((tm, LANE), lambda i: (i % nb, 0))
    row_spec = lambda w: pl.BlockSpec((tm, w), lambda i: (i, 0))
    kern = functools.partial(_dsa_prep_kernel, wq_scale=float(idx_heads ** -0.5 * AT_HEAD ** -0.5))
    return pl.pallas_call(
        kern,
        grid=(rows // tm,),
        in_specs=[seg("cq"), seg("ka"), seg("kidx"), seg("widx"), pos_spec, pos_spec,
                  full(qg), full(wuq), full(wiq), full(kg), full(kb)],
        out_specs=[row_spec(aw), row_spec(iw), row_spec(aw), row_spec(LANE), row_spec(LANE)],
        out_shape=[jax.ShapeDtypeStruct((rows, aw), BF16), jax.ShapeDtypeStruct((rows, iw), BF16),
                   jax.ShapeDtypeStruct((rows, aw), BF16), jax.ShapeDtypeStruct((rows, LANE), BF16),
                   jax.ShapeDtypeStruct((rows, LANE), F32)],
        compiler_params=_cparams(("arbitrary",)),
    )(proj, proj, proj, proj, cosf, sinf, qg, wuq, wiq, kg, kb)


def _topk_mask(sc, causal, n_sel):
    tq, s_len = sc.shape
    bits = lax.bitcast_convert_type(jnp.where(sc == 0.0, 0.0, sc), I32)
    key = jnp.where(bits < 0, bits ^ jnp.int32(0x7FFFFFFF), bits)
    key = jnp.where(causal, key, jnp.int32(INT_MIN))
    one = jnp.ones((tq, s_len), I32)
    zero = jnp.zeros((tq, s_len), I32)

    def count_ge(th):
        return jnp.sum(jnp.where(key >= th, one, zero), axis=1, keepdims=True)

    tau = jnp.where(count_ge(jnp.zeros((tq, 1), I32)) >= n_sel, jnp.int32(0), jnp.int32(INT_MIN))

    def tau_step(i, tau):
        cand = tau | jnp.left_shift(jnp.int32(1), 30 - i)
        return jnp.where(count_ge(cand) >= n_sel, cand, tau)

    tau = lax.fori_loop(0, 31, tau_step, tau)
    gt = key > tau
    eq = key == tau
    need = n_sel - jnp.sum(jnp.where(gt, one, zero), axis=1, keepdims=True)
    idx = lax.broadcasted_iota(I32, (tq, s_len), 1)
    nbits = max(1, (s_len - 1).bit_length())

    def cut_step(i, lo):
        cand = lo | jnp.left_shift(jnp.int32(1), nbits - 1 - i)
        cnt = jnp.sum(jnp.where(eq, jnp.where(idx < cand, one, zero), zero), axis=1, keepdims=True)
        return jnp.where(cnt < need, cand, lo)

    cut = lax.fori_loop(0, nbits, cut_step, jnp.zeros((tq, 1), I32))
    sel = jnp.where(gt, one, jnp.where(eq, jnp.where(idx <= cut, one, zero), zero))
    return jnp.where(causal, sel, zero) > 0


def _dsa_attn_kernel(q_ref, qi_ref, wq_ref, k_ref, v_ref, ki_ref, o_ref, *, n_sel, idx_heads, scale):
    tq = q_ref.shape[0]
    s_len = k_ref.shape[0]
    t0 = pl.program_id(1) * tq
    ki = ki_ref[...]
    wq = wq_ref[...]
    sc = jnp.zeros((tq, s_len), F32)
    for h in range(idx_heads):
        s_h = _dot(qi_ref[:, h * LANE:(h + 1) * LANE], ki, NT)
        sc = sc + jnp.maximum(s_h, 0.0) * wq[:, h:h + 1]
    qpos = t0 + lax.broadcasted_iota(I32, (tq, s_len), 0)
    kpos = lax.broadcasted_iota(I32, (tq, s_len), 1)
    mask = _topk_mask(sc, kpos <= qpos, n_sel)
    bias = jnp.where(mask, 0.0, -jnp.inf)
    for h in range(q_ref.shape[1] // AT_HEAD):
        sl = slice(h * AT_HEAD, (h + 1) * AT_HEAD)
        lg = _dot(q_ref[:, sl], k_ref[:, sl], NT) * scale + bias
        p = jnp.exp(lg - jnp.max(lg, axis=1, keepdims=True))
        den = jnp.sum(p, axis=1, keepdims=True)
        o_ref[:, sl] = (_dot(p.astype(BF16), v_ref[:, sl]) / den).astype(o_ref.dtype)


def _dsa_attn(q, qi, wq, k, v, ki, bsz, p_len, n_sel, idx_heads):
    aw, iw = q.shape[1], qi.shape[1]
    tq = ROW_TILE
    nq = p_len // tq
    r3 = lambda a: a.reshape(bsz, p_len, a.shape[1])
    qspec = lambda w: pl.BlockSpec((None, tq, w), lambda b, i: (b, i, 0))
    kspec = lambda w: pl.BlockSpec((None, p_len, w), lambda b, i: (b, 0, 0), pipeline_mode=pl.Buffered(1))
    kern = functools.partial(_dsa_attn_kernel, n_sel=n_sel, idx_heads=idx_heads, scale=float(AT_HEAD ** -0.5))
    out = pl.pallas_call(
        kern,
        grid=(bsz, nq),
        in_specs=[qspec(aw), qspec(iw), qspec(LANE), kspec(aw), kspec(aw), kspec(LANE)],
        out_specs=qspec(aw),
        out_shape=jax.ShapeDtypeStruct((bsz, p_len, aw), BF16),
        compiler_params=_cparams(("arbitrary", "arbitrary")),
    )(r3(q), r3(qi), r3(wq), r3(k), r3(v), r3(ki))
    return out.reshape(bsz * p_len, aw)


def _pool_kernel(p_ref, h_ref, w_ref, s_ref, o_ref):
    tm, width = p_ref.shape
    grp = width // len(POOL_WINDOWS)
    t0 = pl.program_id(1) * tm
    x = p_ref[...]
    halo = jnp.where(pl.program_id(1) == 0, 0.0, h_ref[...])
    ext = jnp.concatenate([halo, x], axis=0)
    tpos = t0 + lax.broadcasted_iota(I32, (tm, 1), 0)
    acc = ext
    have = 1
    for gi, win in enumerate(POOL_WINDOWS):
        while have < win:
            shifted = jnp.concatenate([jnp.zeros((have, width), F32), acc[:-have]], axis=0)
            acc = acc + shifted
            have *= 2
        sl = slice(gi * grp, (gi + 1) * grp)
        cnt = jnp.minimum(tpos + 1, win).astype(F32)
        pooled = acc[POOL_HALO:, sl] / cnt - x[:, sl]
        y = _dot(pooled.astype(BF16), w_ref[gi])
        o_ref[:, sl] = (y * s_ref[:, sl]).astype(o_ref.dtype)


def _pool_mix(proj, bsz, p_len, off, pw, w_pool, scale):
    width = pw["pin"]
    tm = _pick_tile(p_len, (256, 128))
    nb = p_len // tm
    c = off["pin"] // width
    hb = tm // POOL_HALO
    return pl.pallas_call(
        _pool_kernel,
        grid=(bsz, nb),
        in_specs=[pl.BlockSpec((tm, width), lambda b, i: (b * nb + i, c)),
                  pl.BlockSpec((POOL_HALO, width), lambda b, i: (jnp.maximum((b * nb + i) * hb - 1, 0), c)),
                  pl.BlockSpec(w_pool.shape, lambda b, i: (0, 0, 0)),
                  pl.BlockSpec(scale.shape, lambda b, i: (0, 0))],
        out_specs=pl.BlockSpec((tm, width), lambda b, i: (b * nb + i, 0)),
        out_shape=jax.ShapeDtypeStruct((bsz * p_len, width), BF16),
        compiler_params=_cparams(("arbitrary", "arbitrary")),
    )(proj, proj, w_pool, scale)


def _layernorm_rows(x, g, b):
    mu = jnp.mean(x, axis=-1, keepdims=True)
    var = jnp.mean(jnp.square(x - mu), axis=-1, keepdims=True)
    return (x - mu) * lax.rsqrt(var + LN_EPS) * g + b


def _outproj_ln_kernel(x_ref, w_ref, h_ref, g_ref, b_ref, o_ref, ob_ref, acc_ref, *, alpha):
    kk = pl.program_id(1)

    @pl.when(kk == 0)
    def _():
        acc_ref[...] = jnp.zeros_like(acc_ref)

    acc_ref[...] += _dot(x_ref[...], w_ref[...])

    @pl.when(kk == pl.num_programs(1) - 1)
    def _():
        y = _layernorm_rows(alpha * h_ref[...] + acc_ref[...], g_ref[...], b_ref[...])
        o_ref[...] = y
        ob_ref[...] = y.astype(ob_ref.dtype)


def _outproj_ln(mix, w_out, h, g, b, alpha):
    r, k = mix.shape
    d = w_out.shape[1]
    tm = _pick_tile(r, (256, 128))
    tk = _pick_tile(k, (512, 256, 128))
    kern = functools.partial(_outproj_ln_kernel, alpha=alpha)
    return pl.pallas_call(
        kern,
        grid=(r // tm, k // tk),
        in_specs=[pl.BlockSpec((tm, tk), lambda i, j: (i, j)),
                  pl.BlockSpec((tk, d), lambda i, j: (j, 0)),
                  pl.BlockSpec((tm, d), lambda i, j: (i, 0)),
                  pl.BlockSpec((1, d), lambda i, j: (0, 0)),
                  pl.BlockSpec((1, d), lambda i, j: (0, 0))],
        out_specs=[pl.BlockSpec((tm, d), lambda i, j: (i, 0))] * 2,
        out_shape=[jax.ShapeDtypeStruct((r, d), F32), jax.ShapeDtypeStruct((r, d), BF16)],
        scratch_shapes=[pltpu.VMEM((tm, d), F32)],
        compiler_params=_cparams(("arbitrary", "arbitrary")),
    )(mix, w_out, h, g, b)


def _router_kernel(x_ref, w_ref, b_ref, e_o, g_o, *, n_groups, per_group):
    logits = _dot(x_ref[...], w_ref[...]) + b_ref[...]
    lane = lax.broadcasted_iota(I32, logits.shape, 1)
    big = jnp.int32(LANE)
    ninf = -jnp.inf
    gl = jnp.where(lane < n_groups, logits, ninf)
    ge = jnp.exp(gl - jnp.max(gl, axis=1, keepdims=True))
    gp = ge / jnp.sum(ge, axis=1, keepdims=True)
    g_val = jnp.max(gp, axis=1, keepdims=True)
    g_idx = jnp.min(jnp.where(gp == g_val, lane, big), axis=1, keepdims=True)
    lane_grp = jnp.where(lane >= n_groups, (lane - n_groups) // per_group, -1)
    in_grp = lane_grp == g_idx
    el = jnp.where(in_grp, logits, ninf)
    ee = jnp.exp(el - jnp.max(el, axis=1, keepdims=True))
    ep = jnp.where(in_grp, ee / jnp.sum(ee, axis=1, keepdims=True), -1.0)
    v1 = jnp.max(ep, axis=1, keepdims=True)
    i1 = jnp.min(jnp.where(ep == v1, lane, big), axis=1, keepdims=True)
    ep2 = jnp.where(lane == i1, -1.0, ep)
    v2 = jnp.max(ep2, axis=1, keepdims=True)
    i2 = jnp.min(jnp.where(ep2 == v2, lane, big), axis=1, keepdims=True)
    tot = v1 + v2
    e_o[...] = jnp.where(lane == 0, i1 - n_groups, jnp.where(lane == 1, i2 - n_groups, 0))
    g_o[...] = jnp.where(lane == 0, g_val * v1 / tot, jnp.where(lane == 1, g_val * v2 / tot, 0.0))


def _router(hb, w, b, n_groups, per_group):
    r, d = hb.shape
    tm = _pick_tile(r, (256, 128))
    kern = functools.partial(_router_kernel, n_groups=n_groups, per_group=per_group)
    return pl.pallas_call(
        kern,
        grid=(r // tm,),
        in_specs=[pl.BlockSpec((tm, d), lambda i: (i, 0)),
                  pl.BlockSpec((d, LANE), lambda i: (0, 0)),
                  pl.BlockSpec((1, LANE), lambda i: (0, 0))],
        out_specs=[pl.BlockSpec((tm, LANE), lambda i: (i, 0))] * 2,
        out_shape=[jax.ShapeDtypeStruct((r, LANE), I32), jax.ShapeDtypeStruct((r, LANE), F32)],
        compiler_params=_cparams(("arbitrary",)),
    )(hb, w, b)


def _row_gather(src_hbm, idx_ref, dst, sem, n):
    def issue(j, carry):
        pltpu.make_async_copy(src_hbm.at[pl.ds(idx_ref[0, 0, j], 1), :], dst.at[pl.ds(j, 1), :], sem).start()
        return carry
    lax.fori_loop(0, n, issue, 0)
    pltpu.make_async_copy(src_hbm.at[pl.ds(0, n), :], dst, sem).wait()


def _moe_ffn_kernel(be_ref, nb_ref, tok_ref, h_hbm, w1_ref, w3_ref, w2_ref, y_ref, xbuf, sem):
    del be_ref
    i = pl.program_id(0)

    @pl.when(i < nb_ref[0])
    def _():
        _row_gather(h_hbm, tok_ref, xbuf, sem, xbuf.shape[0])
        x = xbuf[...].astype(BF16)
        h1 = _dot(x, w1_ref[...])
        h3 = _dot(x, w3_ref[...])
        act = h1 * _sigmoid(h1) * h3
        y_ref[...] = _dot(act.astype(BF16), w2_ref[...])

    @pl.when(i >= nb_ref[0])
    def _():
        y_ref[...] = jnp.zeros_like(y_ref)


def _moe_ffn(h, w1, w3, w2, blk_e, n_used, row_tok):
    r, d = h.shape
    nblk = blk_e.shape[0]
    de = w1.shape[2]
    bm = row_tok.shape[2]
    grid_spec = pltpu.PrefetchScalarGridSpec(
        num_scalar_prefetch=2,
        grid=(nblk,),
        in_specs=[pl.BlockSpec((1, 1, bm), lambda i, be, nb: (i, 0, 0), memory_space=pltpu.SMEM),
                  pl.BlockSpec(memory_space=pl.ANY),
                  pl.BlockSpec((None, d, de), lambda i, be, nb: (be[i], 0, 0)),
                  pl.BlockSpec((None, d, de), lambda i, be, nb: (be[i], 0, 0)),
                  pl.BlockSpec((None, de, d), lambda i, be, nb: (be[i], 0, 0))],
        out_specs=pl.BlockSpec((bm, d), lambda i, be, nb: (i, 0)),
        scratch_shapes=[pltpu.VMEM((bm, d), F32), pltpu.SemaphoreType.DMA(())],
    )
    return pl.pallas_call(
        _moe_ffn_kernel,
        grid_spec=grid_spec,
        out_shape=jax.ShapeDtypeStruct((nblk * bm, d), F32),
        compiler_params=_cparams(("arbitrary",)),
    )(blk_e, n_used, row_tok, h, w1, w3, w2)


def _combine_ln_kernel(d0_ref, d1_ref, y_hbm, h_ref, gate_ref, g_ref, b_ref, o_ref, ob_ref, ybuf, sem,
                       *, alpha):
    tm = h_ref.shape[0]
    _row_gather(y_hbm, d0_ref, ybuf.at[0], sem.at[0], tm)
    _row_gather(y_hbm, d1_ref, ybuf.at[1], sem.at[1], tm)
    gate = gate_ref[...]
    ff = ybuf[0] * gate[:, 0:1] + ybuf[1] * gate[:, 1:2]
    y = _layernorm_rows(alpha * h_ref[...] + ff, g_ref[...], b_ref[...])
    o_ref[...] = y
    ob_ref[...] = y.astype(ob_ref.dtype)


def _combine_ln(yb, h, gates, dest0, dest1, g, b, alpha):
    r, d = h.shape
    tm = dest0.shape[2]
    kern = functools.partial(_combine_ln_kernel, alpha=alpha)
    idx_spec = pl.BlockSpec((1, 1, tm), lambda i: (i, 0, 0), memory_space=pltpu.SMEM)
    return pl.pallas_call(
        kern,
        grid=(r // tm,),
        in_specs=[idx_spec, idx_spec,
                  pl.BlockSpec(memory_space=pl.ANY),
                  pl.BlockSpec((tm, d), lambda i: (i, 0)),
                  pl.BlockSpec((tm, LANE), lambda i: (i, 0)),
                  pl.BlockSpec((1, d), lambda i: (0, 0)),
                  pl.BlockSpec((1, d), lambda i: (0, 0))],
        out_specs=[pl.BlockSpec((tm, d), lambda i: (i, 0))] * 2,
        out_shape=[jax.ShapeDtypeStruct((r, d), F32), jax.ShapeDtypeStruct((r, d), BF16)],
        scratch_shapes=[pltpu.VMEM((2, tm, d), F32), pltpu.SemaphoreType.DMA((2,))],
        compiler_params=_cparams(("arbitrary",)),
    )(dest0, dest1, yb, h, gates, g, b)


def _dispatch_plan(eid, n_experts, bm):
    r = eid.shape[0]
    a_tot = r * TOP_K
    e_flat = eid[:, :TOP_K].reshape(a_tot)
    onehot = (e_flat[:, None] == jnp.arange(n_experts, dtype=I32)[None, :]).astype(I32)
    before = jnp.cumsum(onehot, axis=0) - onehot
    rank = jnp.sum(before * onehot, axis=1)
    counts = jnp.sum(onehot, axis=0)
    pcounts = (counts + bm - 1) // bm * bm
    pends = jnp.cumsum(pcounts)
    pstarts = pends - pcounts
    dest = (pstarts[e_flat] + rank).astype(I32)
    nblk = -(-(a_tot + n_experts * (bm - 1)) // bm)
    row_tok = jnp.zeros((nblk * bm,), I32).at[dest].set(jnp.arange(a_tot, dtype=I32) // TOP_K)
    n_used = (pends[-1] // bm).astype(I32)
    blk = jnp.arange(nblk, dtype=I32)
    blk_e = jnp.searchsorted(pends, jnp.minimum(blk, n_used - 1) * bm, side="right").astype(I32)
    blk_e = jnp.minimum(blk_e, n_experts - 1)
    dest2 = dest.reshape(r, TOP_K)
    return row_tok.reshape(nblk, 1, bm), blk_e, n_used.reshape(1), dest2[:, 0], dest2[:, 1]


def kernel(x, meta, w_in, rw_mu, rw_w2, rw_w0, rw_a2, rw_a0, rw_g2, rw_kk, rw_ka, rw_rk, rw_gn_g, rw_gn_b,
           at_qnorm_g, at_w_uq, at_w_iq, at_kidx_g, at_kidx_b, pool_w, pool_scale, w_out, ln1_g, ln1_b,
           router_g_w, router_g_b, router_e_w, router_e_b, exp_w1, exp_w3, exp_w2, ln2_g, ln2_b):
    bsz, seq_len, d = x.shape
    depth = w_in.shape[0]
    n_meta = meta.shape[0]
    t_len = seq_len + n_meta
    p_len = _round_up(t_len, ROW_TILE)
    rows = bsz * p_len
    n_sel = min(TOPK_MAX, seq_len // 4)
    alpha = float((2 * depth) ** 0.25)

    rw = rw_w2.shape[2]
    aw = at_w_uq.shape[2]
    idx_head = at_kidx_g.shape[1]
    idx_heads = at_w_iq.shape[2] // idx_head
    assert idx_head == LANE and aw % AT_HEAD == 0 and rw % LANE == 0
    n_groups = router_g_w.shape[2]
    n_experts = router_e_w.shape[2]
    per_group = n_experts // n_groups
    assert n_groups + n_experts <= LANE
    pool_width = pool_w.shape[1] * pool_w.shape[2]

    src = {"r": rw, "k": rw, "v": rw, "dw": rw_w2.shape[1], "da": rw_a2.shape[1], "dg": rw_g2.shape[1],
           "cq": at_w_uq.shape[1], "ka": aw, "va": aw, "kidx": idx_head, "widx": idx_heads, "pin": pool_width}
    off, pw, n_proj = _layout(src)
    n_proj = _round_up(n_proj, 512)
    rw_names = ("r", "k", "v", "dw", "da", "dg")
    rw_src = {n: src[n] for n in rw_names}
    mu_starts = np.cumsum([0] + [rw_src[n] for n in rw_names])

    pos = jnp.arange(p_len, dtype=F32)
    inv = ROPE_THETA ** (-jnp.arange(0, AT_HEAD, 2, dtype=F32) / AT_HEAD)
    ang = pos[:, None] * inv[None, :]
    cosf = jnp.concatenate([jnp.cos(ang), jnp.cos(ang)], axis=1)
    sinf = jnp.concatenate([-jnp.sin(ang), jnp.sin(ang)], axis=1)

    h = jnp.concatenate([jnp.broadcast_to(meta.astype(x.dtype)[None], (bsz, n_meta, d)), x,
                         jnp.zeros((bsz, p_len - t_len, d), x.dtype)], axis=1).reshape(rows, d)
    hb = h.astype(BF16)
    row1 = lambda a: a.reshape(1, -1)
    pad_rows = lambda a, n: jnp.concatenate([a, jnp.zeros((n - a.shape[0],) + a.shape[1:], a.dtype)], axis=0)
    tm_c = ROW_TILE

    for l in range(depth):
        w_in_p = _scatter_cols(w_in[l], src, off, n_proj).astype(BF16)
        proj = _matmul(hb, w_in_p, F32)

        mu = {}
        for i, n in enumerate(rw_names):
            seg = rw_mu[l, int(mu_starts[i]):int(mu_starts[i + 1])]
            mu[n] = row1(jnp.concatenate([seg, jnp.zeros((pw[n] - src[n],), F32)]))
        g2 = pad_rows(rw_g2[l], pw["dg"]).astype(BF16)
        seqs = _rwkv_prep(proj, bsz, p_len, off, pw, mu, rw_w2[l].astype(BF16), row1(rw_w0[l]),
                          rw_a2[l].astype(BF16), row1(rw_a0[l]), g2, row1(rw_kk[l]), row1(rw_ka[l]))
        y_rw = _rwkv_scan(seqs, row1(rw_rk[l]), row1(rw_gn_g[l]), row1(rw_gn_b[l]), bsz, p_len)

        q, qi, k_at, ki, wq = _dsa_prep(proj, bsz, p_len, off, pw, cosf, sinf, row1(at_qnorm_g[l]),
                                        at_w_uq[l].astype(BF16), at_w_iq[l].astype(BF16),
                                        row1(at_kidx_g[l]), row1(at_kidx_b[l]), idx_heads)
        v_at = lax.slice_in_dim(proj, off["va"], off["va"] + aw, axis=1).astype(BF16)
        y_at = _dsa_attn(q, qi, wq, k_at, v_at, ki, bsz, p_len, n_sel, idx_heads)

        y_pl = _pool_mix(proj, bsz, p_len, off, pw, pool_w[l].astype(BF16), row1(pool_scale[l]))

        mix = jnp.concatenate([y_rw, y_at, y_pl], axis=1)
        h, hb = _outproj_ln(mix, w_out[l].astype(BF16), h, row1(ln1_g[l]), row1(ln1_b[l]), alpha)

        n_r = n_groups + n_experts
        w_r = jnp.concatenate([router_g_w[l], router_e_w[l], jnp.zeros((d, LANE - n_r), F32)], axis=1).astype(BF16)
        b_r = row1(jnp.concatenate([router_g_b[l], router_e_b[l], jnp.zeros((LANE - n_r,), F32)]))
        eid, gates = _router(hb, w_r, b_r, n_groups, per_group)
        row_tok, blk_e, n_used, dest0, dest1 = _dispatch_plan(eid, n_experts, MOE_BM)
        yb = _moe_ffn(h, exp_w1[l].astype(BF16), exp_w3[l].astype(BF16), exp_w2[l].astype(BF16),
                      blk_e, n_used, row_tok)
        h, hb = _combine_ln(yb, h, gates, dest0.reshape(rows // tm_c, 1, tm_c), dest1.reshape(rows // tm_c, 1, tm_c),
                            row1(ln2_g[l]), row1(ln2_b[l]), alpha)

    return h.reshape(bsz, p_len, d)[:, n_meta:t_len]
```

```python
import functools
import math

import jax
import jax.numpy as jnp
import numpy as np
from jax import lax
from jax.experimental import pallas as pl
from jax.experimental.pallas import tpu as pltpu

F32 = jnp.float32
BF16 = jnp.bfloat16
I32 = jnp.int32

LANE = 128
ROW_TILE = 128
RW_HEAD = 64
AT_HEAD = 128
TOPK_MAX = 256
ROPE_THETA = 10000.0
POOL_WINDOWS = (2, 4, 8, 16)
POOL_HALO = 16
GN_EPS = 64e-5
LN_EPS = 1e-5
RW_CHUNK = 64
MOE_BM = 128
DSA_GROUPS = 4
TOP_K = 2
INT_MIN = -(2 ** 31)
VMEM_LIMIT = 56 * 1024 * 1024

HIGHEST = lax.Precision.HIGHEST
NN = (((1,), (0,)), ((), ()))
NT = (((1,), (1,)), ((), ()))
TN = (((0,), (0,)), ((), ()))


def _dot(a, b, dims=NN, precision=None):
    return lax.dot_general(a, b, dims, precision=precision, preferred_element_type=F32)


def _sigmoid(x):
    return 1.0 / (1.0 + jnp.exp(-x))


def _cparams(sem):
    return pltpu.CompilerParams(dimension_semantics=sem, vmem_limit_bytes=VMEM_LIMIT)


def _round_up(x, m):
    return -(-x // m) * m


def _pick_tile(n, candidates):
    for c in candidates:
        if n % c == 0:
            return c
    return n


def _layout(sizes):
    pw = {k: _round_up(v, LANE) for k, v in sizes.items()}
    order = sorted(sizes, key=lambda k: -pw[k])
    off, gaps, cur = {}, [], 0
    for k in order:
        w = pw[k]
        placed = False
        for gi, (g0, g1) in enumerate(gaps):
            s = _round_up(g0, w)
            if s + w <= g1:
                off[k] = s
                new = [(g0, s), (s + w, g1)]
                gaps[gi:gi + 1] = [g for g in new if g[1] > g[0]]
                placed = True
                break
        if not placed:
            s = _round_up(cur, w)
            if s > cur:
                gaps.append((cur, s))
            off[k] = s
            cur = s + w
    return off, pw, cur


def _scatter_cols(w, src_sizes, off, total):
    names = list(src_sizes)
    starts = np.cumsum([0] + [src_sizes[n] for n in names])
    pieces, cur = [], 0
    for n in sorted(names, key=lambda n: off[n]):
        i = names.index(n)
        if off[n] > cur:
            pieces.append(jnp.zeros(w.shape[:-1] + (off[n] - cur,), w.dtype))
        pieces.append(w[..., int(starts[i]):int(starts[i + 1])])
        cur = off[n] + src_sizes[n]
    if total > cur:
        pieces.append(jnp.zeros(w.shape[:-1] + (total - cur,), w.dtype))
    return jnp.concatenate(pieces, axis=-1)


def _mm_kernel(x_ref, w_ref, o_ref):
    o_ref[...] = _dot(x_ref[...], w_ref[...]).astype(o_ref.dtype)


def _matmul(x, w, out_dtype):
    r, k = x.shape
    n = w.shape[1]
    tm = _pick_tile(r, (768, 512, 384, 256, 128))
    tn = _pick_tile(n, (1024, 512, 256, 128))
    return pl.pallas_call(
        _mm_kernel,
        grid=(n // tn, r // tm),
        in_specs=[pl.BlockSpec((tm, k), lambda j, i: (i, 0)),
                  pl.BlockSpec((k, tn), lambda j, i: (0, j))],
        out_specs=pl.BlockSpec((tm, tn), lambda j, i: (i, j)),
        out_shape=jax.ShapeDtypeStruct((r, n), out_dtype),
        compiler_params=_cparams(("arbitrary", "arbitrary")),
    )(x, w)


def _head_sums(x, hsum):
    tiles = [_dot(x[:, c * LANE:(c + 1) * LANE], hsum, precision=HIGHEST) for c in range(x.shape[1] // LANE)]
    return jnp.concatenate(tiles, axis=1)


def _split3(x):
    hi = x.astype(BF16)
    r1 = x - hi.astype(F32)
    mid = r1.astype(BF16)
    lo = (r1 - mid.astype(F32)).astype(BF16)
    return hi, mid, lo


def _rwkv_prep_kernel(r_ref, k_ref, v_ref, dw_ref, da_ref, dg_ref,
                      rh_ref, kh_ref, vh_ref, dwh_ref, dah_ref, dgh_ref,
                      mur_ref, muk_ref, muv_ref, mudw_ref, muda_ref, mudg_ref,
                      w2_ref, w0_ref, a2_ref, a0_ref, g2_ref, kkp_ref, kap_ref, rk_ref, hsum_ref,
                      tri_ref, blk_ref,
                      at_o, bt_o, kt_o, rt_o, bc_o, kc_o, v_o, epc_o, bv_o, g_o):
    first = pl.program_id(1) == 0

    def shift(x_ref, h_ref, mu_ref):
        x = x_ref[...]
        prev_row = jnp.where(first, 0.0, h_ref[7:8, :])
        row = lax.broadcasted_iota(I32, x.shape, 0)
        prev = jnp.where(row == 0, prev_row, pltpu.roll(x, 1, axis=0))
        return x + (prev - x) * mu_ref[...]

    r = shift(r_ref, rh_ref, mur_ref)
    k = shift(k_ref, kh_ref, muk_ref)
    v = shift(v_ref, vh_ref, muv_ref)
    dw = shift(dw_ref, dwh_ref, mudw_ref)
    da = shift(da_ref, dah_ref, muda_ref)
    dg = shift(dg_ref, dgh_ref, mudg_ref)

    wl = w0_ref[...] + _dot(jnp.tanh(dw).astype(BF16), w2_ref[...])
    neg = -wl
    softplus = jnp.maximum(neg, 0.0) + jnp.log(1.0 + jnp.exp(-jnp.abs(neg)))
    lw = -jnp.exp(-softplus - 0.5)
    a = _sigmoid(a0_ref[...] + _dot(da.astype(BF16), a2_ref[...]))
    g_o[...] = _dot(_sigmoid(dg).astype(BF16), g2_ref[...])
    hsum = hsum_ref[...]
    kk = k * kkp_ref[...]
    kk = kk / jnp.maximum(jnp.sqrt(_head_sums(kk * kk, hsum)), 1e-12)
    k = k * (1.0 + (a - 1.0) * kap_ref[...])
    bv_o[...] = _head_sums(r * k * rk_ref[...], hsum) * v

    parts = _split3(lw)
    tri, blk = tri_ref[...], blk_ref[...]
    cl = _dot(tri, parts[0]) + _dot(tri, parts[1]) + _dot(tri, parts[2])
    clc = _dot(blk, parts[0]) + _dot(blk, parts[1]) + _dot(blk, parts[2])
    e_n = jnp.exp(-cl)
    e_nc = jnp.exp(clc - cl)
    kka = kk * a
    at_o[...] = (-kk * jnp.exp(cl - lw)).astype(at_o.dtype)
    bt_o[...] = (kka * e_n).astype(bt_o.dtype)
    kt_o[...] = (k * e_n).astype(kt_o.dtype)
    rt_o[...] = (r * jnp.exp(cl)).astype(rt_o.dtype)
    bc_o[...] = (kka * e_nc).astype(bc_o.dtype)
    kc_o[...] = (k * e_nc).astype(kc_o.dtype)
    v_o[...] = v.astype(v_o.dtype)
    epc_o[...] = jnp.exp(clc)


def _rwkv_prep(proj, bsz, p_len, off, pw, mu, w2, w0, a2, a0, g2, kkp, kap, rk):
    rw = w2.shape[1]
    tm = ROW_TILE
    nb = p_len // tm
    names = ("r", "k", "v", "dw", "da", "dg")

    def cur_spec(n):
        w, c = pw[n], off[n] // pw[n]
        return pl.BlockSpec((tm, w), lambda b, i, c=c: (b * nb + i, c))

    def halo_spec(n):
        w, c = pw[n], off[n] // pw[n]
        return pl.BlockSpec((8, w), lambda b, i, c=c: (jnp.maximum((b * nb + i) * (tm // 8) - 1, 0), c))

    def full(a):
        return pl.BlockSpec(a.shape, lambda b, i: (0,) * a.ndim)

    lane = np.arange(LANE)
    hsum = jnp.asarray((lane[:, None] // RW_HEAD == lane[None, :] // RW_HEAD).astype(np.float32))
    t = np.arange(tm)
    same = t[:, None] // RW_CHUNK == t[None, :] // RW_CHUNK
    tri = jnp.asarray((same & (t[None, :] <= t[:, None])).astype(np.float32)).astype(BF16)
    blk = jnp.asarray(same.astype(np.float32)).astype(BF16)
    params = [mu[n] for n in names] + [w2, w0, a2, a0, g2, kkp, kap, rk, hsum, tri, blk]
    out_spec = pl.BlockSpec((tm, rw), lambda b, i: (b * nb + i, 0))
    sd = lambda dt: jax.ShapeDtypeStruct((bsz * p_len, rw), dt)
    return pl.pallas_call(
        _rwkv_prep_kernel,
        grid=(bsz, nb),
        in_specs=[cur_spec(n) for n in names] + [halo_spec(n) for n in names] + [full(a) for a in params],
        out_specs=[out_spec] * 10,
        out_shape=[sd(BF16)] * 7 + [sd(F32)] * 3,
        compiler_params=_cparams(("arbitrary", "arbitrary")),
    )(*([proj] * 12), *params)


def _rwkv_heads(at, bt, kt, rt, bc, kc, v, p_row, zt):
    n = len(at)
    hs = range(n)
    c = at[0].shape[0]
    b16 = lambda x: x.astype(BF16)
    row = lax.broadcasted_iota(I32, (2 * c, c), 0)
    col = lax.broadcasted_iota(I32, (2 * c, c), 1)
    keep = col < (row & (c - 1)) + jnp.where(row < c, 0, 1)
    lhs2 = [jnp.concatenate([at[h], rt[h]], axis=0) for h in hs]
    x_b = [jnp.where(keep, _dot(lhs2[h], bt[h], NT), 0.0) for h in hs]
    x_k = [jnp.where(keep, _dot(lhs2[h], kt[h], NT), 0.0) for h in hs]
    lkmv = [_dot(b16(x_k[h]), v[h]) for h in hs]
    l_ba = [x[:c] for x in x_b]
    m_b = [b16(x[c:]) for x in x_b]

    ti = lax.broadcasted_iota(I32, (c, c), 0)
    si = lax.broadcasted_iota(I32, (c, c), 1)
    zero = jnp.zeros((c, c), F32)
    eye = (ti == si).astype(F32)
    same_lo = (ti >> 4) == (si >> 4)
    lp = [jnp.where(same_lo, l, zero) for l in l_ba]
    t = [eye + l for l in lp]
    for _ in range(3):
        lp = [_dot(b16(l), b16(l)) for l in lp]
        t = [t[h] + _dot(b16(lp[h]), b16(t[h])) for h in hs]
    size = 32
    while size <= c:
        shift = size.bit_length() - 1
        same_hi = (ti >> shift) == (si >> shift)
        l_off = [b16(jnp.where(same_hi, jnp.where(same_lo, zero, l), zero)) for l in l_ba]
        tl = [_dot(b16(t[h]), l_off[h]) for h in hs]
        t = [t[h] + _dot(b16(tl[h]), b16(t[h])) for h in hs]
        same_lo = same_hi
        size *= 2

    tb = [b16(x) for x in t]
    a_z = [b16(_dot(tb[h], at[h])) for h in hs]
    w_u = [b16(_dot(tb[h], b16(lkmv[h][:c]))) for h in hs]
    ztb = [b16(z) for z in zt]
    q_z = [b16(rt[h].astype(F32) + _dot(m_b[h], a_z[h])) for h in hs]
    y = [_dot(q_z[h], ztb[h], NT) + _dot(m_b[h], w_u[h]) + lkmv[h][c:] for h in hs]
    g_t = [b16(_dot(a_z[h], bc[h], TN)) for h in hs]
    h_t = [_dot(w_u[h], bc[h], TN) + _dot(v[h], kc[h], TN) for h in hs]
    zt_new = [p_row[h] * zt[h] + _dot(ztb[h], g_t[h]) + h_t[h] for h in hs]
    return y, zt_new


def _rwkv_scan_kernel(at_ref, bt_ref, kt_ref, rt_ref, bc_ref, kc_ref, v_ref, epc_ref, bv_ref, g_ref,
                      gg_ref, gb_ref, o_ref, z_ref):
    @pl.when(pl.program_id(2) == 0)
    def _():
        z_ref[...] = jnp.zeros_like(z_ref)

    n_heads = o_ref.shape[1] // RW_HEAD
    sls = [slice(h * RW_HEAD, (h + 1) * RW_HEAD) for h in range(n_heads)]
    cut = lambda ref: [ref[:, sl] for sl in sls]
    y, z_new = _rwkv_heads(cut(at_ref), cut(bt_ref), cut(kt_ref), cut(rt_ref), cut(bc_ref), cut(kc_ref),
                           cut(v_ref), [epc_ref[0:1, sl] for sl in sls], [z_ref[h] for h in range(n_heads)])
    for h in range(n_heads):
        z_ref[h] = z_new[h]
    outs = []
    for h, sl in enumerate(sls):
        mu = jnp.mean(y[h], axis=-1, keepdims=True)
        var = jnp.mean(jnp.square(y[h] - mu), axis=-1, keepdims=True)
        yn = (y[h] - mu) * lax.rsqrt(var + GN_EPS) * gg_ref[:, sl] + gb_ref[:, sl]
        outs.append((yn + bv_ref[:, sl]) * g_ref[:, sl])
    per = LANE // RW_HEAD
    for p in range(n_heads // per):
        o_ref[:, p * LANE:(p + 1) * LANE] = jnp.concatenate(outs[p * per:(p + 1) * per], axis=1).astype(o_ref.dtype)


def _rwkv_scan(seqs, gn_g, gn_b, bsz, p_len):
    rw = gn_g.shape[1]
    width = _pick_tile(rw, (12 * LANE, 4 * LANE, 3 * LANE, 2 * LANE, LANE))
    nch = p_len // RW_CHUNK
    seqs = [s.reshape(bsz, p_len, rw) for s in seqs]
    seq_spec = pl.BlockSpec((None, RW_CHUNK, width), lambda b, p, c: (b, c, p))
    par_spec = pl.BlockSpec((1, width), lambda b, p, c: (0, p))
    out = pl.pallas_call(
        _rwkv_scan_kernel,
        grid=(bsz, rw // width, nch),
        in_specs=[seq_spec] * len(seqs) + [par_spec] * 2,
        out_specs=seq_spec,
        out_shape=jax.ShapeDtypeStruct((bsz, p_len, rw), BF16),
        scratch_shapes=[pltpu.VMEM((width // RW_HEAD, RW_HEAD, RW_HEAD), F32)],
        compiler_params=_cparams(("arbitrary", "arbitrary", "arbitrary")),
    )(*seqs, gn_g, gn_b)
    return out.reshape(bsz * p_len, rw)


def _rope_tiles(x, cosf, sinf):
    outs = []
    for c in range(x.shape[1] // LANE):
        xt = x[:, c * LANE:(c + 1) * LANE]
        outs.append(xt * cosf + pltpu.roll(xt, LANE // 2, axis=1) * sinf)
    return outs


def _dsa_prep_kernel(cq_ref, ka_ref, kidx_ref, widx_ref, cos_ref, sin_ref, qg_ref, wuq_ref, wiq_ref,
                     kg_ref, kb_ref, q_o, qi_o, k_o, ki_o, wq_o, *, wq_scale):
    cosf, sinf = cos_ref[...], sin_ref[...]
    cq = cq_ref[...]
    cqn = cq * lax.rsqrt(jnp.mean(jnp.square(cq), axis=-1, keepdims=True) + 1e-6) * qg_ref[...]
    cqb = cqn.astype(BF16)
    for c, t in enumerate(_rope_tiles(_dot(cqb, wuq_ref[...]), cosf, sinf)):
        q_o[:, c * LANE:(c + 1) * LANE] = t.astype(q_o.dtype)
    for c, t in enumerate(_rope_tiles(_dot(cqb, wiq_ref[...]), cosf, sinf)):
        qi_o[:, c * LANE:(c + 1) * LANE] = t.astype(qi_o.dtype)
    for c, t in enumerate(_rope_tiles(ka_ref[...], cosf, sinf)):
        k_o[:, c * LANE:(c + 1) * LANE] = t.astype(k_o.dtype)
    kx = kidx_ref[...]
    mu = jnp.mean(kx, axis=-1, keepdims=True)
    var = jnp.mean(jnp.square(kx - mu), axis=-1, keepdims=True)
    kn = (kx - mu) * lax.rsqrt(var + LN_EPS) * kg_ref[...] + kb_ref[...]
    ki_o[...] = _rope_tiles(kn, cosf, sinf)[0].astype(ki_o.dtype)
    wq_o[...] = widx_ref[...] * wq_scale


def _dsa_prep(proj, bsz, p_len, off, pw, cosf, sinf, qg, wuq, wiq, kg, kb, idx_heads):
    tm = _pick_tile(p_len, (256, 128))
    nb = p_len // tm
    rows = bsz * p_len
    aw, iw = wuq.shape[1], wiq.shape[1]

    def seg(n):
        w, c = pw[n], off[n] // pw[n]
        return pl.BlockSpec((tm, w), lambda i, c=c: (i, c))

    def full(a):
        return pl.BlockSpec(a.shape, lambda i: (0,) * a.ndim)

    pos_spec = pl.BlockSpec((tm, LANE), lambda i: (i % nb, 0))
    row_spec = lambda w: pl.BlockSpec((tm, w), lambda i: (i, 0))
    kern = functools.partial(_dsa_prep_kernel, wq_scale=float(idx_heads ** -0.5 * AT_HEAD ** -0.5))
    return pl.pallas_call(
        kern,
        grid=(rows // tm,),
        in_specs=[seg("cq"), seg("ka"), seg("kidx"), seg("widx"), pos_spec, pos_spec,
                  full(qg), full(wuq), full(wiq), full(kg), full(kb)],
        out_specs=[row_spec(aw), row_spec(iw), row_spec(aw), row_spec(LANE), row_spec(LANE)],
        out_shape=[jax.ShapeDtypeStruct((rows, aw), BF16), jax.ShapeDtypeStruct((rows, iw), BF16),
                   jax.ShapeDtypeStruct((rows, aw), BF16), jax.ShapeDtypeStruct((rows, LANE), BF16),
                   jax.ShapeDtypeStruct((rows, LANE), F32)],
        compiler_params=_cparams(("arbitrary",)),
    )(proj, proj, proj, proj, cosf, sinf, qg, wuq, wiq, kg, kb)


def _topk_mask(sc, causal, n_sel):
    tq, s_len = sc.shape
    bits = lax.bitcast_convert_type(jnp.where(sc == 0.0, 0.0, sc), I32)
    key = jnp.where(bits < 0, bits ^ jnp.int32(0x7FFFFFFF), bits)
    key = jnp.where(causal, key, jnp.int32(INT_MIN))
    one = jnp.ones((tq, s_len), I32)
    zero = jnp.zeros((tq, s_len), I32)

    def count_ge(th):
        return jnp.sum(jnp.where(key >= th, one, zero), axis=1, keepdims=True)

    tau = jnp.where(count_ge(jnp.zeros((tq, 1), I32)) >= n_sel, jnp.int32(0), jnp.int32(INT_MIN))

    def tau_step(i, tau):
        cand = tau | jnp.left_shift(jnp.int32(1), 30 - i)
        return jnp.where(count_ge(cand) >= n_sel, cand, tau)

    tau = lax.fori_loop(0, 31, tau_step, tau)
    gt = key > tau
    eq = key == tau
    n_gt = jnp.sum(jnp.where(gt, one, zero), axis=1, keepdims=True)
    n_eq = jnp.sum(jnp.where(eq, one, zero), axis=1, keepdims=True)
    need = n_sel - n_gt
    idx = lax.broadcasted_iota(I32, (tq, s_len), 1)
    nbits = max(1, (s_len - 1).bit_length())

    def cut_search():
        def cut_step(i, lo):
            cand = lo | jnp.left_shift(jnp.int32(1), nbits - 1 - i)
            cnt = jnp.sum(jnp.where(eq, jnp.where(idx < cand, one, zero), zero), axis=1, keepdims=True)
            return jnp.where(cnt < need, cand, lo)
        return lax.fori_loop(0, nbits, cut_step, jnp.zeros((tq, 1), I32))

    surplus = jnp.where(tau > jnp.int32(INT_MIN), jnp.where(n_eq > need, 1, 0), 0)
    cut = lax.cond(jnp.max(surplus) > 0, cut_search, lambda: jnp.full((tq, 1), s_len, I32))
    sel = jnp.where(gt, one, jnp.where(eq, jnp.where(idx <= cut, one, zero), zero))
    return jnp.where(causal, sel, zero) > 0


def _dsa_attn_kernel(q_ref, qi_ref, wq_ref, k_ref, v_ref, ki_ref, o_ref, *, n_sel, idx_heads, scale, q_lo):
    tq = q_ref.shape[0]
    s_len = k_ref.shape[0]
    t0 = (pl.program_id(1) + q_lo) * tq
    ki = ki_ref[...]
    wq = wq_ref[...]
    sc = jnp.zeros((tq, s_len), F32)
    for h in range(idx_heads):
        s_h = _dot(qi_ref[:, h * LANE:(h + 1) * LANE], ki, NT)
        sc = sc + jnp.maximum(s_h, 0.0) * wq[:, h:h + 1]
    qpos = t0 + lax.broadcasted_iota(I32, (tq, s_len), 0)
    kpos = lax.broadcasted_iota(I32, (tq, s_len), 1)
    mask = _topk_mask(sc, kpos <= qpos, n_sel)
    bias = jnp.where(mask, 0.0, -jnp.inf)
    log2e_scale = scale * math.log2(math.e)
    for h in range(q_ref.shape[1] // AT_HEAD):
        sl = slice(h * AT_HEAD, (h + 1) * AT_HEAD)
        lg = _dot(q_ref[:, sl], k_ref[:, sl], NT) * log2e_scale + bias
        p = jnp.exp2(lg - jnp.max(lg, axis=1, keepdims=True))
        den = jnp.sum(p, axis=1, keepdims=True)
        o_ref[:, sl] = (_dot(p.astype(BF16), v_ref[:, sl]) / den).astype(o_ref.dtype)


def _dsa_attn(q, qi, wq, k, v, ki, bsz, p_len, n_sel, idx_heads):
    aw, iw = q.shape[1], qi.shape[1]
    tq = ROW_TILE
    nq = p_len // tq
    n_groups = min(DSA_GROUPS, nq)
    bounds = [round(nq * g / n_groups) for g in range(n_groups + 1)]
    r3 = lambda a: a.reshape(bsz, p_len, a.shape[1])
    args = (r3(q), r3(qi), r3(wq), r3(k), r3(v), r3(ki))
    outs = []
    for lo, hi in zip(bounds[:-1], bounds[1:]):
        s_len = hi * tq
        assert s_len >= n_sel
        qspec = lambda w, lo=lo: pl.BlockSpec((None, tq, w), lambda b, i: (b, i + lo, 0))
        kspec = lambda w, s_len=s_len: pl.BlockSpec((None, s_len, w), lambda b, i: (b, 0, 0),
                                                    pipeline_mode=pl.Buffered(1))
        kern = functools.partial(_dsa_attn_kernel, n_sel=n_sel, idx_heads=idx_heads,
                                 scale=float(AT_HEAD ** -0.5), q_lo=lo)
        outs.append(pl.pallas_call(
            kern,
            grid=(bsz, hi - lo),
            in_specs=[qspec(aw), qspec(iw), qspec(LANE), kspec(aw), kspec(aw), kspec(LANE)],
            out_specs=pl.BlockSpec((None, tq, aw), lambda b, i: (b, i, 0)),
            out_shape=jax.ShapeDtypeStruct((bsz, (hi - lo) * tq, aw), BF16),
            compiler_params=_cparams(("arbitrary", "arbitrary")),
        )(*args))
    return jnp.concatenate(outs, axis=1).reshape(bsz * p_len, aw)


def _pool_kernel(p_ref, h_ref, w_ref, s_ref, o_ref):
    tm, width = p_ref.shape
    grp = width // len(POOL_WINDOWS)
    t0 = pl.program_id(1) * tm
    x = p_ref[...]
    halo = jnp.where(pl.program_id(1) == 0, 0.0, h_ref[...])
    ext = jnp.concatenate([halo, x], axis=0)
    tpos = t0 + lax.broadcasted_iota(I32, (tm, 1), 0)
    acc = ext
    have = 1
    for gi, win in enumerate(POOL_WINDOWS):
        while have < win:
            shifted = jnp.concatenate([jnp.zeros((have, width), F32), acc[:-have]], axis=0)
            acc = acc + shifted
            have *= 2
        sl = slice(gi * grp, (gi + 1) * grp)
        cnt = jnp.minimum(tpos + 1, win).astype(F32)
        pooled = acc[POOL_HALO:, sl] / cnt - x[:, sl]
        y = _dot(pooled.astype(BF16), w_ref[gi])
        o_ref[:, sl] = (y * s_ref[:, sl]).astype(o_ref.dtype)


def _pool_mix(proj, bsz, p_len, off, pw, w_pool, scale):
    width = pw["pin"]
    tm = _pick_tile(p_len, (256, 128))
    nb = p_len // tm
    c = off["pin"] // width
    hb = tm // POOL_HALO
    return pl.pallas_call(
        _pool_kernel,
        grid=(bsz, nb),
        in_specs=[pl.BlockSpec((tm, width), lambda b, i: (b * nb + i, c)),
                  pl.BlockSpec((POOL_HALO, width), lambda b, i: (jnp.maximum((b * nb + i) * hb - 1, 0), c)),
                  pl.BlockSpec(w_pool.shape, lambda b, i: (0, 0, 0)),
                  pl.BlockSpec(scale.shape, lambda b, i: (0, 0))],
        out_specs=pl.BlockSpec((tm, width), lambda b, i: (b * nb + i, 0)),
        out_shape=jax.ShapeDtypeStruct((bsz * p_len, width), BF16),
        compiler_params=_cparams(("arbitrary", "arbitrary")),
    )(proj, proj, w_pool, scale)


def _layernorm_rows(x, g, b):
    mu = jnp.mean(x, axis=-1, keepdims=True)
    var = jnp.mean(jnp.square(x - mu), axis=-1, keepdims=True)
    return (x - mu) * lax.rsqrt(var + LN_EPS) * g + b


def _outproj_ln_kernel(x_ref, w_ref, h_ref, g_ref, b_ref, o_ref, ob_ref, acc_ref, *, alpha):
    kk = pl.program_id(1)

    @pl.when(kk == 0)
    def _():
        acc_ref[...] = jnp.zeros_like(acc_ref)

    acc_ref[...] += _dot(x_ref[...], w_ref[...])

    @pl.when(kk == pl.num_programs(1) - 1)
    def _():
        y = _layernorm_rows(alpha * h_ref[...] + acc_ref[...], g_ref[...], b_ref[...])
        o_ref[...] = y
        ob_ref[...] = y.astype(ob_ref.dtype)


def _outproj_ln(mix, w_out, h, g, b, alpha):
    r, k = mix.shape
    d = w_out.shape[1]
    tm = _pick_tile(r, (384, 256, 128))
    tk = _pick_tile(k, (512, 256, 128))
    kern = functools.partial(_outproj_ln_kernel, alpha=alpha)
    return pl.pallas_call(
        kern,
        grid=(r // tm, k // tk),
        in_specs=[pl.BlockSpec((tm, tk), lambda i, j: (i, j)),
                  pl.BlockSpec((tk, d), lambda i, j: (j, 0)),
                  pl.BlockSpec((tm, d), lambda i, j: (i, 0)),
                  pl.BlockSpec((1, d), lambda i, j: (0, 0)),
                  pl.BlockSpec((1, d), lambda i, j: (0, 0))],
        out_specs=[pl.BlockSpec((tm, d), lambda i, j: (i, 0))] * 2,
        out_shape=[jax.ShapeDtypeStruct((r, d), F32), jax.ShapeDtypeStruct((r, d), BF16)],
        scratch_shapes=[pltpu.VMEM((tm, d), F32)],
        compiler_params=_cparams(("arbitrary", "arbitrary")),
    )(mix, w_out, h, g, b)


def _router_kernel(x_ref, w_ref, b_ref, e_o, g_o, *, n_groups, per_group):
    logits = _dot(x_ref[...], w_ref[...]) + b_ref[...]
    lane = lax.broadcasted_iota(I32, logits.shape, 1)
    big = jnp.int32(LANE)
    ninf = -jnp.inf
    gl = jnp.where(lane < n_groups, logits, ninf)
    ge = jnp.exp(gl - jnp.max(gl, axis=1, keepdims=True))
    gp = ge / jnp.sum(ge, axis=1, keepdims=True)
    g_val = jnp.max(gp, axis=1, keepdims=True)
    g_idx = jnp.min(jnp.where(gp == g_val, lane, big), axis=1, keepdims=True)
    lane_grp = jnp.where(lane >= n_groups, (lane - n_groups) // per_group, -1)
    in_grp = lane_grp == g_idx
    el = jnp.where(in_grp, logits, ninf)
    ee = jnp.exp(el - jnp.max(el, axis=1, keepdims=True))
    ep = jnp.where(in_grp, ee / jnp.sum(ee, axis=1, keepdims=True), -1.0)
    v1 = jnp.max(ep, axis=1, keepdims=True)
    i1 = jnp.min(jnp.where(ep == v1, lane, big), axis=1, keepdims=True)
    ep2 = jnp.where(lane == i1, -1.0, ep)
    v2 = jnp.max(ep2, axis=1, keepdims=True)
    i2 = jnp.min(jnp.where(ep2 == v2, lane, big), axis=1, keepdims=True)
    tot = v1 + v2
    e_o[...] = jnp.where(lane == 0, i1 - n_groups, jnp.where(lane == 1, i2 - n_groups, 0))
    g_o[...] = jnp.where(lane == 0, g_val * v1 / tot, jnp.where(lane == 1, g_val * v2 / tot, 0.0))


def _router(hb, w, b, n_groups, per_group):
    r, d = hb.shape
    tm = _pick_tile(r, (256, 128))
    kern = functools.partial(_router_kernel, n_groups=n_groups, per_group=per_group)
    return pl.pallas_call(
        kern,
        grid=(r // tm,),
        in_specs=[pl.BlockSpec((tm, d), lambda i: (i, 0)),
                  pl.BlockSpec((d, LANE), lambda i: (0, 0)),
                  pl.BlockSpec((1, LANE), lambda i: (0, 0))],
        out_specs=[pl.BlockSpec((tm, LANE), lambda i: (i, 0))] * 2,
        out_shape=[jax.ShapeDtypeStruct((r, LANE), I32), jax.ShapeDtypeStruct((r, LANE), F32)],
        compiler_params=_cparams(("arbitrary",)),
    )(hb, w, b)


def _gather_start(src_hbm, idx_ref, dst, sem):
    def issue(j, carry):
        pltpu.make_async_copy(src_hbm.at[pl.ds(idx_ref[0, 0, j], 1), :], dst.at[pl.ds(j, 1), :], sem).start()
        return carry
    lax.fori_loop(0, dst.shape[0], issue, 0)


def _gather_wait(src_hbm, dst, sem):
    pltpu.make_async_copy(src_hbm.at[pl.ds(0, dst.shape[0]), :], dst, sem).wait()


def _moe_ffn_kernel(be_ref, nb_ref, tok_ref, tok_next_ref, h_hbm, w1_ref, w3_ref, w2_ref, y_ref, xbuf, sem):
    del be_ref
    i = pl.program_id(0)
    n_used = nb_ref[0]
    slot = i % 2

    @pl.when(i == 0)
    def _():
        _gather_start(h_hbm, tok_ref, xbuf.at[0], sem.at[0])

    @pl.when(i + 1 < n_used)
    def _():
        _gather_start(h_hbm, tok_next_ref, xbuf.at[1 - slot], sem.at[1 - slot])

    @pl.when(i < n_used)
    def _():
        _gather_wait(h_hbm, xbuf.at[slot], sem.at[slot])
        x = xbuf[slot].astype(BF16)
        h1 = _dot(x, w1_ref[...])
        h3 = _dot(x, w3_ref[...])
        act = h1 * _sigmoid(h1) * h3
        y_ref[...] = _dot(act.astype(BF16), w2_ref[...])

    @pl.when(i >= n_used)
    def _():
        y_ref[...] = jnp.zeros_like(y_ref)


def _moe_ffn(h, w1, w3, w2, blk_e, n_used, row_tok):
    r, d = h.shape
    nblk = blk_e.shape[0]
    de = w1.shape[2]
    bm = row_tok.shape[2]
    grid_spec = pltpu.PrefetchScalarGridSpec(
        num_scalar_prefetch=2,
        grid=(nblk,),
        in_specs=[pl.BlockSpec((1, 1, bm), lambda i, be, nb: (i, 0, 0), memory_space=pltpu.SMEM),
                  pl.BlockSpec((1, 1, bm), lambda i, be, nb: (jnp.minimum(i + 1, nblk - 1), 0, 0),
                               memory_space=pltpu.SMEM),
                  pl.BlockSpec(memory_space=pl.ANY),
                  pl.BlockSpec((None, d, de), lambda i, be, nb: (be[i], 0, 0)),
                  pl.BlockSpec((None, d, de), lambda i, be, nb: (be[i], 0, 0)),
                  pl.BlockSpec((None, de, d), lambda i, be, nb: (be[i], 0, 0))],
        out_specs=pl.BlockSpec((bm, d), lambda i, be, nb: (i, 0)),
        scratch_shapes=[pltpu.VMEM((2, bm, d), F32), pltpu.SemaphoreType.DMA((2,))],
    )
    return pl.pallas_call(
        _moe_ffn_kernel,
        grid_spec=grid_spec,
        out_shape=jax.ShapeDtypeStruct((nblk * bm, d), F32),
        compiler_params=_cparams(("arbitrary",)),
    )(blk_e, n_used, row_tok, row_tok, h, w1, w3, w2)


def _combine_ln_kernel(d0_ref, d1_ref, d0n_ref, d1n_ref, y_hbm, h_ref, gate_ref, g_ref, b_ref, o_ref, ob_ref,
                       ybuf, sem, *, alpha):
    i = pl.program_id(0)
    slot = i % 2

    @pl.when(i == 0)
    def _():
        _gather_start(y_hbm, d0_ref, ybuf.at[0, 0], sem.at[0, 0])
        _gather_start(y_hbm, d1_ref, ybuf.at[0, 1], sem.at[0, 1])

    @pl.when(i + 1 < pl.num_programs(0))
    def _():
        _gather_start(y_hbm, d0n_ref, ybuf.at[1 - slot, 0], sem.at[1 - slot, 0])
        _gather_start(y_hbm, d1n_ref, ybuf.at[1 - slot, 1], sem.at[1 - slot, 1])

    _gather_wait(y_hbm, ybuf.at[slot, 0], sem.at[slot, 0])
    _gather_wait(y_hbm, ybuf.at[slot, 1], sem.at[slot, 1])
    gate = gate_ref[...]
    ff = ybuf[slot, 0] * gate[:, 0:1] + ybuf[slot, 1] * gate[:, 1:2]
    y = _layernorm_rows(alpha * h_ref[...] + ff, g_ref[...], b_ref[...])
    o_ref[...] = y
    ob_ref[...] = y.astype(ob_ref.dtype)


def _combine_ln(yb, h, gates, dest0, dest1, g, b, alpha):
    r, d = h.shape
    tm = dest0.shape[2]
    nblk = r // tm
    kern = functools.partial(_combine_ln_kernel, alpha=alpha)
    idx_spec = pl.BlockSpec((1, 1, tm), lambda i: (i, 0, 0), memory_space=pltpu.SMEM)
    nxt_spec = pl.BlockSpec((1, 1, tm), lambda i: (jnp.minimum(i + 1, nblk - 1), 0, 0), memory_space=pltpu.SMEM)
    return pl.pallas_call(
        kern,
        grid=(nblk,),
        in_specs=[idx_spec, idx_spec, nxt_spec, nxt_spec,
                  pl.BlockSpec(memory_space=pl.ANY),
                  pl.BlockSpec((tm, d), lambda i: (i, 0)),
                  pl.BlockSpec((tm, LANE), lambda i: (i, 0)),
                  pl.BlockSpec((1, d), lambda i: (0, 0)),
                  pl.BlockSpec((1, d), lambda i: (0, 0))],
        out_specs=[pl.BlockSpec((tm, d), lambda i: (i, 0))] * 2,
        out_shape=[jax.ShapeDtypeStruct((r, d), F32), jax.ShapeDtypeStruct((r, d), BF16)],
        scratch_shapes=[pltpu.VMEM((2, 2, tm, d), F32), pltpu.SemaphoreType.DMA((2, 2))],
        compiler_params=_cparams(("arbitrary",)),
    )(dest0, dest1, dest0, dest1, yb, h, gates, g, b)


def _dispatch_plan(eid, n_experts, bm):
    r = eid.shape[0]
    a_tot = r * TOP_K
    e_flat = eid[:, :TOP_K].reshape(a_tot)
    onehot = (e_flat[:, None] == jnp.arange(n_experts, dtype=I32)[None, :]).astype(I32)
    before = jnp.cumsum(onehot, axis=0) - onehot
    rank = jnp.sum(before * onehot, axis=1)
    counts = jnp.sum(onehot, axis=0)
    pcounts = (counts + bm - 1) // bm * bm
    pends = jnp.cumsum(pcounts)
    pstarts = pends - pcounts
    dest = (pstarts[e_flat] + rank).astype(I32)
    nblk = -(-(a_tot + n_experts * (bm - 1)) // bm)
    row_tok = jnp.zeros((nblk * bm,), I32).at[dest].set(jnp.arange(a_tot, dtype=I32) // TOP_K)
    n_used = (pends[-1] // bm).astype(I32)
    blk = jnp.arange(nblk, dtype=I32)
    blk_e = jnp.searchsorted(pends, jnp.minimum(blk, n_used - 1) * bm, side="right").astype(I32)
    blk_e = jnp.minimum(blk_e, n_experts - 1)
    dest2 = dest.reshape(r, TOP_K)
    return row_tok.reshape(nblk, 1, bm), blk_e, n_used.reshape(1), dest2[:, 0], dest2[:, 1]


def kernel(x, meta, w_in, rw_mu, rw_w2, rw_w0, rw_a2, rw_a0, rw_g2, rw_kk, rw_ka, rw_rk, rw_gn_g, rw_gn_b,
           at_qnorm_g, at_w_uq, at_w_iq, at_kidx_g, at_kidx_b, pool_w, pool_scale, w_out, ln1_g, ln1_b,
           router_g_w, router_g_b, router_e_w, router_e_b, exp_w1, exp_w3, exp_w2, ln2_g, ln2_b):
    bsz, seq_len, d = x.shape
    depth = w_in.shape[0]
    n_meta = meta.shape[0]
    t_len = seq_len + n_meta
    p_len = _round_up(t_len, ROW_TILE)
    rows = bsz * p_len
    n_sel = min(TOPK_MAX, seq_len // 4)
    alpha = float((2 * depth) ** 0.25)

    rw = rw_w2.shape[2]
    aw = at_w_uq.shape[2]
    idx_head = at_kidx_g.shape[1]
    idx_heads = at_w_iq.shape[2] // idx_head
    assert idx_head == LANE and aw % AT_HEAD == 0 and rw % LANE == 0
    n_groups = router_g_w.shape[2]
    n_experts = router_e_w.shape[2]
    per_group = n_experts // n_groups
    assert n_groups + n_experts <= LANE
    pool_width = pool_w.shape[1] * pool_w.shape[2]

    src = {"r": rw, "k": rw, "v": rw, "dw": rw_w2.shape[1], "da": rw_a2.shape[1], "dg": rw_g2.shape[1],
           "cq": at_w_uq.shape[1], "ka": aw, "va": aw, "kidx": idx_head, "widx": idx_heads, "pin": pool_width}
    off, pw, n_proj = _layout(src)
    n_proj = _round_up(n_proj, 512)
    rw_names = ("r", "k", "v", "dw", "da", "dg")
    rw_src = {n: src[n] for n in rw_names}
    mu_starts = np.cumsum([0] + [rw_src[n] for n in rw_names])

    pos = jnp.arange(p_len, dtype=F32)
    inv = ROPE_THETA ** (-jnp.arange(0, AT_HEAD, 2, dtype=F32) / AT_HEAD)
    ang = pos[:, None] * inv[None, :]
    cosf = jnp.concatenate([jnp.cos(ang), jnp.cos(ang)], axis=1)
    sinf = jnp.concatenate([-jnp.sin(ang), jnp.sin(ang)], axis=1)

    h = jnp.concatenate([jnp.broadcast_to(meta.astype(x.dtype)[None], (bsz, n_meta, d)), x,
                         jnp.zeros((bsz, p_len - t_len, d), x.dtype)], axis=1).reshape(rows, d)
    hb = h.astype(BF16)
    row1 = lambda a: a.reshape(1, -1)
    pad_rows = lambda a, n: jnp.concatenate([a, jnp.zeros((n - a.shape[0],) + a.shape[1:], a.dtype)], axis=0)
    tm_c = ROW_TILE

    for l in range(depth):
        w_in_p = _scatter_cols(w_in[l], src, off, n_proj).astype(BF16)
        proj = _matmul(hb, w_in_p, F32)

        mu = {}
        for i, n in enumerate(rw_names):
            seg = rw_mu[l, int(mu_starts[i]):int(mu_starts[i + 1])]
            mu[n] = row1(jnp.concatenate([seg, jnp.zeros((pw[n] - src[n],), F32)]))
        g2 = pad_rows(rw_g2[l], pw["dg"]).astype(BF16)
        seqs = _rwkv_prep(proj, bsz, p_len, off, pw, mu, rw_w2[l].astype(BF16), row1(rw_w0[l]),
                          rw_a2[l].astype(BF16), row1(rw_a0[l]), g2, row1(rw_kk[l]), row1(rw_ka[l]),
                          row1(rw_rk[l]))
        y_rw = _rwkv_scan(seqs, row1(rw_gn_g[l]), row1(rw_gn_b[l]), bsz, p_len)

        q, qi, k_at, ki, wq = _dsa_prep(proj, bsz, p_len, off, pw, cosf, sinf, row1(at_qnorm_g[l]),
                                        at_w_uq[l].astype(BF16), at_w_iq[l].astype(BF16),
                                        row1(at_kidx_g[l]), row1(at_kidx_b[l]), idx_heads)
        v_at = lax.slice_in_dim(proj, off["va"], off["va"] + aw, axis=1).astype(BF16)
        y_at = _dsa_attn(q, qi, wq, k_at, v_at, ki, bsz, p_len, n_sel, idx_heads)

        y_pl = _pool_mix(proj, bsz, p_len, off, pw, pool_w[l].astype(BF16), row1(pool_scale[l]))

        mix = jnp.concatenate([y_rw, y_at, y_pl], axis=1)
        h, hb = _outproj_ln(mix, w_out[l].astype(BF16), h, row1(ln1_g[l]), row1(ln1_b[l]), alpha)

        n_r = n_groups + n_experts
        w_r = jnp.concatenate([router_g_w[l], router_e_w[l], jnp.zeros((d, LANE - n_r), F32)], axis=1).astype(BF16)
        b_r = row1(jnp.concatenate([router_g_b[l], router_e_b[l], jnp.zeros((LANE - n_r,), F32)]))
        eid, gates = _router(hb, w_r, b_r, n_groups, per_group)
        row_tok, blk_e, n_used, dest0, dest1 = _dispatch_plan(eid, n_experts, MOE_BM)
        yb = _moe_ffn(h, exp_w1[l].astype(BF16), exp_w3[l].astype(BF16), exp_w2[l].astype(BF16),
                      blk_e, n_used, row_tok)
        h, hb = _combine_ln(yb, h, gates, dest0.reshape(rows // tm_c, 1, tm_c), dest1.reshape(rows // tm_c, 1, tm_c),
                            row1(ln2_g[l]), row1(ln2_b[l]), alpha)

    return h.reshape(bsz, p_len, d)[:, n_meta:t_len]
```

```python
import functools
import math

import jax
import jax.numpy as jnp
import numpy as np
from jax import lax
from jax.experimental import pallas as pl
from jax.experimental.pallas import tpu as pltpu

F32 = jnp.float32
BF16 = jnp.bfloat16
I32 = jnp.int32

LANE = 128
ROW_TILE = 128
RW_HEAD = 64
AT_HEAD = 128
TOPK_MAX = 256
ROPE_THETA = 10000.0
POOL_WINDOWS = (2, 4, 8, 16)
POOL_HALO = 16
GN_EPS = 64e-5
LN_EPS = 1e-5
RW_CHUNK = 64
MOE_BM = 128
MOE_KC = 512
DSA_GROUPS = 4
Q_LOGIT_SCALE = AT_HEAD ** -0.5 * math.log2(math.e)
TOP_K = 2
INT_MIN = -(2 ** 31)
VMEM_LIMIT = 56 * 1024 * 1024

HIGHEST = lax.Precision.HIGHEST
NN = (((1,), (0,)), ((), ()))
NT = (((1,), (1,)), ((), ()))
TN = (((0,), (0,)), ((), ()))


def _dot(a, b, dims=NN, precision=None):
    return lax.dot_general(a, b, dims, precision=precision, preferred_element_type=F32)


def _sigmoid(x):
    return 1.0 / (1.0 + jnp.exp(-x))


def _cparams(sem):
    return pltpu.CompilerParams(dimension_semantics=sem, vmem_limit_bytes=VMEM_LIMIT)


def _round_up(x, m):
    return -(-x // m) * m


def _pick_tile(n, candidates):
    for c in candidates:
        if n % c == 0:
            return c
    return n


def _layout(sizes):
    pw = {k: _round_up(v, LANE) for k, v in sizes.items()}
    order = sorted(sizes, key=lambda k: -pw[k])
    off, gaps, cur = {}, [], 0
    for k in order:
        w = pw[k]
        placed = False
        for gi, (g0, g1) in enumerate(gaps):
            s = _round_up(g0, w)
            if s + w <= g1:
                off[k] = s
                new = [(g0, s), (s + w, g1)]
                gaps[gi:gi + 1] = [g for g in new if g[1] > g[0]]
                placed = True
                break
        if not placed:
            s = _round_up(cur, w)
            if s > cur:
                gaps.append((cur, s))
            off[k] = s
            cur = s + w
    return off, pw, cur


def _relayout_kernel(w_ref, o_ref, *, moves):
    cur = 0
    rows, total = o_ref.shape
    for s0, w, d0 in moves:
        if d0 > cur:
            o_ref[:, cur:d0] = jnp.zeros((rows, d0 - cur), o_ref.dtype)
        o_ref[:, d0:d0 + w] = w_ref[:, s0:s0 + w].astype(o_ref.dtype)
        cur = d0 + w
    if total > cur:
        o_ref[:, cur:total] = jnp.zeros((rows, total - cur), o_ref.dtype)


def _relayout_cols(w, src_sizes, off, total):
    depth, k, n_src = w.shape
    names = list(src_sizes)
    starts = np.cumsum([0] + [src_sizes[n] for n in names])
    moves = tuple(sorted(((int(starts[i]), src_sizes[n], off[n]) for i, n in enumerate(names)), key=lambda m: m[2]))
    tk = _pick_tile(k, (256, 128))
    return pl.pallas_call(
        functools.partial(_relayout_kernel, moves=moves),
        grid=(depth, k // tk),
        in_specs=[pl.BlockSpec((None, tk, n_src), lambda l, i: (l, i, 0))],
        out_specs=pl.BlockSpec((None, tk, total), lambda l, i: (l, i, 0)),
        out_shape=jax.ShapeDtypeStruct((depth, k, total), BF16),
        compiler_params=_cparams(("arbitrary", "arbitrary")),
    )(w)


def _mm_kernel(x_ref, w_ref, o_ref):
    o_ref[...] = _dot(x_ref[...], w_ref[...]).astype(o_ref.dtype)


def _matmul(x, w, layer, out_dtype):
    r, k = x.shape
    n = w.shape[2]
    tm = _pick_tile(r, (768, 512, 384, 256, 128))
    tn = _pick_tile(n, (1024, 512, 256, 128))
    return pl.pallas_call(
        _mm_kernel,
        grid=(n // tn, r // tm),
        in_specs=[pl.BlockSpec((tm, k), lambda j, i: (i, 0)),
                  pl.BlockSpec((None, k, tn), lambda j, i: (layer, 0, j))],
        out_specs=pl.BlockSpec((tm, tn), lambda j, i: (i, j)),
        out_shape=jax.ShapeDtypeStruct((r, n), out_dtype),
        compiler_params=_cparams(("arbitrary", "arbitrary")),
    )(x, w)


def _head_sums(x, hsum):
    tiles = [_dot(x[:, c * LANE:(c + 1) * LANE], hsum, precision=HIGHEST) for c in range(x.shape[1] // LANE)]
    return jnp.concatenate(tiles, axis=1)


def _split3(x):
    hi = x.astype(BF16)
    r1 = x - hi.astype(F32)
    mid = r1.astype(BF16)
    lo = (r1 - mid.astype(F32)).astype(BF16)
    return hi, mid, lo


def _rwkv_prep_kernel(r_ref, k_ref, v_ref, dw_ref, da_ref, dg_ref,
                      rh_ref, kh_ref, vh_ref, dwh_ref, dah_ref, dgh_ref,
                      mur_ref, muk_ref, muv_ref, mudw_ref, muda_ref, mudg_ref,
                      w2_ref, w0_ref, a2_ref, a0_ref, g2_ref, kkp_ref, kap_ref, rk_ref, hsum_ref,
                      tri_ref, blk_ref,
                      at_o, bt_o, kt_o, rt_o, bc_o, kc_o, v_o, epc_o, bv_o, g_o):
    first = pl.program_id(1) == 0

    def shift(x_ref, h_ref, mu_ref):
        x = x_ref[...]
        prev_row = jnp.where(first, 0.0, h_ref[7:8, :])
        row = lax.broadcasted_iota(I32, x.shape, 0)
        prev = jnp.where(row == 0, prev_row, pltpu.roll(x, 1, axis=0))
        return x + (prev - x) * mu_ref[...]

    r = shift(r_ref, rh_ref, mur_ref)
    k = shift(k_ref, kh_ref, muk_ref)
    v = shift(v_ref, vh_ref, muv_ref)
    dw = shift(dw_ref, dwh_ref, mudw_ref)
    da = shift(da_ref, dah_ref, muda_ref)
    dg = shift(dg_ref, dgh_ref, mudg_ref)

    wl = w0_ref[...] + _dot(jnp.tanh(dw).astype(BF16), w2_ref[...])
    neg = -wl
    softplus = jnp.maximum(neg, 0.0) + jnp.log(1.0 + jnp.exp(-jnp.abs(neg)))
    lw = -jnp.exp(-softplus - 0.5)
    a = _sigmoid(a0_ref[...] + _dot(da.astype(BF16), a2_ref[...]))
    g_o[...] = _dot(_sigmoid(dg).astype(BF16), g2_ref[...])
    hsum = hsum_ref[...]
    kk = k * kkp_ref[...]
    kk = kk / jnp.maximum(jnp.sqrt(_head_sums(kk * kk, hsum)), 1e-12)
    k = k * (1.0 + (a - 1.0) * kap_ref[...])
    bv_o[...] = _head_sums(r * k * rk_ref[...], hsum) * v

    parts = _split3(lw)
    tri, blk = tri_ref[...], blk_ref[...]
    cl = _dot(tri, parts[0]) + _dot(tri, parts[1]) + _dot(tri, parts[2])
    clc = _dot(blk, parts[0]) + _dot(blk, parts[1]) + _dot(blk, parts[2])
    e_n = jnp.exp(-cl)
    e_nc = jnp.exp(clc - cl)
    kka = kk * a
    at_o[...] = (-kk * jnp.exp(cl - lw)).astype(at_o.dtype)
    bt_o[...] = (kka * e_n).astype(bt_o.dtype)
    kt_o[...] = (k * e_n).astype(kt_o.dtype)
    rt_o[...] = (r * jnp.exp(cl)).astype(rt_o.dtype)
    bc_o[...] = (kka * e_nc).astype(bc_o.dtype)
    kc_o[...] = (k * e_nc).astype(kc_o.dtype)
    v_o[...] = v.astype(v_o.dtype)
    epc_o[...] = jnp.exp(clc)


def _rwkv_prep(proj, bsz, p_len, off, pw, mu, w2, w0, a2, a0, g2, kkp, kap, rk):
    rw = w2.shape[1]
    tm = ROW_TILE
    nb = p_len // tm
    names = ("r", "k", "v", "dw", "da", "dg")

    def cur_spec(n):
        w, c = pw[n], off[n] // pw[n]
        return pl.BlockSpec((tm, w), lambda b, i, c=c: (b * nb + i, c))

    def halo_spec(n):
        w, c = pw[n], off[n] // pw[n]
        return pl.BlockSpec((8, w), lambda b, i, c=c: (jnp.maximum((b * nb + i) * (tm // 8) - 1, 0), c))

    def full(a):
        return pl.BlockSpec(a.shape, lambda b, i: (0,) * a.ndim)

    lane = np.arange(LANE)
    hsum = jnp.asarray((lane[:, None] // RW_HEAD == lane[None, :] // RW_HEAD).astype(np.float32))
    t = np.arange(tm)
    same = t[:, None] // RW_CHUNK == t[None, :] // RW_CHUNK
    tri = jnp.asarray((same & (t[None, :] <= t[:, None])).astype(np.float32)).astype(BF16)
    blk = jnp.asarray(same.astype(np.float32)).astype(BF16)
    params = [mu[n] for n in names] + [w2, w0, a2, a0, g2, kkp, kap, rk, hsum, tri, blk]
    out_spec = pl.BlockSpec((tm, rw), lambda b, i: (b * nb + i, 0))
    sd = lambda dt: jax.ShapeDtypeStruct((bsz * p_len, rw), dt)
    return pl.pallas_call(
        _rwkv_prep_kernel,
        grid=(bsz, nb),
        in_specs=[cur_spec(n) for n in names] + [halo_spec(n) for n in names] + [full(a) for a in params],
        out_specs=[out_spec] * 10,
        out_shape=[sd(BF16)] * 7 + [sd(F32)] * 3,
        compiler_params=_cparams(("arbitrary", "arbitrary")),
    )(*([proj] * 12), *params)


def _rwkv_heads(at, bt, kt, rt, bc, kc, v, p_row, zt):
    n = len(at)
    hs = range(n)
    c = at[0].shape[0]
    b16 = lambda x: x.astype(BF16)
    row = lax.broadcasted_iota(I32, (2 * c, c), 0)
    col = lax.broadcasted_iota(I32, (2 * c, c), 1)
    keep = col < (row & (c - 1)) + jnp.where(row < c, 0, 1)
    lhs2 = [jnp.concatenate([at[h], rt[h]], axis=0) for h in hs]
    x_b = [jnp.where(keep, _dot(lhs2[h], bt[h], NT), 0.0) for h in hs]
    x_k = [jnp.where(keep, _dot(lhs2[h], kt[h], NT), 0.0) for h in hs]
    lkmv = [_dot(b16(x_k[h]), v[h]) for h in hs]
    l_ba = [x[:c] for x in x_b]
    m_b = [b16(x[c:]) for x in x_b]

    ti = lax.broadcasted_iota(I32, (c, c), 0)
    si = lax.broadcasted_iota(I32, (c, c), 1)
    zero = jnp.zeros((c, c), F32)
    eye = (ti == si).astype(F32)
    same_lo = (ti >> 4) == (si >> 4)
    lp = [jnp.where(same_lo, l, zero) for l in l_ba]
    t = [eye + l for l in lp]
    for _ in range(3):
        lp = [_dot(b16(l), b16(l)) for l in lp]
        t = [t[h] + _dot(b16(lp[h]), b16(t[h])) for h in hs]
    size = 32
    while size <= c:
        shift = size.bit_length() - 1
        same_hi = (ti >> shift) == (si >> shift)
        l_off = [b16(jnp.where(same_hi, jnp.where(same_lo, zero, l), zero)) for l in l_ba]
        tl = [_dot(b16(t[h]), l_off[h]) for h in hs]
        t = [t[h] + _dot(b16(tl[h]), b16(t[h])) for h in hs]
        same_lo = same_hi
        size *= 2

    tb = [b16(x) for x in t]
    a_z = [b16(_dot(tb[h], at[h])) for h in hs]
    w_u = [b16(_dot(tb[h], b16(lkmv[h][:c]))) for h in hs]
    ztb = [b16(z) for z in zt]
    q_z = [b16(rt[h].astype(F32) + _dot(m_b[h], a_z[h])) for h in hs]
    y = [_dot(q_z[h], ztb[h], NT) + _dot(m_b[h], w_u[h]) + lkmv[h][c:] for h in hs]
    g_t = [b16(_dot(a_z[h], bc[h], TN)) for h in hs]
    h_t = [_dot(w_u[h], bc[h], TN) + _dot(v[h], kc[h], TN) for h in hs]
    zt_new = [p_row[h] * zt[h] + _dot(ztb[h], g_t[h]) + h_t[h] for h in hs]
    return y, zt_new


def _rwkv_scan_kernel(at_ref, bt_ref, kt_ref, rt_ref, bc_ref, kc_ref, v_ref, epc_ref, bv_ref, g_ref,
                      gg_ref, gb_ref, mix_ref, o_ref, z_ref):
    del mix_ref
    @pl.when(pl.program_id(2) == 0)
    def _():
        z_ref[...] = jnp.zeros_like(z_ref)

    n_heads = o_ref.shape[1] // RW_HEAD
    sls = [slice(h * RW_HEAD, (h + 1) * RW_HEAD) for h in range(n_heads)]
    cut = lambda ref: [ref[:, sl] for sl in sls]
    y, z_new = _rwkv_heads(cut(at_ref), cut(bt_ref), cut(kt_ref), cut(rt_ref), cut(bc_ref), cut(kc_ref),
                           cut(v_ref), [epc_ref[0:1, sl] for sl in sls], [z_ref[h] for h in range(n_heads)])
    for h in range(n_heads):
        z_ref[h] = z_new[h]
    outs = []
    for h, sl in enumerate(sls):
        mu = jnp.mean(y[h], axis=-1, keepdims=True)
        var = jnp.mean(jnp.square(y[h] - mu), axis=-1, keepdims=True)
        yn = (y[h] - mu) * lax.rsqrt(var + GN_EPS) * gg_ref[:, sl] + gb_ref[:, sl]
        outs.append((yn + bv_ref[:, sl]) * g_ref[:, sl])
    per = LANE // RW_HEAD
    for p in range(n_heads // per):
        o_ref[:, p * LANE:(p + 1) * LANE] = jnp.concatenate(outs[p * per:(p + 1) * per], axis=1).astype(o_ref.dtype)


def _rwkv_scan(seqs, gn_g, gn_b, mix):
    bsz, p_len, _ = mix.shape
    rw = gn_g.shape[1]
    width = _pick_tile(rw, (12 * LANE, 4 * LANE, 3 * LANE, 2 * LANE, LANE))
    nch = p_len // RW_CHUNK
    seqs = [s.reshape(bsz, p_len, rw) for s in seqs]
    seq_spec = pl.BlockSpec((None, RW_CHUNK, width), lambda b, p, c: (b, c, p))
    par_spec = pl.BlockSpec((1, width), lambda b, p, c: (0, p))
    return pl.pallas_call(
        _rwkv_scan_kernel,
        grid=(bsz, rw // width, nch),
        in_specs=[seq_spec] * len(seqs) + [par_spec] * 2 + [pl.BlockSpec(memory_space=pl.ANY)],
        out_specs=seq_spec,
        out_shape=jax.ShapeDtypeStruct(mix.shape, mix.dtype),
        input_output_aliases={len(seqs) + 2: 0},
        scratch_shapes=[pltpu.VMEM((width // RW_HEAD, RW_HEAD, RW_HEAD), F32)],
        compiler_params=_cparams(("arbitrary", "arbitrary", "arbitrary")),
    )(*seqs, gn_g, gn_b, mix)


def _rope_tiles(x, cosf, sinf):
    outs = []
    for c in range(x.shape[1] // LANE):
        xt = x[:, c * LANE:(c + 1) * LANE]
        outs.append(xt * cosf + pltpu.roll(xt, LANE // 2, axis=1) * sinf)
    return outs


def _dsa_prep_kernel(cq_ref, ka_ref, va_ref, kidx_ref, widx_ref, cos_ref, sin_ref, qg_ref, wuq_ref, wiq_ref,
                     kg_ref, kb_ref, q_o, qi_o, k_o, v_o, ki_o, wq_o, *, wq_scale):
    cosf, sinf = cos_ref[...], sin_ref[...]
    v_o[...] = va_ref[...].astype(v_o.dtype)
    cq = cq_ref[...]
    cqn = cq * lax.rsqrt(jnp.mean(jnp.square(cq), axis=-1, keepdims=True) + 1e-6) * qg_ref[...]
    cqb = cqn.astype(BF16)
    for c, t in enumerate(_rope_tiles(_dot(cqb, wuq_ref[...]), cosf, sinf)):
        q_o[:, c * LANE:(c + 1) * LANE] = (t * Q_LOGIT_SCALE).astype(q_o.dtype)
    for c, t in enumerate(_rope_tiles(_dot(cqb, wiq_ref[...]), cosf, sinf)):
        qi_o[:, c * LANE:(c + 1) * LANE] = t.astype(qi_o.dtype)
    for c, t in enumerate(_rope_tiles(ka_ref[...], cosf, sinf)):
        k_o[:, c * LANE:(c + 1) * LANE] = t.astype(k_o.dtype)
    kx = kidx_ref[...]
    mu = jnp.mean(kx, axis=-1, keepdims=True)
    var = jnp.mean(jnp.square(kx - mu), axis=-1, keepdims=True)
    kn = (kx - mu) * lax.rsqrt(var + LN_EPS) * kg_ref[...] + kb_ref[...]
    ki_o[...] = _rope_tiles(kn, cosf, sinf)[0].astype(ki_o.dtype)
    wq_o[...] = widx_ref[...] * wq_scale


def _dsa_prep(proj, bsz, p_len, off, pw, cosf, sinf, qg, wuq, wiq, kg, kb, idx_heads):
    tm = _pick_tile(p_len, (256, 128))
    nb = p_len // tm
    rows = bsz * p_len
    aw, iw = wuq.shape[1], wiq.shape[1]

    def seg(n):
        w, c = pw[n], off[n] // pw[n]
        return pl.BlockSpec((tm, w), lambda i, c=c: (i, c))

    def full(a):
        return pl.BlockSpec(a.shape, lambda i: (0,) * a.ndim)

    pos_spec = pl.BlockSpec((tm, LANE), lambda i: (i % nb, 0))
    row_spec = lambda w: pl.BlockSpec((tm, w), lambda i: (i, 0))
    kern = functools.partial(_dsa_prep_kernel, wq_scale=float(idx_heads ** -0.5 * AT_HEAD ** -0.5))
    return pl.pallas_call(
        kern,
        grid=(rows // tm,),
        in_specs=[seg("cq"), seg("ka"), seg("va"), seg("kidx"), seg("widx"), pos_spec, pos_spec,
                  full(qg), full(wuq), full(wiq), full(kg), full(kb)],
        out_specs=[row_spec(aw), row_spec(iw), row_spec(aw), row_spec(aw), row_spec(LANE), row_spec(LANE)],
        out_shape=[jax.ShapeDtypeStruct((rows, aw), BF16), jax.ShapeDtypeStruct((rows, iw), BF16),
                   jax.ShapeDtypeStruct((rows, aw), BF16), jax.ShapeDtypeStruct((rows, aw), BF16),
                   jax.ShapeDtypeStruct((rows, LANE), BF16), jax.ShapeDtypeStruct((rows, LANE), F32)],
        compiler_params=_cparams(("arbitrary",)),
    )(proj, proj, proj, proj, proj, cosf, sinf, qg, wuq, wiq, kg, kb)


def _topk_mask(sc, causal, n_sel):
    tq, s_len = sc.shape
    bits = lax.bitcast_convert_type(jnp.where(sc == 0.0, 0.0, sc), I32)
    key = jnp.where(bits < 0, bits ^ jnp.int32(0x7FFFFFFF), bits)
    key = jnp.where(causal, key, jnp.int32(INT_MIN))
    one = jnp.ones((tq, s_len), I32)
    zero = jnp.zeros((tq, s_len), I32)

    i16 = jnp.int16
    one16 = jnp.ones((tq, s_len), i16)
    zero16 = jnp.zeros((tq, s_len), i16)

    def count_ge16(x16, th):
        hit = jnp.where(x16 >= th.astype(i16), one16, zero16)
        acc = hit[:, :LANE]
        for j in range(1, s_len // LANE):
            acc = acc + hit[:, j * LANE:(j + 1) * LANE]
        return jnp.sum(acc.astype(I32), axis=1, keepdims=True)

    def kth_largest16(x16):
        t = jnp.where(count_ge16(x16, jnp.zeros((tq, 1), I32)) >= n_sel, jnp.int32(0), jnp.int32(-32768))

        def step(i, t):
            cand = t | jnp.left_shift(jnp.int32(1), 14 - i)
            return jnp.where(count_ge16(x16, cand) >= n_sel, cand, t)

        return lax.fori_loop(0, 15, step, t)

    hi = (key >> 16).astype(i16)
    lo = ((key & jnp.int32(0xFFFF)) - 32768).astype(i16)
    tau_hi = kth_largest16(hi)
    th16 = tau_hi.astype(i16)
    lo_sel = jnp.where(hi == th16, lo, jnp.where(hi > th16, i16(32767), i16(-32768)))
    tau = tau_hi * 65536 + (kth_largest16(lo_sel) + 32768)
    gt = key > tau
    eq = key == tau
    n_gt = jnp.sum(jnp.where(gt, one, zero), axis=1, keepdims=True)
    n_eq = jnp.sum(jnp.where(eq, one, zero), axis=1, keepdims=True)
    need = n_sel - n_gt
    idx = lax.broadcasted_iota(I32, (tq, s_len), 1)
    nbits = max(1, (s_len - 1).bit_length())

    def cut_search():
        def cut_step(i, lo):
            cand = lo | jnp.left_shift(jnp.int32(1), nbits - 1 - i)
            cnt = jnp.sum(jnp.where(eq, jnp.where(idx < cand, one, zero), zero), axis=1, keepdims=True)
            return jnp.where(cnt < need, cand, lo)
        return lax.fori_loop(0, nbits, cut_step, jnp.zeros((tq, 1), I32))

    surplus = jnp.where(tau > jnp.int32(INT_MIN), jnp.where(n_eq > need, 1, 0), 0)
    cut = lax.cond(jnp.max(surplus) > 0, cut_search, lambda: jnp.full((tq, 1), s_len, I32))
    sel = jnp.where(gt, one, jnp.where(eq, jnp.where(idx <= cut, one, zero), zero))
    return jnp.where(causal, sel, zero) > 0


def _dsa_attn_kernel(q_ref, qi_ref, wq_ref, k_ref, v_ref, ki_ref, mix_ref, o_ref, *, n_sel, idx_heads, q_lo):
    del mix_ref
    tq = q_ref.shape[0]
    s_len = k_ref.shape[0]
    t0 = (pl.program_id(1) + q_lo) * tq
    ki = ki_ref[...]
    wq = wq_ref[...]
    sc = jnp.zeros((tq, s_len), F32)
    for h in range(idx_heads):
        s_h = _dot(qi_ref[:, h * LANE:(h + 1) * LANE], ki, NT)
        sc = sc + jnp.maximum(s_h, 0.0) * wq[:, h:h + 1]
    qpos = t0 + lax.broadcasted_iota(I32, (tq, s_len), 0)
    kpos = lax.broadcasted_iota(I32, (tq, s_len), 1)
    mask = _topk_mask(sc, kpos <= qpos, n_sel)
    bias = jnp.where(mask, 0.0, -jnp.inf)
    for h in range(q_ref.shape[1] // AT_HEAD):
        sl = slice(h * AT_HEAD, (h + 1) * AT_HEAD)
        lg = _dot(q_ref[:, sl], k_ref[:, sl], NT) + bias
        p = jnp.exp2(lg - jnp.max(lg, axis=1, keepdims=True))
        den = jnp.sum(p, axis=1, keepdims=True)
        o_ref[:, sl] = (_dot(p.astype(BF16), v_ref[:, sl]) / den).astype(o_ref.dtype)


def _dsa_attn(q, qi, wq, k, v, ki, mix, col, n_sel, idx_heads):
    bsz, p_len, _ = mix.shape
    aw, iw = q.shape[1], qi.shape[1]
    assert col % aw == 0
    tq = ROW_TILE
    nq = p_len // tq
    n_groups = min(DSA_GROUPS, nq)
    bounds = [round(nq * g / n_groups) for g in range(n_groups + 1)]
    r3 = lambda a: a.reshape(bsz, p_len, a.shape[1])
    args = (r3(q), r3(qi), r3(wq), r3(k), r3(v), r3(ki))
    for lo, hi in zip(bounds[:-1], bounds[1:]):
        s_len = hi * tq
        assert s_len >= n_sel
        qspec = lambda w, lo=lo: pl.BlockSpec((None, tq, w), lambda b, i: (b, i + lo, 0))
        kspec = lambda w, s_len=s_len: pl.BlockSpec((None, s_len, w), lambda b, i: (b, 0, 0),
                                                    pipeline_mode=pl.Buffered(1))
        kern = functools.partial(_dsa_attn_kernel, n_sel=n_sel, idx_heads=idx_heads, q_lo=lo)
        mix = pl.pallas_call(
            kern,
            grid=(bsz, hi - lo),
            in_specs=[qspec(aw), qspec(iw), qspec(LANE), kspec(aw), kspec(aw), kspec(LANE),
                      pl.BlockSpec(memory_space=pl.ANY)],
            out_specs=pl.BlockSpec((None, tq, aw), lambda b, i, lo=lo: (b, i + lo, col // aw)),
            out_shape=jax.ShapeDtypeStruct(mix.shape, mix.dtype),
            input_output_aliases={6: 0},
            compiler_params=_cparams(("arbitrary", "arbitrary")),
        )(*args, mix)
    return mix


def _pool_kernel(p_ref, h_ref, w_ref, s_ref, mix_ref, o_ref):
    del mix_ref
    tm, width = p_ref.shape
    grp = width // len(POOL_WINDOWS)
    t0 = pl.program_id(1) * tm
    x = p_ref[...]
    halo = jnp.where(pl.program_id(1) == 0, 0.0, h_ref[...])
    ext = jnp.concatenate([halo, x], axis=0)
    tpos = t0 + lax.broadcasted_iota(I32, (tm, 1), 0)
    acc = ext
    have = 1
    for gi, win in enumerate(POOL_WINDOWS):
        while have < win:
            shifted = jnp.concatenate([jnp.zeros((have, width), F32), acc[:-have]], axis=0)
            acc = acc + shifted
            have *= 2
        sl = slice(gi * grp, (gi + 1) * grp)
        cnt = jnp.minimum(tpos + 1, win).astype(F32)
        pooled = acc[POOL_HALO:, sl] / cnt - x[:, sl]
        y = _dot(pooled.astype(BF16), w_ref[gi])
        o_ref[:, sl] = (y * s_ref[:, sl]).astype(o_ref.dtype)


def _pool_mix(proj, off, pw, w_pool, scale, mix, col):
    bsz, p_len, d_mix = mix.shape
    width = pw["pin"]
    assert col % width == 0
    tm = _pick_tile(p_len, (256, 128))
    nb = p_len // tm
    c = off["pin"] // width
    hb = tm // POOL_HALO
    mix2 = mix.reshape(bsz * p_len, d_mix)
    out = pl.pallas_call(
        _pool_kernel,
        grid=(bsz, nb),
        in_specs=[pl.BlockSpec((tm, width), lambda b, i: (b * nb + i, c)),
                  pl.BlockSpec((POOL_HALO, width), lambda b, i: (jnp.maximum((b * nb + i) * hb - 1, 0), c)),
                  pl.BlockSpec(w_pool.shape, lambda b, i: (0, 0, 0)),
                  pl.BlockSpec(scale.shape, lambda b, i: (0, 0)),
                  pl.BlockSpec(memory_space=pl.ANY)],
        out_specs=pl.BlockSpec((tm, width), lambda b, i: (b * nb + i, col // width)),
        out_shape=jax.ShapeDtypeStruct(mix2.shape, mix2.dtype),
        input_output_aliases={4: 0},
        compiler_params=_cparams(("arbitrary", "arbitrary")),
    )(proj, proj, w_pool, scale, mix2)
    return out


def _layernorm_rows(x, g, b):
    mu = jnp.mean(x, axis=-1, keepdims=True)
    var = jnp.mean(jnp.square(x - mu), axis=-1, keepdims=True)
    return (x - mu) * lax.rsqrt(var + LN_EPS) * g + b


def _outproj_ln_kernel(x_ref, w_ref, h_ref, g_ref, b_ref, o_ref, ob_ref, acc_ref, *, alpha):
    kk = pl.program_id(1)

    @pl.when(kk == 0)
    def _():
        acc_ref[...] = jnp.zeros_like(acc_ref)

    acc_ref[...] += _dot(x_ref[...], w_ref[...])

    @pl.when(kk == pl.num_programs(1) - 1)
    def _():
        y = _layernorm_rows(alpha * h_ref[...] + acc_ref[...], g_ref[...], b_ref[...])
        o_ref[...] = y
        ob_ref[...] = y.astype(ob_ref.dtype)


def _outproj_ln(mix, w_out, h, g, b, alpha):
    r, k = mix.shape
    d = w_out.shape[1]
    tm = _pick_tile(r, (384, 256, 128))
    tk = _pick_tile(k, (512, 256, 128))
    kern = functools.partial(_outproj_ln_kernel, alpha=alpha)
    return pl.pallas_call(
        kern,
        grid=(r // tm, k // tk),
        in_specs=[pl.BlockSpec((tm, tk), lambda i, j: (i, j)),
                  pl.BlockSpec((tk, d), lambda i, j: (j, 0)),
                  pl.BlockSpec((tm, d), lambda i, j: (i, 0)),
                  pl.BlockSpec((1, d), lambda i, j: (0, 0)),
                  pl.BlockSpec((1, d), lambda i, j: (0, 0))],
        out_specs=[pl.BlockSpec((tm, d), lambda i, j: (i, 0))] * 2,
        out_shape=[jax.ShapeDtypeStruct((r, d), F32), jax.ShapeDtypeStruct((r, d), BF16)],
        scratch_shapes=[pltpu.VMEM((tm, d), F32)],
        compiler_params=_cparams(("arbitrary", "arbitrary")),
    )(mix, w_out, h, g, b)


def _router_kernel(x_ref, w_ref, b_ref, e_o, g_o, *, n_groups, per_group):
    logits = _dot(x_ref[...], w_ref[...]) + b_ref[...]
    lane = lax.broadcasted_iota(I32, logits.shape, 1)
    big = jnp.int32(LANE)
    ninf = -jnp.inf
    gl = jnp.where(lane < n_groups, logits, ninf)
    ge = jnp.exp(gl - jnp.max(gl, axis=1, keepdims=True))
    gp = ge / jnp.sum(ge, axis=1, keepdims=True)
    g_val = jnp.max(gp, axis=1, keepdims=True)
    g_idx = jnp.min(jnp.where(gp == g_val, lane, big), axis=1, keepdims=True)
    lane_grp = jnp.where(lane >= n_groups, (lane - n_groups) // per_group, -1)
    in_grp = lane_grp == g_idx
    el = jnp.where(in_grp, logits, ninf)
    ee = jnp.exp(el - jnp.max(el, axis=1, keepdims=True))
    ep = jnp.where(in_grp, ee / jnp.sum(ee, axis=1, keepdims=True), -1.0)
    v1 = jnp.max(ep, axis=1, keepdims=True)
    i1 = jnp.min(jnp.where(ep == v1, lane, big), axis=1, keepdims=True)
    ep2 = jnp.where(lane == i1, -1.0, ep)
    v2 = jnp.max(ep2, axis=1, keepdims=True)
    i2 = jnp.min(jnp.where(ep2 == v2, lane, big), axis=1, keepdims=True)
    tot = v1 + v2
    e_o[...] = jnp.where(lane == 0, i1 - n_groups, jnp.where(lane == 1, i2 - n_groups, 0))
    g_o[...] = jnp.where(lane == 0, g_val * v1 / tot, jnp.where(lane == 1, g_val * v2 / tot, 0.0))


def _router(hb, w, b, n_groups, per_group):
    r, d = hb.shape
    tm = _pick_tile(r, (256, 128))
    kern = functools.partial(_router_kernel, n_groups=n_groups, per_group=per_group)
    return pl.pallas_call(
        kern,
        grid=(r // tm,),
        in_specs=[pl.BlockSpec((tm, d), lambda i: (i, 0)),
                  pl.BlockSpec((d, LANE), lambda i: (0, 0)),
                  pl.BlockSpec((1, LANE), lambda i: (0, 0))],
        out_specs=[pl.BlockSpec((tm, LANE), lambda i: (i, 0))] * 2,
        out_shape=[jax.ShapeDtypeStruct((r, LANE), I32), jax.ShapeDtypeStruct((r, LANE), F32)],
        compiler_params=_cparams(("arbitrary",)),
    )(hb, w, b)


def _gather_start(src_hbm, idx_ref, dst, sem):
    def issue(j, carry):
        pltpu.make_async_copy(src_hbm.at[pl.ds(idx_ref[0, 0, j], 1), :], dst.at[pl.ds(j, 1), :], sem).start()
        return carry
    lax.fori_loop(0, dst.shape[0], issue, 0)


def _gather_wait(src_hbm, dst, sem):
    pltpu.make_async_copy(src_hbm.at[pl.ds(0, dst.shape[0]), :], dst, sem).wait()


def _moe_ffn_kernel(be_ref, nb_ref, tok_ref, tok_next_ref, h_hbm, w1_ref, w3_ref, w2_ref, y_ref, xbuf, sem):
    del be_ref
    i = pl.program_id(0)
    n_used = nb_ref[0]
    slot = i % 2

    @pl.when(i == 0)
    def _():
        _gather_start(h_hbm, tok_ref, xbuf.at[0], sem.at[0])

    @pl.when(i + 1 < n_used)
    def _():
        _gather_start(h_hbm, tok_next_ref, xbuf.at[1 - slot], sem.at[1 - slot])

    @pl.when(i < n_used)
    def _():
        _gather_wait(h_hbm, xbuf.at[slot], sem.at[slot])
        d, de = w1_ref.shape
        h1 = jnp.zeros((xbuf.shape[1], de), F32)
        h3 = jnp.zeros((xbuf.shape[1], de), F32)
        for c in range(d // MOE_KC):
            ks = slice(c * MOE_KC, (c + 1) * MOE_KC)
            x = xbuf[slot, :, ks].astype(BF16)
            h1 = h1 + _dot(x, w1_ref[ks, :].astype(BF16))
            h3 = h3 + _dot(x, w3_ref[ks, :].astype(BF16))
        act = (h1 * _sigmoid(h1) * h3).astype(BF16)
        for c in range(d // MOE_KC):
            ks = slice(c * MOE_KC, (c + 1) * MOE_KC)
            y_ref[:, ks] = _dot(act, w2_ref[:, ks].astype(BF16))

    @pl.when(i >= n_used)
    def _():
        y_ref[...] = jnp.zeros_like(y_ref)


def _moe_ffn(h, w1, w3, w2, layer, blk_e, n_used, row_tok):
    r, d = h.shape
    nblk = blk_e.shape[0]
    de = w1.shape[3]
    bm = row_tok.shape[2]
    grid_spec = pltpu.PrefetchScalarGridSpec(
        num_scalar_prefetch=2,
        grid=(nblk,),
        in_specs=[pl.BlockSpec((1, 1, bm), lambda i, be, nb: (i, 0, 0), memory_space=pltpu.SMEM),
                  pl.BlockSpec((1, 1, bm), lambda i, be, nb: (jnp.minimum(i + 1, nblk - 1), 0, 0),
                               memory_space=pltpu.SMEM),
                  pl.BlockSpec(memory_space=pl.ANY),
                  pl.BlockSpec((None, None, d, de), lambda i, be, nb: (layer, be[i], 0, 0)),
                  pl.BlockSpec((None, None, d, de), lambda i, be, nb: (layer, be[i], 0, 0)),
                  pl.BlockSpec((None, None, de, d), lambda i, be, nb: (layer, be[i], 0, 0),
                               pipeline_mode=pl.Buffered(1))],
        out_specs=pl.BlockSpec((bm, d), lambda i, be, nb: (i, 0)),
        scratch_shapes=[pltpu.VMEM((2, bm, d), F32), pltpu.SemaphoreType.DMA((2,))],
    )
    return pl.pallas_call(
        _moe_ffn_kernel,
        grid_spec=grid_spec,
        out_shape=jax.ShapeDtypeStruct((nblk * bm, d), F32),
        compiler_params=_cparams(("arbitrary",)),
    )(blk_e, n_used, row_tok, row_tok, h, w1, w3, w2)


def _combine_ln_kernel(d0_ref, d1_ref, d0n_ref, d1n_ref, y_hbm, h_ref, gate_ref, g_ref, b_ref, o_ref, ob_ref,
                       ybuf, sem, *, alpha):
    i = pl.program_id(0)
    slot = i % 2

    @pl.when(i == 0)
    def _():
        _gather_start(y_hbm, d0_ref, ybuf.at[0, 0], sem.at[0, 0])
        _gather_start(y_hbm, d1_ref, ybuf.at[0, 1], sem.at[0, 1])

    @pl.when(i + 1 < pl.num_programs(0))
    def _():
        _gather_start(y_hbm, d0n_ref, ybuf.at[1 - slot, 0], sem.at[1 - slot, 0])
        _gather_start(y_hbm, d1n_ref, ybuf.at[1 - slot, 1], sem.at[1 - slot, 1])

    _gather_wait(y_hbm, ybuf.at[slot, 0], sem.at[slot, 0])
    _gather_wait(y_hbm, ybuf.at[slot, 1], sem.at[slot, 1])
    gate = gate_ref[...]
    ff = ybuf[slot, 0] * gate[:, 0:1] + ybuf[slot, 1] * gate[:, 1:2]
    y = _layernorm_rows(alpha * h_ref[...] + ff, g_ref[...], b_ref[...])
    o_ref[...] = y
    ob_ref[...] = y.astype(ob_ref.dtype)


def _combine_ln(yb, h, gates, dest0, dest1, g, b, alpha):
    r, d = h.shape
    tm = dest0.shape[2]
    nblk = r // tm
    kern = functools.partial(_combine_ln_kernel, alpha=alpha)
    idx_spec = pl.BlockSpec((1, 1, tm), lambda i: (i, 0, 0), memory_space=pltpu.SMEM)
    nxt_spec = pl.BlockSpec((1, 1, tm), lambda i: (jnp.minimum(i + 1, nblk - 1), 0, 0), memory_space=pltpu.SMEM)
    return pl.pallas_call(
        kern,
        grid=(nblk,),
        in_specs=[idx_spec, idx_spec, nxt_spec, nxt_spec,
                  pl.BlockSpec(memory_space=pl.ANY),
                  pl.BlockSpec((tm, d), lambda i: (i, 0)),
                  pl.BlockSpec((tm, LANE), lambda i: (i, 0)),
                  pl.BlockSpec((1, d), lambda i: (0, 0)),
                  pl.BlockSpec((1, d), lambda i: (0, 0))],
        out_specs=[pl.BlockSpec((tm, d), lambda i: (i, 0))] * 2,
        out_shape=[jax.ShapeDtypeStruct((r, d), F32), jax.ShapeDtypeStruct((r, d), BF16)],
        scratch_shapes=[pltpu.VMEM((2, 2, tm, d), F32), pltpu.SemaphoreType.DMA((2, 2))],
        compiler_params=_cparams(("arbitrary",)),
    )(dest0, dest1, dest0, dest1, yb, h, gates, g, b)


def _dispatch_plan(eid, n_experts, bm):
    r = eid.shape[0]
    a_tot = r * TOP_K
    e_flat = eid[:, :TOP_K].reshape(a_tot)
    onehot = (e_flat[:, None] == jnp.arange(n_experts, dtype=I32)[None, :]).astype(I32)
    before = jnp.cumsum(onehot, axis=0) - onehot
    rank = jnp.sum(before * onehot, axis=1)
    counts = jnp.sum(onehot, axis=0)
    pcounts = (counts + bm - 1) // bm * bm
    pends = jnp.cumsum(pcounts)
    pstarts = pends - pcounts
    dest = (pstarts[e_flat] + rank).astype(I32)
    nblk = -(-(a_tot + n_experts * (bm - 1)) // bm)
    row_tok = jnp.zeros((nblk * bm,), I32).at[dest].set(jnp.arange(a_tot, dtype=I32) // TOP_K)
    n_used = (pends[-1] // bm).astype(I32)
    blk = jnp.arange(nblk, dtype=I32)
    blk_e = jnp.searchsorted(pends, jnp.minimum(blk, n_used - 1) * bm, side="right").astype(I32)
    blk_e = jnp.minimum(blk_e, n_experts - 1)
    dest2 = dest.reshape(r, TOP_K)
    return row_tok.reshape(nblk, 1, bm), blk_e, n_used.reshape(1), dest2[:, 0], dest2[:, 1]


def kernel(x, meta, w_in, rw_mu, rw_w2, rw_w0, rw_a2, rw_a0, rw_g2, rw_kk, rw_ka, rw_rk, rw_gn_g, rw_gn_b,
           at_qnorm_g, at_w_uq, at_w_iq, at_kidx_g, at_kidx_b, pool_w, pool_scale, w_out, ln1_g, ln1_b,
           router_g_w, router_g_b, router_e_w, router_e_b, exp_w1, exp_w3, exp_w2, ln2_g, ln2_b):
    bsz, seq_len, d = x.shape
    depth = w_in.shape[0]
    n_meta = meta.shape[0]
    t_len = seq_len + n_meta
    p_len = _round_up(t_len, ROW_TILE)
    rows = bsz * p_len
    n_sel = min(TOPK_MAX, seq_len // 4)
    alpha = float((2 * depth) ** 0.25)

    rw = rw_w2.shape[2]
    aw = at_w_uq.shape[2]
    idx_head = at_kidx_g.shape[1]
    idx_heads = at_w_iq.shape[2] // idx_head
    assert idx_head == LANE and aw % AT_HEAD == 0 and rw % LANE == 0
    n_groups = router_g_w.shape[2]
    n_experts = router_e_w.shape[2]
    per_group = n_experts // n_groups
    assert n_groups + n_experts <= LANE
    pool_width = pool_w.shape[1] * pool_w.shape[2]

    src = {"r": rw, "k": rw, "v": rw, "dw": rw_w2.shape[1], "da": rw_a2.shape[1], "dg": rw_g2.shape[1],
           "cq": at_w_uq.shape[1], "ka": aw, "va": aw, "kidx": idx_head, "widx": idx_heads, "pin": pool_width}
    off, pw, n_proj = _layout(src)
    n_proj = _round_up(n_proj, 512)
    rw_names = ("r", "k", "v", "dw", "da", "dg")
    rw_src = {n: src[n] for n in rw_names}
    mu_starts = np.cumsum([0] + [rw_src[n] for n in rw_names])

    pos = jnp.arange(p_len, dtype=F32)
    inv = ROPE_THETA ** (-jnp.arange(0, AT_HEAD, 2, dtype=F32) / AT_HEAD)
    ang = pos[:, None] * inv[None, :]
    cosf = jnp.concatenate([jnp.cos(ang), jnp.cos(ang)], axis=1)
    sinf = jnp.concatenate([-jnp.sin(ang), jnp.sin(ang)], axis=1)

    h = jnp.concatenate([jnp.broadcast_to(meta.astype(x.dtype)[None], (bsz, n_meta, d)), x,
                         jnp.zeros((bsz, p_len - t_len, d), x.dtype)], axis=1).reshape(rows, d)
    hb = h.astype(BF16)
    row1 = lambda a: a.reshape(1, -1)
    pad_rows = lambda a, n: jnp.concatenate([a, jnp.zeros((n - a.shape[0],) + a.shape[1:], a.dtype)], axis=0)
    tm_c = ROW_TILE

    w_in_p = _relayout_cols(w_in, src, off, n_proj)

    for l in range(depth):
        proj = _matmul(hb, w_in_p, l, F32)

        mu = {}
        for i, n in enumerate(rw_names):
            seg = rw_mu[l, int(mu_starts[i]):int(mu_starts[i + 1])]
            mu[n] = row1(jnp.concatenate([seg, jnp.zeros((pw[n] - src[n],), F32)]))
        g2 = pad_rows(rw_g2[l], pw["dg"]).astype(BF16)
        seqs = _rwkv_prep(proj, bsz, p_len, off, pw, mu, rw_w2[l].astype(BF16), row1(rw_w0[l]),
                          rw_a2[l].astype(BF16), row1(rw_a0[l]), g2, row1(rw_kk[l]), row1(rw_ka[l]),
                          row1(rw_rk[l]))
        mix = jnp.zeros((bsz, p_len, rw + aw + pool_width), BF16)
        mix = _rwkv_scan(seqs, row1(rw_gn_g[l]), row1(rw_gn_b[l]), mix)

        q, qi, k_at, v_at, ki, wq = _dsa_prep(proj, bsz, p_len, off, pw, cosf, sinf, row1(at_qnorm_g[l]),
                                              at_w_uq[l].astype(BF16), at_w_iq[l].astype(BF16),
                                              row1(at_kidx_g[l]), row1(at_kidx_b[l]), idx_heads)
        mix = _dsa_attn(q, qi, wq, k_at, v_at, ki, mix, rw, n_sel, idx_heads)
        mix = _pool_mix(proj, off, pw, pool_w[l].astype(BF16), row1(pool_scale[l]), mix, rw + aw)
        h, hb = _outproj_ln(mix, w_out[l].astype(BF16), h, row1(ln1_g[l]), row1(ln1_b[l]), alpha)

        n_r = n_groups + n_experts
        w_r = jnp.concatenate([router_g_w[l], router_e_w[l], jnp.zeros((d, LANE - n_r), F32)], axis=1).astype(BF16)
        b_r = row1(jnp.concatenate([router_g_b[l], router_e_b[l], jnp.zeros((LANE - n_r,), F32)]))
        eid, gates = _router(hb, w_r, b_r, n_groups, per_group)
        row_tok, blk_e, n_used, dest0, dest1 = _dispatch_plan(eid, n_experts, MOE_BM)
        yb = _moe_ffn(h, exp_w1, exp_w3, exp_w2, l, blk_e, n_used, row_tok)
        h, hb = _combine_ln(yb, h, gates, dest0.reshape(rows // tm_c, 1, tm_c), dest1.reshape(rows // tm_c, 1, tm_c),
                            row1(ln2_g[l]), row1(ln2_b[l]), alpha)

    return h.reshape(bsz, p_len, d)[:, n_meta:t_len]
```

```python
import functools
import math

import jax
import jax.numpy as jnp
import numpy as np
from jax import lax
from jax.experimental import pallas as pl
from jax.experimental.pallas import tpu as pltpu

F32 = jnp.float32
BF16 = jnp.bfloat16
I32 = jnp.int32

LANE = 128
ROW_TILE = 128
RW_HEAD = 64
AT_HEAD = 128
TOPK_MAX = 256
ROPE_THETA = 10000.0
POOL_WINDOWS = (2, 4, 8, 16)
POOL_HALO = 16
GN_EPS = 64e-5
LN_EPS = 1e-5
RW_CHUNK = 64
MOE_BM = 384
MOE_DC = 256
MOE_KC = 512
DSA_GROUPS = 8
Q_LOGIT_SCALE = AT_HEAD ** -0.5 * math.log2(math.e)
TOP_K = 2
INT_MIN = -(2 ** 31)
VMEM_LIMIT = 56 * 1024 * 1024

HIGHEST = lax.Precision.HIGHEST
NN = (((1,), (0,)), ((), ()))
NT = (((1,), (1,)), ((), ()))
TN = (((0,), (0,)), ((), ()))


def _dot(a, b, dims=NN, precision=None):
    return lax.dot_general(a, b, dims, precision=precision, preferred_element_type=F32)


def _sigmoid(x):
    return 1.0 / (1.0 + jnp.exp(-x))


def _cparams(sem):
    return pltpu.CompilerParams(dimension_semantics=sem, vmem_limit_bytes=VMEM_LIMIT)


def _round_up(x, m):
    return -(-x // m) * m


def _pick_tile(n, candidates):
    for c in candidates:
        if n % c == 0:
            return c
    return n


def _layout(sizes):
    pw = {k: _round_up(v, LANE) for k, v in sizes.items()}
    order = sorted(sizes, key=lambda k: -pw[k])
    off, gaps, cur = {}, [], 0
    for k in order:
        w = pw[k]
        placed = False
        for gi, (g0, g1) in enumerate(gaps):
            s = _round_up(g0, w)
            if s + w <= g1:
                off[k] = s
                new = [(g0, s), (s + w, g1)]
                gaps[gi:gi + 1] = [g for g in new if g[1] > g[0]]
                placed = True
                break
        if not placed:
            s = _round_up(cur, w)
            if s > cur:
                gaps.append((cur, s))
            off[k] = s
            cur = s + w
    return off, pw, cur


def _relayout_kernel(w_ref, o_ref, *, moves):
    cur = 0
    rows, total = o_ref.shape
    for s0, w, d0 in moves:
        if d0 > cur:
            o_ref[:, cur:d0] = jnp.zeros((rows, d0 - cur), o_ref.dtype)
        o_ref[:, d0:d0 + w] = w_ref[:, s0:s0 + w].astype(o_ref.dtype)
        cur = d0 + w
    if total > cur:
        o_ref[:, cur:total] = jnp.zeros((rows, total - cur), o_ref.dtype)


def _relayout_cols(w, src_sizes, off, total):
    depth, k, n_src = w.shape
    names = list(src_sizes)
    starts = np.cumsum([0] + [src_sizes[n] for n in names])
    moves = tuple(sorted(((int(starts[i]), src_sizes[n], off[n]) for i, n in enumerate(names)), key=lambda m: m[2]))
    tk = _pick_tile(k, (256, 128))
    return pl.pallas_call(
        functools.partial(_relayout_kernel, moves=moves),
        grid=(depth, k // tk),
        in_specs=[pl.BlockSpec((None, tk, n_src), lambda l, i: (l, i, 0))],
        out_specs=pl.BlockSpec((None, tk, total), lambda l, i: (l, i, 0)),
        out_shape=jax.ShapeDtypeStruct((depth, k, total), BF16),
        compiler_params=_cparams(("arbitrary", "arbitrary")),
    )(w)


def _mm_kernel(x_ref, w_ref, o_ref):
    o_ref[...] = _dot(x_ref[...], w_ref[...]).astype(o_ref.dtype)


def _matmul(x, w, layer, out_dtype):
    r, k = x.shape
    n = w.shape[2]
    tm = _pick_tile(r, (768, 512, 384, 256, 128))
    tn = _pick_tile(n, (1024, 512, 256, 128))
    return pl.pallas_call(
        _mm_kernel,
        grid=(n // tn, r // tm),
        in_specs=[pl.BlockSpec((tm, k), lambda j, i: (i, 0)),
                  pl.BlockSpec((None, k, tn), lambda j, i: (layer, 0, j))],
        out_specs=pl.BlockSpec((tm, tn), lambda j, i: (i, j)),
        out_shape=jax.ShapeDtypeStruct((r, n), out_dtype),
        compiler_params=_cparams(("arbitrary", "arbitrary")),
    )(x, w)


def _head_sums(x, hsum):
    tiles = [_dot(x[:, c * LANE:(c + 1) * LANE], hsum, precision=HIGHEST) for c in range(x.shape[1] // LANE)]
    return jnp.concatenate(tiles, axis=1)


def _split3(x):
    hi = x.astype(BF16)
    r1 = x - hi.astype(F32)
    mid = r1.astype(BF16)
    lo = (r1 - mid.astype(F32)).astype(BF16)
    return hi, mid, lo


def _rwkv_prep_kernel(r_ref, k_ref, v_ref, dw_ref, da_ref, dg_ref,
                      rh_ref, kh_ref, vh_ref, dwh_ref, dah_ref, dgh_ref,
                      mur_ref, muk_ref, muv_ref, mudw_ref, muda_ref, mudg_ref,
                      w2_ref, w0_ref, a2_ref, a0_ref, g2_ref, kkp_ref, kap_ref, rk_ref, hsum_ref,
                      tri_ref, blk_ref,
                      at_o, bt_o, kt_o, rt_o, bc_o, kc_o, v_o, epc_o, bv_o, g_o):
    first = pl.program_id(1) == 0

    def shift(x_ref, h_ref, mu_ref):
        x = x_ref[...]
        prev_row = jnp.where(first, 0.0, h_ref[7:8, :])
        row = lax.broadcasted_iota(I32, x.shape, 0)
        prev = jnp.where(row == 0, prev_row, pltpu.roll(x, 1, axis=0))
        return x + (prev - x) * mu_ref[...]

    r = shift(r_ref, rh_ref, mur_ref)
    k = shift(k_ref, kh_ref, muk_ref)
    v = shift(v_ref, vh_ref, muv_ref)
    dw = shift(dw_ref, dwh_ref, mudw_ref)
    da = shift(da_ref, dah_ref, muda_ref)
    dg = shift(dg_ref, dgh_ref, mudg_ref)

    wl = w0_ref[...] + _dot(jnp.tanh(dw).astype(BF16), w2_ref[...])
    neg = -wl
    softplus = jnp.maximum(neg, 0.0) + jnp.log(1.0 + jnp.exp(-jnp.abs(neg)))
    lw = -jnp.exp(-softplus - 0.5)
    a = _sigmoid(a0_ref[...] + _dot(da.astype(BF16), a2_ref[...]))
    g_o[...] = _dot(_sigmoid(dg).astype(BF16), g2_ref[...])
    hsum = hsum_ref[...]
    kk = k * kkp_ref[...]
    kk = kk / jnp.maximum(jnp.sqrt(_head_sums(kk * kk, hsum)), 1e-12)
    k = k * (1.0 + (a - 1.0) * kap_ref[...])
    bv_o[...] = _head_sums(r * k * rk_ref[...], hsum) * v

    parts = _split3(lw)
    tri, blk = tri_ref[...], blk_ref[...]
    cl = _dot(tri, parts[0]) + _dot(tri, parts[1]) + _dot(tri, parts[2])
    clc = _dot(blk, parts[0]) + _dot(blk, parts[1]) + _dot(blk, parts[2])
    e_n = jnp.exp(-cl)
    e_nc = jnp.exp(clc - cl)
    kka = kk * a
    at_o[...] = (-kk * jnp.exp(cl - lw)).astype(at_o.dtype)
    bt_o[...] = (kka * e_n).astype(bt_o.dtype)
    kt_o[...] = (k * e_n).astype(kt_o.dtype)
    rt_o[...] = (r * jnp.exp(cl)).astype(rt_o.dtype)
    bc_o[...] = (kka * e_nc).astype(bc_o.dtype)
    kc_o[...] = (k * e_nc).astype(kc_o.dtype)
    v_o[...] = v.astype(v_o.dtype)
    epc_o[...] = jnp.exp(clc)


def _rwkv_prep(proj, bsz, p_len, off, pw, mu, w2, w0, a2, a0, g2, kkp, kap, rk):
    rw = w2.shape[1]
    tm = ROW_TILE
    nb = p_len // tm
    names = ("r", "k", "v", "dw", "da", "dg")

    def cur_spec(n):
        w, c = pw[n], off[n] // pw[n]
        return pl.BlockSpec((tm, w), lambda b, i, c=c: (b * nb + i, c))

    def halo_spec(n):
        w, c = pw[n], off[n] // pw[n]
        return pl.BlockSpec((8, w), lambda b, i, c=c: (jnp.maximum((b * nb + i) * (tm // 8) - 1, 0), c))

    def full(a):
        return pl.BlockSpec(a.shape, lambda b, i: (0,) * a.ndim)

    lane = np.arange(LANE)
    hsum = jnp.asarray((lane[:, None] // RW_HEAD == lane[None, :] // RW_HEAD).astype(np.float32))
    t = np.arange(tm)
    same = t[:, None] // RW_CHUNK == t[None, :] // RW_CHUNK
    tri = jnp.asarray((same & (t[None, :] <= t[:, None])).astype(np.float32)).astype(BF16)
    blk = jnp.asarray(same.astype(np.float32)).astype(BF16)
    params = [mu[n] for n in names] + [w2, w0, a2, a0, g2, kkp, kap, rk, hsum, tri, blk]
    out_spec = pl.BlockSpec((tm, rw), lambda b, i: (b * nb + i, 0))
    sd = lambda dt: jax.ShapeDtypeStruct((bsz * p_len, rw), dt)
    return pl.pallas_call(
        _rwkv_prep_kernel,
        grid=(bsz, nb),
        in_specs=[cur_spec(n) for n in names] + [halo_spec(n) for n in names] + [full(a) for a in params],
        out_specs=[out_spec] * 10,
        out_shape=[sd(BF16)] * 7 + [sd(F32)] * 3,
        compiler_params=_cparams(("arbitrary", "arbitrary")),
    )(*([proj] * 12), *params)


def _rwkv_heads(at, bt, kt, rt, bc, kc, v, p_row, zt):
    n = len(at)
    hs = range(n)
    c = at[0].shape[0]
    b16 = lambda x: x.astype(BF16)
    row = lax.broadcasted_iota(I32, (2 * c, c), 0)
    col = lax.broadcasted_iota(I32, (2 * c, c), 1)
    keep = col < (row & (c - 1)) + jnp.where(row < c, 0, 1)
    lhs2 = [jnp.concatenate([at[h], rt[h]], axis=0) for h in hs]
    x_b = [jnp.where(keep, _dot(lhs2[h], bt[h], NT), 0.0) for h in hs]
    x_k = [jnp.where(keep, _dot(lhs2[h], kt[h], NT), 0.0) for h in hs]
    lkmv = [_dot(b16(x_k[h]), v[h]) for h in hs]
    l_ba = [x[:c] for x in x_b]
    m_b = [b16(x[c:]) for x in x_b]

    ti = lax.broadcasted_iota(I32, (c, c), 0)
    si = lax.broadcasted_iota(I32, (c, c), 1)
    zero = jnp.zeros((c, c), F32)
    eye = (ti == si).astype(F32)
    same_lo = (ti >> 4) == (si >> 4)
    lp = [jnp.where(same_lo, l, zero) for l in l_ba]
    t = [eye + l for l in lp]
    for _ in range(3):
        lp = [_dot(b16(l), b16(l)) for l in lp]
        t = [t[h] + _dot(b16(lp[h]), b16(t[h])) for h in hs]
    size = 32
    while size <= c:
        shift = size.bit_length() - 1
        same_hi = (ti >> shift) == (si >> shift)
        l_off = [b16(jnp.where(same_hi, jnp.where(same_lo, zero, l), zero)) for l in l_ba]
        tl = [_dot(b16(t[h]), l_off[h]) for h in hs]
        t = [t[h] + _dot(b16(tl[h]), b16(t[h])) for h in hs]
        same_lo = same_hi
        size *= 2

    tb = [b16(x) for x in t]
    a_z = [b16(_dot(tb[h], at[h])) for h in hs]
    w_u = [b16(_dot(tb[h], b16(lkmv[h][:c]))) for h in hs]
    ztb = [b16(z) for z in zt]
    q_z = [b16(rt[h].astype(F32) + _dot(m_b[h], a_z[h])) for h in hs]
    y = [_dot(q_z[h], ztb[h], NT) + _dot(m_b[h], w_u[h]) + lkmv[h][c:] for h in hs]
    g_t = [b16(_dot(a_z[h], bc[h], TN)) for h in hs]
    h_t = [_dot(w_u[h], bc[h], TN) + _dot(v[h], kc[h], TN) for h in hs]
    zt_new = [p_row[h] * zt[h] + _dot(ztb[h], g_t[h]) + h_t[h] for h in hs]
    return y, zt_new


def _rwkv_scan_kernel(at_ref, bt_ref, kt_ref, rt_ref, bc_ref, kc_ref, v_ref, epc_ref, bv_ref, g_ref,
                      gg_ref, gb_ref, mix_ref, o_ref, z_ref):
    del mix_ref
    @pl.when(pl.program_id(2) == 0)
    def _():
        z_ref[...] = jnp.zeros_like(z_ref)

    n_heads = o_ref.shape[1] // RW_HEAD
    sls = [slice(h * RW_HEAD, (h + 1) * RW_HEAD) for h in range(n_heads)]
    cut = lambda ref: [ref[:, sl] for sl in sls]
    y, z_new = _rwkv_heads(cut(at_ref), cut(bt_ref), cut(kt_ref), cut(rt_ref), cut(bc_ref), cut(kc_ref),
                           cut(v_ref), [epc_ref[0:1, sl] for sl in sls], [z_ref[h] for h in range(n_heads)])
    for h in range(n_heads):
        z_ref[h] = z_new[h]
    outs = []
    for h, sl in enumerate(sls):
        mu = jnp.mean(y[h], axis=-1, keepdims=True)
        var = jnp.mean(jnp.square(y[h] - mu), axis=-1, keepdims=True)
        yn = (y[h] - mu) * lax.rsqrt(var + GN_EPS) * gg_ref[:, sl] + gb_ref[:, sl]
        outs.append((yn + bv_ref[:, sl]) * g_ref[:, sl])
    per = LANE // RW_HEAD
    for p in range(n_heads // per):
        o_ref[:, p * LANE:(p + 1) * LANE] = jnp.concatenate(outs[p * per:(p + 1) * per], axis=1).astype(o_ref.dtype)


def _rwkv_scan(seqs, gn_g, gn_b, mix):
    bsz, p_len, _ = mix.shape
    rw = gn_g.shape[1]
    width = _pick_tile(rw, (12 * LANE, 4 * LANE, 3 * LANE, 2 * LANE, LANE))
    nch = p_len // RW_CHUNK
    seqs = [s.reshape(bsz, p_len, rw) for s in seqs]
    seq_spec = pl.BlockSpec((None, RW_CHUNK, width), lambda b, p, c: (b, c, p))
    par_spec = pl.BlockSpec((1, width), lambda b, p, c: (0, p))
    return pl.pallas_call(
        _rwkv_scan_kernel,
        grid=(bsz, rw // width, nch),
        in_specs=[seq_spec] * len(seqs) + [par_spec] * 2 + [pl.BlockSpec(memory_space=pl.ANY)],
        out_specs=seq_spec,
        out_shape=jax.ShapeDtypeStruct(mix.shape, mix.dtype),
        input_output_aliases={len(seqs) + 2: 0},
        scratch_shapes=[pltpu.VMEM((width // RW_HEAD, RW_HEAD, RW_HEAD), F32)],
        compiler_params=_cparams(("arbitrary", "arbitrary", "arbitrary")),
    )(*seqs, gn_g, gn_b, mix)


def _rope_tiles(x, cosf, sinf):
    outs = []
    for c in range(x.shape[1] // LANE):
        xt = x[:, c * LANE:(c + 1) * LANE]
        outs.append(xt * cosf + pltpu.roll(xt, LANE // 2, axis=1) * sinf)
    return outs


def _dsa_prep_kernel(cq_ref, ka_ref, va_ref, kidx_ref, widx_ref, cos_ref, sin_ref, qg_ref, wuq_ref, wiq_ref,
                     kg_ref, kb_ref, q_o, qi_o, k_o, v_o, ki_o, wq_o, *, wq_scale):
    cosf, sinf = cos_ref[...], sin_ref[...]
    v_o[...] = va_ref[...].astype(v_o.dtype)
    cq = cq_ref[...]
    cqn = cq * lax.rsqrt(jnp.mean(jnp.square(cq), axis=-1, keepdims=True) + 1e-6) * qg_ref[...]
    cqb = cqn.astype(BF16)
    for c, t in enumerate(_rope_tiles(_dot(cqb, wuq_ref[...]), cosf, sinf)):
        q_o[:, c * LANE:(c + 1) * LANE] = (t * Q_LOGIT_SCALE).astype(q_o.dtype)
    for c, t in enumerate(_rope_tiles(_dot(cqb, wiq_ref[...]), cosf, sinf)):
        qi_o[:, c * LANE:(c + 1) * LANE] = t.astype(qi_o.dtype)
    for c, t in enumerate(_rope_tiles(ka_ref[...], cosf, sinf)):
        k_o[:, c * LANE:(c + 1) * LANE] = t.astype(k_o.dtype)
    kx = kidx_ref[...]
    mu = jnp.mean(kx, axis=-1, keepdims=True)
    var = jnp.mean(jnp.square(kx - mu), axis=-1, keepdims=True)
    kn = (kx - mu) * lax.rsqrt(var + LN_EPS) * kg_ref[...] + kb_ref[...]
    ki_o[...] = _rope_tiles(kn, cosf, sinf)[0].astype(ki_o.dtype)
    wq_o[...] = widx_ref[...] * wq_scale


def _dsa_prep(proj, bsz, p_len, off, pw, cosf, sinf, qg, wuq, wiq, kg, kb, idx_heads):
    tm = _pick_tile(p_len, (256, 128))
    nb = p_len // tm
    rows = bsz * p_len
    aw, iw = wuq.shape[1], wiq.shape[1]

    def seg(n):
        w, c = pw[n], off[n] // pw[n]
        return pl.BlockSpec((tm, w), lambda i, c=c: (i, c))

    def full(a):
        return pl.BlockSpec(a.shape, lambda i: (0,) * a.ndim)

    pos_spec = pl.BlockSpec((tm, LANE), lambda i: (i % nb, 0))
    row_spec = lambda w: pl.BlockSpec((tm, w), lambda i: (i, 0))
    kern = functools.partial(_dsa_prep_kernel, wq_scale=float(idx_heads ** -0.5 * AT_HEAD ** -0.5))
    return pl.pallas_call(
        kern,
        grid=(rows // tm,),
        in_specs=[seg("cq"), seg("ka"), seg("va"), seg("kidx"), seg("widx"), pos_spec, pos_spec,
                  full(qg), full(wuq), full(wiq), full(kg), full(kb)],
        out_specs=[row_spec(aw), row_spec(iw), row_spec(aw), row_spec(aw), row_spec(LANE), row_spec(LANE)],
        out_shape=[jax.ShapeDtypeStruct((rows, aw), BF16), jax.ShapeDtypeStruct((rows, iw), BF16),
                   jax.ShapeDtypeStruct((rows, aw), BF16), jax.ShapeDtypeStruct((rows, aw), BF16),
                   jax.ShapeDtypeStruct((rows, LANE), BF16), jax.ShapeDtypeStruct((rows, LANE), F32)],
        compiler_params=_cparams(("arbitrary",)),
    )(proj, proj, proj, proj, proj, cosf, sinf, qg, wuq, wiq, kg, kb)


def _topk_mask(sc, causal, n_sel):
    tq, s_len = sc.shape
    bits = lax.bitcast_convert_type(jnp.where(sc == 0.0, 0.0, sc), I32)
    key = jnp.where(bits < 0, bits ^ jnp.int32(0x7FFFFFFF), bits)
    key = jnp.where(causal, key, jnp.int32(INT_MIN))
    one = jnp.ones((tq, s_len), I32)
    zero = jnp.zeros((tq, s_len), I32)

    i16 = jnp.int16
    one16 = jnp.ones((tq, s_len), i16)
    zero16 = jnp.zeros((tq, s_len), i16)

    def count_ge16(x16, th):
        hit = jnp.where(x16 >= th.astype(i16), one16, zero16)
        acc = hit[:, :LANE]
        for j in range(1, s_len // LANE):
            acc = acc + hit[:, j * LANE:(j + 1) * LANE]
        return jnp.sum(acc.astype(I32), axis=1, keepdims=True)

    def kth_largest16(x16):
        t = jnp.where(count_ge16(x16, jnp.zeros((tq, 1), I32)) >= n_sel, jnp.int32(0), jnp.int32(-32768))

        def step(i, t):
            cand = t | jnp.left_shift(jnp.int32(1), 14 - i)
            return jnp.where(count_ge16(x16, cand) >= n_sel, cand, t)

        return lax.fori_loop(0, 15, step, t)

    hi = (key >> 16).astype(i16)
    lo = ((key & jnp.int32(0xFFFF)) - 32768).astype(i16)
    tau_hi = kth_largest16(hi)
    th16 = tau_hi.astype(i16)
    lo_sel = jnp.where(hi == th16, lo, jnp.where(hi > th16, i16(32767), i16(-32768)))
    tau = tau_hi * 65536 + (kth_largest16(lo_sel) + 32768)
    gt = key > tau
    eq = key == tau
    n_gt = jnp.sum(jnp.where(gt, one, zero), axis=1, keepdims=True)
    n_eq = jnp.sum(jnp.where(eq, one, zero), axis=1, keepdims=True)
    need = n_sel - n_gt
    idx = lax.broadcasted_iota(I32, (tq, s_len), 1)
    nbits = max(1, (s_len - 1).bit_length())

    def cut_search():
        def cut_step(i, lo):
            cand = lo | jnp.left_shift(jnp.int32(1), nbits - 1 - i)
            cnt = jnp.sum(jnp.where(eq, jnp.where(idx < cand, one, zero), zero), axis=1, keepdims=True)
            return jnp.where(cnt < need, cand, lo)
        return lax.fori_loop(0, nbits, cut_step, jnp.zeros((tq, 1), I32))

    surplus = jnp.where(tau > jnp.int32(INT_MIN), jnp.where(n_eq > need, 1, 0), 0)
    cut = lax.cond(jnp.max(surplus) > 0, cut_search, lambda: jnp.full((tq, 1), s_len, I32))
    sel = jnp.where(gt, one, jnp.where(eq, jnp.where(idx <= cut, one, zero), zero))
    return jnp.where(causal, sel, zero) > 0


def _dsa_attn_kernel(q_ref, qi_ref, wq_ref, k_ref, v_ref, ki_ref, mix_ref, o_ref, *, n_sel, idx_heads, q_lo):
    del mix_ref
    tq = q_ref.shape[0]
    s_len = k_ref.shape[0]
    t0 = (pl.program_id(1) + q_lo) * tq
    ki = ki_ref[...]
    wq = wq_ref[...]
    sc = jnp.zeros((tq, s_len), F32)
    for h in range(idx_heads):
        s_h = _dot(qi_ref[:, h * LANE:(h + 1) * LANE], ki, NT)
        sc = sc + jnp.maximum(s_h, 0.0) * wq[:, h:h + 1]
    qpos = t0 + lax.broadcasted_iota(I32, (tq, s_len), 0)
    kpos = lax.broadcasted_iota(I32, (tq, s_len), 1)
    mask = _topk_mask(sc, kpos <= qpos, n_sel)
    bias = jnp.where(mask, 0.0, -jnp.inf)
    for h in range(q_ref.shape[1] // AT_HEAD):
        sl = slice(h * AT_HEAD, (h + 1) * AT_HEAD)
        lg = _dot(q_ref[:, sl], k_ref[:, sl], NT) + bias
        p = jnp.exp2(lg - jnp.max(lg, axis=1, keepdims=True))
        den = jnp.sum(p, axis=1, keepdims=True)
        o_ref[:, sl] = (_dot(p.astype(BF16), v_ref[:, sl]) / den).astype(o_ref.dtype)


def _dsa_attn(q, qi, wq, k, v, ki, mix, col, n_sel, idx_heads):
    bsz, p_len, _ = mix.shape
    aw, iw = q.shape[1], qi.shape[1]
    assert col % aw == 0
    tq = ROW_TILE
    nq = p_len // tq
    n_groups = min(DSA_GROUPS, nq)
    bounds = [round(nq * g / n_groups) for g in range(n_groups + 1)]
    r3 = lambda a: a.reshape(bsz, p_len, a.shape[1])
    args = (r3(q), r3(qi), r3(wq), r3(k), r3(v), r3(ki))
    for lo, hi in zip(bounds[:-1], bounds[1:]):
        s_len = hi * tq
        assert s_len >= n_sel
        qspec = lambda w, lo=lo: pl.BlockSpec((None, tq, w), lambda b, i: (b, i + lo, 0))
        kspec = lambda w, s_len=s_len: pl.BlockSpec((None, s_len, w), lambda b, i: (b, 0, 0),
                                                    pipeline_mode=pl.Buffered(1))
        kern = functools.partial(_dsa_attn_kernel, n_sel=n_sel, idx_heads=idx_heads, q_lo=lo)
        mix = pl.pallas_call(
            kern,
            grid=(bsz, hi - lo),
            in_specs=[qspec(aw), qspec(iw), qspec(LANE), kspec(aw), kspec(aw), kspec(LANE),
                      pl.BlockSpec(memory_space=pl.ANY)],
            out_specs=pl.BlockSpec((None, tq, aw), lambda b, i, lo=lo: (b, i + lo, col // aw)),
            out_shape=jax.ShapeDtypeStruct(mix.shape, mix.dtype),
            input_output_aliases={6: 0},
            compiler_params=_cparams(("arbitrary", "arbitrary")),
        )(*args, mix)
    return mix


def _pool_kernel(p_ref, h_ref, w_ref, s_ref, mix_ref, o_ref):
    del mix_ref
    tm, width = p_ref.shape
    grp = width // len(POOL_WINDOWS)
    t0 = pl.program_id(1) * tm
    x = p_ref[...]
    halo = jnp.where(pl.program_id(1) == 0, 0.0, h_ref[...])
    ext = jnp.concatenate([halo, x], axis=0)
    tpos = t0 + lax.broadcasted_iota(I32, (tm, 1), 0)
    acc = ext
    have = 1
    for gi, win in enumerate(POOL_WINDOWS):
        while have < win:
            shifted = jnp.concatenate([jnp.zeros((have, width), F32), acc[:-have]], axis=0)
            acc = acc + shifted
            have *= 2
        sl = slice(gi * grp, (gi + 1) * grp)
        cnt = jnp.minimum(tpos + 1, win).astype(F32)
        pooled = acc[POOL_HALO:, sl] / cnt - x[:, sl]
        y = _dot(pooled.astype(BF16), w_ref[gi])
        o_ref[:, sl] = (y * s_ref[:, sl]).astype(o_ref.dtype)


def _pool_mix(proj, off, pw, w_pool, scale, mix, col):
    bsz, p_len, d_mix = mix.shape
    width = pw["pin"]
    assert col % width == 0
    tm = _pick_tile(p_len, (256, 128))
    nb = p_len // tm
    c = off["pin"] // width
    hb = tm // POOL_HALO
    mix2 = mix.reshape(bsz * p_len, d_mix)
    out = pl.pallas_call(
        _pool_kernel,
        grid=(bsz, nb),
        in_specs=[pl.BlockSpec((tm, width), lambda b, i: (b * nb + i, c)),
                  pl.BlockSpec((POOL_HALO, width), lambda b, i: (jnp.maximum((b * nb + i) * hb - 1, 0), c)),
                  pl.BlockSpec(w_pool.shape, lambda b, i: (0, 0, 0)),
                  pl.BlockSpec(scale.shape, lambda b, i: (0, 0)),
                  pl.BlockSpec(memory_space=pl.ANY)],
        out_specs=pl.BlockSpec((tm, width), lambda b, i: (b * nb + i, col // width)),
        out_shape=jax.ShapeDtypeStruct(mix2.shape, mix2.dtype),
        input_output_aliases={4: 0},
        compiler_params=_cparams(("arbitrary", "arbitrary")),
    )(proj, proj, w_pool, scale, mix2)
    return out


def _layernorm_rows(x, g, b):
    mu = jnp.mean(x, axis=-1, keepdims=True)
    var = jnp.mean(jnp.square(x - mu), axis=-1, keepdims=True)
    return (x - mu) * lax.rsqrt(var + LN_EPS) * g + b


def _outproj_ln_kernel(x_ref, w_ref, h_ref, g_ref, b_ref, o_ref, ob_ref, pre_ref, *, alpha):
    j = pl.program_id(1)
    nj, _, tn = pre_ref.shape
    pre_ref[j] = alpha * h_ref[...] + _dot(x_ref[...], w_ref[...])

    @pl.when(j == nj - 1)
    def _():
        d = nj * tn
        total = pre_ref[0]
        for c in range(1, nj):
            total = total + pre_ref[c]
        mu = jnp.sum(total, axis=-1, keepdims=True) / d
        sq = jnp.square(pre_ref[0] - mu)
        for c in range(1, nj):
            sq = sq + jnp.square(pre_ref[c] - mu)
        inv = lax.rsqrt(jnp.sum(sq, axis=-1, keepdims=True) / d + LN_EPS)
        for c in range(nj):
            cs = slice(c * tn, (c + 1) * tn)
            y = (pre_ref[c] - mu) * inv * g_ref[:, cs] + b_ref[:, cs]
            o_ref[:, cs] = y
            ob_ref[:, cs] = y.astype(ob_ref.dtype)


def _outproj_ln(mix, w_out, h, g, b, alpha):
    r, k = mix.shape
    d = w_out.shape[1]
    tm = _pick_tile(r, (384, 256, 128))
    tn = _pick_tile(d, (512, 256, 128))
    kern = functools.partial(_outproj_ln_kernel, alpha=alpha)
    return pl.pallas_call(
        kern,
        grid=(r // tm, d // tn),
        in_specs=[pl.BlockSpec((tm, k), lambda i, j: (i, 0)),
                  pl.BlockSpec((k, tn), lambda i, j: (0, j)),
                  pl.BlockSpec((tm, tn), lambda i, j: (i, j)),
                  pl.BlockSpec((1, d), lambda i, j: (0, 0)),
                  pl.BlockSpec((1, d), lambda i, j: (0, 0))],
        out_specs=[pl.BlockSpec((tm, d), lambda i, j: (i, 0))] * 2,
        out_shape=[jax.ShapeDtypeStruct((r, d), F32), jax.ShapeDtypeStruct((r, d), BF16)],
        scratch_shapes=[pltpu.VMEM((d // tn, tm, tn), F32)],
        compiler_params=_cparams(("arbitrary", "arbitrary")),
    )(mix, w_out, h, g, b)


def _router_kernel(x_ref, w_ref, b_ref, e_o, g_o, *, n_groups, per_group):
    logits = _dot(x_ref[...], w_ref[...]) + b_ref[...]
    lane = lax.broadcasted_iota(I32, logits.shape, 1)
    big = jnp.int32(LANE)
    ninf = -jnp.inf
    gl = jnp.where(lane < n_groups, logits, ninf)
    ge = jnp.exp(gl - jnp.max(gl, axis=1, keepdims=True))
    gp = ge / jnp.sum(ge, axis=1, keepdims=True)
    g_val = jnp.max(gp, axis=1, keepdims=True)
    g_idx = jnp.min(jnp.where(gp == g_val, lane, big), axis=1, keepdims=True)
    lane_grp = jnp.where(lane >= n_groups, (lane - n_groups) // per_group, -1)
    in_grp = lane_grp == g_idx
    el = jnp.where(in_grp, logits, ninf)
    ee = jnp.exp(el - jnp.max(el, axis=1, keepdims=True))
    ep = jnp.where(in_grp, ee / jnp.sum(ee, axis=1, keepdims=True), -1.0)
    v1 = jnp.max(ep, axis=1, keepdims=True)
    i1 = jnp.min(jnp.where(ep == v1, lane, big), axis=1, keepdims=True)
    ep2 = jnp.where(lane == i1, -1.0, ep)
    v2 = jnp.max(ep2, axis=1, keepdims=True)
    i2 = jnp.min(jnp.where(ep2 == v2, lane, big), axis=1, keepdims=True)
    tot = v1 + v2
    e_o[...] = jnp.where(lane == 0, i1 - n_groups, jnp.where(lane == 1, i2 - n_groups, 0))
    g_o[...] = jnp.where(lane == 0, g_val * v1 / tot, jnp.where(lane == 1, g_val * v2 / tot, 0.0))


def _router(hb, w, b, n_groups, per_group):
    r, d = hb.shape
    tm = _pick_tile(r, (256, 128))
    kern = functools.partial(_router_kernel, n_groups=n_groups, per_group=per_group)
    return pl.pallas_call(
        kern,
        grid=(r // tm,),
        in_specs=[pl.BlockSpec((tm, d), lambda i: (i, 0)),
                  pl.BlockSpec((d, LANE), lambda i: (0, 0)),
                  pl.BlockSpec((1, LANE), lambda i: (0, 0))],
        out_specs=[pl.BlockSpec((tm, LANE), lambda i: (i, 0))] * 2,
        out_shape=[jax.ShapeDtypeStruct((r, LANE), I32), jax.ShapeDtypeStruct((r, LANE), F32)],
        compiler_params=_cparams(("arbitrary",)),
    )(hb, w, b)


def _gather_start(src_hbm, idx_ref, dst, sem):
    def issue(j, carry):
        pltpu.make_async_copy(src_hbm.at[pl.ds(idx_ref[0, 0, j], 1), :], dst.at[pl.ds(j, 1), :], sem).start()
        return carry
    lax.fori_loop(0, dst.shape[0], issue, 0)


def _gather_wait(src_hbm, dst, sem):
    pltpu.make_async_copy(src_hbm.at[pl.ds(0, dst.shape[0]), :], dst, sem).wait()


def _moe_ffn_kernel(be_ref, nb_ref, tok_ref, tok_next_ref, h_hbm, w1_ref, w3_ref, w2_ref, y_ref, xbuf, xb_ref,
                    sem):
    del be_ref
    i = pl.program_id(0)
    j = pl.program_id(1)
    n_used = nb_ref[0]
    slot = i % 2
    d = xbuf.shape[2]

    @pl.when(jnp.logical_and(i == 0, j == 0))
    def _():
        _gather_start(h_hbm, tok_ref, xbuf.at[0], sem.at[0])

    @pl.when(jnp.logical_and(j == 0, i + 1 < n_used))
    def _():
        _gather_start(h_hbm, tok_next_ref, xbuf.at[1 - slot], sem.at[1 - slot])

    @pl.when(jnp.logical_and(j == 0, i < n_used))
    def _():
        _gather_wait(h_hbm, xbuf.at[slot], sem.at[slot])
        for c in range(d // MOE_KC):
            ks = slice(c * MOE_KC, (c + 1) * MOE_KC)
            xb_ref[:, ks] = xbuf[slot, :, ks].astype(BF16)

    def ffn_chunk(first):
        dc = w1_ref.shape[1]
        h1 = jnp.zeros((xb_ref.shape[0], dc), F32)
        h3 = jnp.zeros((xb_ref.shape[0], dc), F32)
        for c in range(d // MOE_KC):
            ks = slice(c * MOE_KC, (c + 1) * MOE_KC)
            x = xb_ref[:, ks]
            h1 = h1 + _dot(x, w1_ref[ks, :].astype(BF16))
            h3 = h3 + _dot(x, w3_ref[ks, :].astype(BF16))
        act = (h1 * _sigmoid(h1) * h3).astype(BF16)
        for c in range(d // MOE_KC):
            ks = slice(c * MOE_KC, (c + 1) * MOE_KC)
            part = _dot(act, w2_ref[:, ks].astype(BF16))
            y_ref[:, ks] = part if first else y_ref[:, ks] + part

    @pl.when(jnp.logical_and(j == 0, i < n_used))
    def _():
        ffn_chunk(True)

    @pl.when(jnp.logical_and(j > 0, i < n_used))
    def _():
        ffn_chunk(False)

    @pl.when(jnp.logical_and(j == 0, i >= n_used))
    def _():
        y_ref[...] = jnp.zeros_like(y_ref)


def _moe_ffn(h, w1, w3, w2, layer, blk_e, n_used, row_tok):
    r, d = h.shape
    nblk = blk_e.shape[0]
    de = w1.shape[3]
    bm = row_tok.shape[2]
    dc = _pick_tile(de, (MOE_DC, LANE))
    nj = de // dc

    def chunk(i, j, nb):
        return jnp.where(i < nb[0], j, nj - 1)

    grid_spec = pltpu.PrefetchScalarGridSpec(
        num_scalar_prefetch=2,
        grid=(nblk, nj),
        in_specs=[pl.BlockSpec((1, 1, bm), lambda i, j, be, nb: (i, 0, 0), memory_space=pltpu.SMEM),
                  pl.BlockSpec((1, 1, bm), lambda i, j, be, nb: (jnp.minimum(i + 1, nblk - 1), 0, 0),
                               memory_space=pltpu.SMEM),
                  pl.BlockSpec(memory_space=pl.ANY),
                  pl.BlockSpec((None, None, d, dc), lambda i, j, be, nb: (layer, be[i], 0, chunk(i, j, nb))),
                  pl.BlockSpec((None, None, d, dc), lambda i, j, be, nb: (layer, be[i], 0, chunk(i, j, nb))),
                  pl.BlockSpec((None, None, dc, d), lambda i, j, be, nb: (layer, be[i], chunk(i, j, nb), 0))],
        out_specs=pl.BlockSpec((bm, d), lambda i, j, be, nb: (i, 0)),
        scratch_shapes=[pltpu.VMEM((2, bm, d), F32), pltpu.VMEM((bm, d), BF16), pltpu.SemaphoreType.DMA((2,))],
    )
    return pl.pallas_call(
        _moe_ffn_kernel,
        grid_spec=grid_spec,
        out_shape=jax.ShapeDtypeStruct((nblk * bm, d), F32),
        compiler_params=_cparams(("arbitrary", "arbitrary")),
    )(blk_e, n_used, row_tok, row_tok, h, w1, w3, w2)


def _combine_ln_kernel(d0_ref, d1_ref, d0n_ref, d1n_ref, y_hbm, h_ref, gate_ref, g_ref, b_ref, o_ref, ob_ref,
                       ybuf, sem, *, alpha):
    i = pl.program_id(0)
    slot = i % 2

    @pl.when(i == 0)
    def _():
        _gather_start(y_hbm, d0_ref, ybuf.at[0, 0], sem.at[0, 0])
        _gather_start(y_hbm, d1_ref, ybuf.at[0, 1], sem.at[0, 1])

    @pl.when(i + 1 < pl.num_programs(0))
    def _():
        _gather_start(y_hbm, d0n_ref, ybuf.at[1 - slot, 0], sem.at[1 - slot, 0])
        _gather_start(y_hbm, d1n_ref, ybuf.at[1 - slot, 1], sem.at[1 - slot, 1])

    _gather_wait(y_hbm, ybuf.at[slot, 0], sem.at[slot, 0])
    _gather_wait(y_hbm, ybuf.at[slot, 1], sem.at[slot, 1])
    gate = gate_ref[...]
    ff = ybuf[slot, 0] * gate[:, 0:1] + ybuf[slot, 1] * gate[:, 1:2]
    y = _layernorm_rows(alpha * h_ref[...] + ff, g_ref[...], b_ref[...])
    o_ref[...] = y
    ob_ref[...] = y.astype(ob_ref.dtype)


def _combine_ln(yb, h, gates, dest0, dest1, g, b, alpha):
    r, d = h.shape
    tm = dest0.shape[2]
    nblk = r // tm
    kern = functools.partial(_combine_ln_kernel, alpha=alpha)
    idx_spec = pl.BlockSpec((1, 1, tm), lambda i: (i, 0, 0), memory_space=pltpu.SMEM)
    nxt_spec = pl.BlockSpec((1, 1, tm), lambda i: (jnp.minimum(i + 1, nblk - 1), 0, 0), memory_space=pltpu.SMEM)
    return pl.pallas_call(
        kern,
        grid=(nblk,),
        in_specs=[idx_spec, idx_spec, nxt_spec, nxt_spec,
                  pl.BlockSpec(memory_space=pl.ANY),
                  pl.BlockSpec((tm, d), lambda i: (i, 0)),
                  pl.BlockSpec((tm, LANE), lambda i: (i, 0)),
                  pl.BlockSpec((1, d), lambda i: (0, 0)),
                  pl.BlockSpec((1, d), lambda i: (0, 0))],
        out_specs=[pl.BlockSpec((tm, d), lambda i: (i, 0))] * 2,
        out_shape=[jax.ShapeDtypeStruct((r, d), F32), jax.ShapeDtypeStruct((r, d), BF16)],
        scratch_shapes=[pltpu.VMEM((2, 2, tm, d), F32), pltpu.SemaphoreType.DMA((2, 2))],
        compiler_params=_cparams(("arbitrary",)),
    )(dest0, dest1, dest0, dest1, yb, h, gates, g, b)


def _dispatch_plan(eid, n_experts, bm):
    r = eid.shape[0]
    a_tot = r * TOP_K
    e_flat = eid[:, :TOP_K].reshape(a_tot)
    onehot = (e_flat[:, None] == jnp.arange(n_experts, dtype=I32)[None, :]).astype(I32)
    before = jnp.cumsum(onehot, axis=0) - onehot
    rank = jnp.sum(before * onehot, axis=1)
    counts = jnp.sum(onehot, axis=0)
    pcounts = (counts + bm - 1) // bm * bm
    pends = jnp.cumsum(pcounts)
    pstarts = pends - pcounts
    dest = (pstarts[e_flat] + rank).astype(I32)
    nblk = -(-(a_tot + n_experts * (bm - 1)) // bm)
    row_tok = jnp.zeros((nblk * bm,), I32).at[dest].set(jnp.arange(a_tot, dtype=I32) // TOP_K)
    n_used = (pends[-1] // bm).astype(I32)
    blk = jnp.arange(nblk, dtype=I32)
    blk_e = jnp.searchsorted(pends, jnp.minimum(blk, n_used - 1) * bm, side="right").astype(I32)
    blk_e = jnp.minimum(blk_e, n_experts - 1)
    dest2 = dest.reshape(r, TOP_K)
    return row_tok.reshape(nblk, 1, bm), blk_e, n_used.reshape(1), dest2[:, 0], dest2[:, 1]


def kernel(x, meta, w_in, rw_mu, rw_w2, rw_w0, rw_a2, rw_a0, rw_g2, rw_kk, rw_ka, rw_rk, rw_gn_g, rw_gn_b,
           at_qnorm_g, at_w_uq, at_w_iq, at_kidx_g, at_kidx_b, pool_w, pool_scale, w_out, ln1_g, ln1_b,
           router_g_w, router_g_b, router_e_w, router_e_b, exp_w1, exp_w3, exp_w2, ln2_g, ln2_b):
    bsz, seq_len, d = x.shape
    depth = w_in.shape[0]
    n_meta = meta.shape[0]
    t_len = seq_len + n_meta
    p_len = _round_up(t_len, ROW_TILE)
    rows = bsz * p_len
    n_sel = min(TOPK_MAX, seq_len // 4)
    alpha = float((2 * depth) ** 0.25)

    rw = rw_w2.shape[2]
    aw = at_w_uq.shape[2]
    idx_head = at_kidx_g.shape[1]
    idx_heads = at_w_iq.shape[2] // idx_head
    assert idx_head == LANE and aw % AT_HEAD == 0 and rw % LANE == 0
    n_groups = router_g_w.shape[2]
    n_experts = router_e_w.shape[2]
    per_group = n_experts // n_groups
    assert n_groups + n_experts <= LANE
    pool_width = pool_w.shape[1] * pool_w.shape[2]

    src = {"r": rw, "k": rw, "v": rw, "dw": rw_w2.shape[1], "da": rw_a2.shape[1], "dg": rw_g2.shape[1],
           "cq": at_w_uq.shape[1], "ka": aw, "va": aw, "kidx": idx_head, "widx": idx_heads, "pin": pool_width}
    off, pw, n_proj = _layout(src)
    n_proj = _round_up(n_proj, 512)
    rw_names = ("r", "k", "v", "dw", "da", "dg")
    rw_src = {n: src[n] for n in rw_names}
    mu_starts = np.cumsum([0] + [rw_src[n] for n in rw_names])

    pos = jnp.arange(p_len, dtype=F32)
    inv = ROPE_THETA ** (-jnp.arange(0, AT_HEAD, 2, dtype=F32) / AT_HEAD)
    ang = pos[:, None] * inv[None, :]
    cosf = jnp.concatenate([jnp.cos(ang), jnp.cos(ang)], axis=1)
    sinf = jnp.concatenate([-jnp.sin(ang), jnp.sin(ang)], axis=1)

    h = jnp.concatenate([jnp.broadcast_to(meta.astype(x.dtype)[None], (bsz, n_meta, d)), x,
                         jnp.zeros((bsz, p_len - t_len, d), x.dtype)], axis=1).reshape(rows, d)
    hb = h.astype(BF16)
    row1 = lambda a: a.reshape(1, -1)
    pad_rows = lambda a, n: jnp.concatenate([a, jnp.zeros((n - a.shape[0],) + a.shape[1:], a.dtype)], axis=0)
    tm_c = ROW_TILE

    w_in_p = _relayout_cols(w_in, src, off, n_proj)

    for l in range(depth):
        proj = _matmul(hb, w_in_p, l, F32)

        mu = {}
        for i, n in enumerate(rw_names):
            seg = rw_mu[l, int(mu_starts[i]):int(mu_starts[i + 1])]
            mu[n] = row1(jnp.concatenate([seg, jnp.zeros((pw[n] - src[n],), F32)]))
        g2 = pad_rows(rw_g2[l], pw["dg"]).astype(BF16)
        seqs = _rwkv_prep(proj, bsz, p_len, off, pw, mu, rw_w2[l].astype(BF16), row1(rw_w0[l]),
                          rw_a2[l].astype(BF16), row1(rw_a0[l]), g2, row1(rw_kk[l]), row1(rw_ka[l]),
                          row1(rw_rk[l]))
        mix = jnp.zeros((bsz, p_len, rw + aw + pool_width), BF16)
        mix = _rwkv_scan(seqs, row1(rw_gn_g[l]), row1(rw_gn_b[l]), mix)

        q, qi, k_at, v_at, ki, wq = _dsa_prep(proj, bsz, p_len, off, pw, cosf, sinf, row1(at_qnorm_g[l]),
                                              at_w_uq[l].astype(BF16), at_w_iq[l].astype(BF16),
                                              row1(at_kidx_g[l]), row1(at_kidx_b[l]), idx_heads)
        mix = _dsa_attn(q, qi, wq, k_at, v_at, ki, mix, rw, n_sel, idx_heads)
        mix = _pool_mix(proj, off, pw, pool_w[l].astype(BF16), row1(pool_scale[l]), mix, rw + aw)
        h, hb = _outproj_ln(mix, w_out[l].astype(BF16), h, row1(ln1_g[l]), row1(ln1_b[l]), alpha)

        n_r = n_groups + n_experts
        w_r = jnp.concatenate([router_g_w[l], router_e_w[l], jnp.zeros((d, LANE - n_r), F32)], axis=1).astype(BF16)
        b_r = row1(jnp.concatenate([router_g_b[l], router_e_b[l], jnp.zeros((LANE - n_r,), F32)]))
        eid, gates = _router(hb, w_r, b_r, n_groups, per_group)
        row_tok, blk_e, n_used, dest0, dest1 = _dispatch_plan(eid, n_experts, MOE_BM)
        yb = _moe_ffn(h, exp_w1, exp_w3, exp_w2, l, blk_e, n_used, row_tok)
        h, hb = _combine_ln(yb, h, gates, dest0.reshape(rows // tm_c, 1, tm_c), dest1.reshape(rows // tm_c, 1, tm_c),
                            row1(ln2_g[l]), row1(ln2_b[l]), alpha)

    return h.reshape(bsz, p_len, d)[:, n_meta:t_len]
```

```python
import functools
import math

import jax
import jax.numpy as jnp
import numpy as np
from jax import lax
from jax.experimental import pallas as pl
from jax.experimental.pallas import tpu as pltpu

F32 = jnp.float32
BF16 = jnp.bfloat16
I32 = jnp.int32

LANE = 128
ROW_TILE = 128
RW_HEAD = 64
AT_HEAD = 128
TOPK_MAX = 256
ROPE_THETA = 10000.0
POOL_WINDOWS = (2, 4, 8, 16)
POOL_HALO = 16
GN_EPS = 64e-5
LN_EPS = 1e-5
RW_CHUNK = 64
MOE_BM = 384
MOE_KC = 512
DSA_TQ = 384
DSA_BIG_S = 3072
DSA_GROUPS = 8
DSA_TAIL_GROUPS = 3
Q_LOGIT_SCALE = AT_HEAD ** -0.5 * math.log2(math.e)
TOP_K = 2
INT_MIN = -(2 ** 31)
VMEM_LIMIT = 56 * 1024 * 1024
VMEM_LIMIT_WIDE = 60 * 1024 * 1024

HIGHEST = lax.Precision.HIGHEST
NN = (((1,), (0,)), ((), ()))
NT = (((1,), (1,)), ((), ()))
TN = (((0,), (0,)), ((), ()))


def _dot(a, b, dims=NN, precision=None):
    return lax.dot_general(a, b, dims, precision=precision, preferred_element_type=F32)


def _sigmoid(x):
    return 1.0 / (1.0 + jnp.exp(-x))


def _cparams(sem, vmem_limit=VMEM_LIMIT):
    return pltpu.CompilerParams(dimension_semantics=sem, vmem_limit_bytes=vmem_limit)


def _round_up(x, m):
    return -(-x // m) * m


def _pick_tile(n, candidates):
    for c in candidates:
        if n % c == 0:
            return c
    return n


def _layout(sizes):
    pw = {k: _round_up(v, LANE) for k, v in sizes.items()}
    order = sorted(sizes, key=lambda k: -pw[k])
    off, gaps, cur = {}, [], 0
    for k in order:
        w = pw[k]
        placed = False
        for gi, (g0, g1) in enumerate(gaps):
            s = _round_up(g0, w)
            if s + w <= g1:
                off[k] = s
                new = [(g0, s), (s + w, g1)]
                gaps[gi:gi + 1] = [g for g in new if g[1] > g[0]]
                placed = True
                break
        if not placed:
            s = _round_up(cur, w)
            if s > cur:
                gaps.append((cur, s))
            off[k] = s
            cur = s + w
    return off, pw, cur


def _relayout_kernel(w_ref, o_ref, *, moves):
    cur = 0
    rows, total = o_ref.shape
    for s0, w, d0 in moves:
        if d0 > cur:
            o_ref[:, cur:d0] = jnp.zeros((rows, d0 - cur), o_ref.dtype)
        o_ref[:, d0:d0 + w] = w_ref[:, s0:s0 + w].astype(o_ref.dtype)
        cur = d0 + w
    if total > cur:
        o_ref[:, cur:total] = jnp.zeros((rows, total - cur), o_ref.dtype)


def _relayout_cols(w, src_sizes, off, total):
    depth, k, n_src = w.shape
    names = list(src_sizes)
    starts = np.cumsum([0] + [src_sizes[n] for n in names])
    moves = tuple(sorted(((int(starts[i]), src_sizes[n], off[n]) for i, n in enumerate(names)), key=lambda m: m[2]))
    tk = _pick_tile(k, (256, 128))
    return pl.pallas_call(
        functools.partial(_relayout_kernel, moves=moves),
        grid=(depth, k // tk),
        in_specs=[pl.BlockSpec((None, tk, n_src), lambda l, i: (l, i, 0))],
        out_specs=pl.BlockSpec((None, tk, total), lambda l, i: (l, i, 0)),
        out_shape=jax.ShapeDtypeStruct((depth, k, total), BF16),
        compiler_params=_cparams(("arbitrary", "arbitrary")),
    )(w)


def _mm_kernel(x_ref, w_ref, o_ref):
    o_ref[...] = _dot(x_ref[...], w_ref[...]).astype(o_ref.dtype)


def _matmul(x, w, layer, out_dtype):
    r, k = x.shape
    n = w.shape[2]
    tm = _pick_tile(r, (768, 512, 384, 256, 128))
    tn = _pick_tile(n, (1024, 512, 256, 128))
    return pl.pallas_call(
        _mm_kernel,
        grid=(n // tn, r // tm),
        in_specs=[pl.BlockSpec((tm, k), lambda j, i: (i, 0)),
                  pl.BlockSpec((None, k, tn), lambda j, i: (layer, 0, j))],
        out_specs=pl.BlockSpec((tm, tn), lambda j, i: (i, j)),
        out_shape=jax.ShapeDtypeStruct((r, n), out_dtype),
        compiler_params=_cparams(("arbitrary", "arbitrary")),
    )(x, w)


def _head_sums(x, hsum):
    tiles = [_dot(x[:, c * LANE:(c + 1) * LANE], hsum, precision=HIGHEST) for c in range(x.shape[1] // LANE)]
    return jnp.concatenate(tiles, axis=1)


def _split3(x):
    hi = x.astype(BF16)
    r1 = x - hi.astype(F32)
    mid = r1.astype(BF16)
    lo = (r1 - mid.astype(F32)).astype(BF16)
    return hi, mid, lo


def _rwkv_prep_kernel(r_ref, k_ref, v_ref, dw_ref, da_ref, dg_ref,
                      rh_ref, kh_ref, vh_ref, dwh_ref, dah_ref, dgh_ref,
                      mur_ref, muk_ref, muv_ref, mudw_ref, muda_ref, mudg_ref,
                      w2_ref, w0_ref, a2_ref, a0_ref, g2_ref, kkp_ref, kap_ref, rk_ref, hsum_ref,
                      tri_ref, blk_ref,
                      at_o, bt_o, kt_o, rt_o, bc_o, kc_o, v_o, epc_o, bv_o, g_o):
    first = pl.program_id(1) == 0

    def shift(x_ref, h_ref, mu_ref):
        x = x_ref[...]
        prev_row = jnp.where(first, 0.0, h_ref[7:8, :])
        row = lax.broadcasted_iota(I32, x.shape, 0)
        prev = jnp.where(row == 0, prev_row, pltpu.roll(x, 1, axis=0))
        return x + (prev - x) * mu_ref[...]

    r = shift(r_ref, rh_ref, mur_ref)
    k = shift(k_ref, kh_ref, muk_ref)
    v = shift(v_ref, vh_ref, muv_ref)
    dw = shift(dw_ref, dwh_ref, mudw_ref)
    da = shift(da_ref, dah_ref, muda_ref)
    dg = shift(dg_ref, dgh_ref, mudg_ref)

    wl = w0_ref[...] + _dot(jnp.tanh(dw).astype(BF16), w2_ref[...])
    neg = -wl
    softplus = jnp.maximum(neg, 0.0) + jnp.log(1.0 + jnp.exp(-jnp.abs(neg)))
    lw = -jnp.exp(-softplus - 0.5)
    a = _sigmoid(a0_ref[...] + _dot(da.astype(BF16), a2_ref[...]))
    g_o[...] = _dot(_sigmoid(dg).astype(BF16), g2_ref[...])
    hsum = hsum_ref[...]
    kk = k * kkp_ref[...]
    kk = kk / jnp.maximum(jnp.sqrt(_head_sums(kk * kk, hsum)), 1e-12)
    k = k * (1.0 + (a - 1.0) * kap_ref[...])
    bv_o[...] = _head_sums(r * k * rk_ref[...], hsum) * v

    parts = _split3(lw)
    tri, blk = tri_ref[...], blk_ref[...]
    cl = _dot(tri, parts[0]) + _dot(tri, parts[1]) + _dot(tri, parts[2])
    clc = _dot(blk, parts[0]) + _dot(blk, parts[1]) + _dot(blk, parts[2])
    e_n = jnp.exp(-cl)
    e_nc = jnp.exp(clc - cl)
    kka = kk * a
    at_o[...] = (-kk * jnp.exp(cl - lw)).astype(at_o.dtype)
    bt_o[...] = (kka * e_n).astype(bt_o.dtype)
    kt_o[...] = (k * e_n).astype(kt_o.dtype)
    rt_o[...] = (r * jnp.exp(cl)).astype(rt_o.dtype)
    bc_o[...] = (kka * e_nc).astype(bc_o.dtype)
    kc_o[...] = (k * e_nc).astype(kc_o.dtype)
    v_o[...] = v.astype(v_o.dtype)
    epc_o[...] = jnp.exp(clc)


def _rwkv_prep(proj, bsz, p_len, off, pw, mu, w2, w0, a2, a0, g2, kkp, kap, rk):
    rw = w2.shape[1]
    tm = ROW_TILE
    nb = p_len // tm
    names = ("r", "k", "v", "dw", "da", "dg")

    def cur_spec(n):
        w, c = pw[n], off[n] // pw[n]
        return pl.BlockSpec((tm, w), lambda b, i, c=c: (b * nb + i, c))

    def halo_spec(n):
        w, c = pw[n], off[n] // pw[n]
        return pl.BlockSpec((8, w), lambda b, i, c=c: (jnp.maximum((b * nb + i) * (tm // 8) - 1, 0), c))

    def full(a):
        return pl.BlockSpec(a.shape, lambda b, i: (0,) * a.ndim)

    lane = np.arange(LANE)
    hsum = jnp.asarray((lane[:, None] // RW_HEAD == lane[None, :] // RW_HEAD).astype(np.float32))
    t = np.arange(tm)
    same = t[:, None] // RW_CHUNK == t[None, :] // RW_CHUNK
    tri = jnp.asarray((same & (t[None, :] <= t[:, None])).astype(np.float32)).astype(BF16)
    blk = jnp.asarray(same.astype(np.float32)).astype(BF16)
    params = [mu[n] for n in names] + [w2, w0, a2, a0, g2, kkp, kap, rk, hsum, tri, blk]
    out_spec = pl.BlockSpec((tm, rw), lambda b, i: (b * nb + i, 0))
    sd = lambda dt: jax.ShapeDtypeStruct((bsz * p_len, rw), dt)
    return pl.pallas_call(
        _rwkv_prep_kernel,
        grid=(bsz, nb),
        in_specs=[cur_spec(n) for n in names] + [halo_spec(n) for n in names] + [full(a) for a in params],
        out_specs=[out_spec] * 10,
        out_shape=[sd(BF16)] * 7 + [sd(F32)] * 3,
        compiler_params=_cparams(("arbitrary", "arbitrary")),
    )(*([proj] * 12), *params)


def _rwkv_heads(at, bt, kt, rt, bc, kc, v, p_row, zt):
    n = len(at)
    hs = range(n)
    c = at[0].shape[0]
    b16 = lambda x: x.astype(BF16)
    row = lax.broadcasted_iota(I32, (2 * c, c), 0)
    col = lax.broadcasted_iota(I32, (2 * c, c), 1)
    keep = col < (row & (c - 1)) + jnp.where(row < c, 0, 1)
    lhs2 = [jnp.concatenate([at[h], rt[h]], axis=0) for h in hs]
    x_b = [jnp.where(keep, _dot(lhs2[h], bt[h], NT), 0.0) for h in hs]
    x_k = [jnp.where(keep, _dot(lhs2[h], kt[h], NT), 0.0) for h in hs]
    lkmv = [_dot(b16(x_k[h]), v[h]) for h in hs]
    l_ba = [x[:c] for x in x_b]
    m_b = [b16(x[c:]) for x in x_b]

    ti = lax.broadcasted_iota(I32, (c, c), 0)
    si = lax.broadcasted_iota(I32, (c, c), 1)
    zero = jnp.zeros((c, c), F32)
    eye = (ti == si).astype(F32)
    same_lo = (ti >> 4) == (si >> 4)
    lp = [jnp.where(same_lo, l, zero) for l in l_ba]
    t = [eye + l for l in lp]
    for _ in range(3):
        lp = [_dot(b16(l), b16(l)) for l in lp]
        t = [t[h] + _dot(b16(lp[h]), b16(t[h])) for h in hs]
    size = 32
    while size <= c:
        shift = size.bit_length() - 1
        same_hi = (ti >> shift) == (si >> shift)
        l_off = [b16(jnp.where(same_hi, jnp.where(same_lo, zero, l), zero)) for l in l_ba]
        tl = [_dot(b16(t[h]), l_off[h]) for h in hs]
        t = [t[h] + _dot(b16(tl[h]), b16(t[h])) for h in hs]
        same_lo = same_hi
        size *= 2

    tb = [b16(x) for x in t]
    a_z = [b16(_dot(tb[h], at[h])) for h in hs]
    w_u = [b16(_dot(tb[h], b16(lkmv[h][:c]))) for h in hs]
    ztb = [b16(z) for z in zt]
    q_z = [b16(rt[h].astype(F32) + _dot(m_b[h], a_z[h])) for h in hs]
    y = [_dot(q_z[h], ztb[h], NT) + _dot(m_b[h], w_u[h]) + lkmv[h][c:] for h in hs]
    g_t = [b16(_dot(a_z[h], bc[h], TN)) for h in hs]
    h_t = [_dot(w_u[h], bc[h], TN) + _dot(v[h], kc[h], TN) for h in hs]
    zt_new = [p_row[h] * zt[h] + _dot(ztb[h], g_t[h]) + h_t[h] for h in hs]
    return y, zt_new


def _rwkv_scan_kernel(at_ref, bt_ref, kt_ref, rt_ref, bc_ref, kc_ref, v_ref, epc_ref, bv_ref, g_ref,
                      gg_ref, gb_ref, mix_ref, o_ref, z_ref):
    del mix_ref
    @pl.when(pl.program_id(2) == 0)
    def _():
        z_ref[...] = jnp.zeros_like(z_ref)

    n_heads = o_ref.shape[1] // RW_HEAD
    sls = [slice(h * RW_HEAD, (h + 1) * RW_HEAD) for h in range(n_heads)]
    cut = lambda ref: [ref[:, sl] for sl in sls]
    y, z_new = _rwkv_heads(cut(at_ref), cut(bt_ref), cut(kt_ref), cut(rt_ref), cut(bc_ref), cut(kc_ref),
                           cut(v_ref), [epc_ref[0:1, sl] for sl in sls], [z_ref[h] for h in range(n_heads)])
    for h in range(n_heads):
        z_ref[h] = z_new[h]
    outs = []
    for h, sl in enumerate(sls):
        mu = jnp.mean(y[h], axis=-1, keepdims=True)
        var = jnp.mean(jnp.square(y[h] - mu), axis=-1, keepdims=True)
        yn = (y[h] - mu) * lax.rsqrt(var + GN_EPS) * gg_ref[:, sl] + gb_ref[:, sl]
        outs.append((yn + bv_ref[:, sl]) * g_ref[:, sl])
    per = LANE // RW_HEAD
    for p in range(n_heads // per):
        o_ref[:, p * LANE:(p + 1) * LANE] = jnp.concatenate(outs[p * per:(p + 1) * per], axis=1).astype(o_ref.dtype)


def _rwkv_scan(seqs, gn_g, gn_b, mix):
    bsz, p_len, _ = mix.shape
    rw = gn_g.shape[1]
    width = _pick_tile(rw, (12 * LANE, 4 * LANE, 3 * LANE, 2 * LANE, LANE))
    nch = p_len // RW_CHUNK
    seqs = [s.reshape(bsz, p_len, rw) for s in seqs]
    seq_spec = pl.BlockSpec((None, RW_CHUNK, width), lambda b, p, c: (b, c, p))
    par_spec = pl.BlockSpec((1, width), lambda b, p, c: (0, p))
    return pl.pallas_call(
        _rwkv_scan_kernel,
        grid=(bsz, rw // width, nch),
        in_specs=[seq_spec] * len(seqs) + [par_spec] * 2 + [pl.BlockSpec(memory_space=pl.ANY)],
        out_specs=seq_spec,
        out_shape=jax.ShapeDtypeStruct(mix.shape, mix.dtype),
        input_output_aliases={len(seqs) + 2: 0},
        scratch_shapes=[pltpu.VMEM((width // RW_HEAD, RW_HEAD, RW_HEAD), F32)],
        compiler_params=_cparams(("arbitrary", "arbitrary", "arbitrary")),
    )(*seqs, gn_g, gn_b, mix)


def _rope_tiles(x, cosf, sinf):
    outs = []
    for c in range(x.shape[1] // LANE):
        xt = x[:, c * LANE:(c + 1) * LANE]
        outs.append(xt * cosf + pltpu.roll(xt, LANE // 2, axis=1) * sinf)
    return outs


def _dsa_prep_kernel(cq_ref, ka_ref, va_ref, kidx_ref, widx_ref, cos_ref, sin_ref, qg_ref, wuq_ref, wiq_ref,
                     kg_ref, kb_ref, q_o, qi_o, k_o, v_o, ki_o, wq_o, *, wq_scale):
    cosf, sinf = cos_ref[...], sin_ref[...]
    v_o[...] = va_ref[...].astype(v_o.dtype)
    cq = cq_ref[...]
    cqn = cq * lax.rsqrt(jnp.mean(jnp.square(cq), axis=-1, keepdims=True) + 1e-6) * qg_ref[...]
    cqb = cqn.astype(BF16)
    for c, t in enumerate(_rope_tiles(_dot(cqb, wuq_ref[...]), cosf, sinf)):
        q_o[:, c * LANE:(c + 1) * LANE] = (t * Q_LOGIT_SCALE).astype(q_o.dtype)
    for c, t in enumerate(_rope_tiles(_dot(cqb, wiq_ref[...]), cosf, sinf)):
        qi_o[:, c * LANE:(c + 1) * LANE] = t.astype(qi_o.dtype)
    for c, t in enumerate(_rope_tiles(ka_ref[...], cosf, sinf)):
        k_o[:, c * LANE:(c + 1) * LANE] = t.astype(k_o.dtype)
    kx = kidx_ref[...]
    mu = jnp.mean(kx, axis=-1, keepdims=True)
    var = jnp.mean(jnp.square(kx - mu), axis=-1, keepdims=True)
    kn = (kx - mu) * lax.rsqrt(var + LN_EPS) * kg_ref[...] + kb_ref[...]
    ki_o[...] = _rope_tiles(kn, cosf, sinf)[0].astype(ki_o.dtype)
    wq_o[...] = widx_ref[...] * wq_scale


def _dsa_prep(proj, bsz, p_len, off, pw, cosf, sinf, qg, wuq, wiq, kg, kb, idx_heads):
    tm = _pick_tile(p_len, (256, 128))
    nb = p_len // tm
    rows = bsz * p_len
    aw, iw = wuq.shape[1], wiq.shape[1]

    def seg(n):
        w, c = pw[n], off[n] // pw[n]
        return pl.BlockSpec((tm, w), lambda i, c=c: (i, c))

    def full(a):
        return pl.BlockSpec(a.shape, lambda i: (0,) * a.ndim)

    pos_spec = pl.BlockSpec((tm, LANE), lambda i: (i % nb, 0))
    row_spec = lambda w: pl.BlockSpec((tm, w), lambda i: (i, 0))
    kern = functools.partial(_dsa_prep_kernel, wq_scale=float(idx_heads ** -0.5 * AT_HEAD ** -0.5))
    return pl.pallas_call(
        kern,
        grid=(rows // tm,),
        in_specs=[seg("cq"), seg("ka"), seg("va"), seg("kidx"), seg("widx"), pos_spec, pos_spec,
                  full(qg), full(wuq), full(wiq), full(kg), full(kb)],
        out_specs=[row_spec(aw), row_spec(iw), row_spec(aw), row_spec(aw), row_spec(LANE), row_spec(LANE)],
        out_shape=[jax.ShapeDtypeStruct((rows, aw), BF16), jax.ShapeDtypeStruct((rows, iw), BF16),
                   jax.ShapeDtypeStruct((rows, aw), BF16), jax.ShapeDtypeStruct((rows, aw), BF16),
                   jax.ShapeDtypeStruct((rows, LANE), BF16), jax.ShapeDtypeStruct((rows, LANE), F32)],
        compiler_params=_cparams(("arbitrary",)),
    )(proj, proj, proj, proj, proj, cosf, sinf, qg, wuq, wiq, kg, kb)


def _topk_mask(sc, causal, n_sel):
    tq, s_len = sc.shape
    bits = lax.bitcast_convert_type(jnp.where(sc == 0.0, 0.0, sc), I32)
    key = jnp.where(bits < 0, bits ^ jnp.int32(0x7FFFFFFF), bits)
    key = jnp.where(causal, key, jnp.int32(INT_MIN))
    one = jnp.ones((tq, s_len), I32)
    zero = jnp.zeros((tq, s_len), I32)

    i16 = jnp.int16
    one16 = jnp.ones((tq, s_len), i16)
    zero16 = jnp.zeros((tq, s_len), i16)

    def count_ge16(x16, th):
        hit = jnp.where(x16 >= th.astype(i16), one16, zero16)
        acc = hit[:, :LANE]
        for j in range(1, s_len // LANE):
            acc = acc + hit[:, j * LANE:(j + 1) * LANE]
        return jnp.sum(acc.astype(I32), axis=1, keepdims=True)

    def kth_largest16(x16):
        t = jnp.where(count_ge16(x16, jnp.zeros((tq, 1), I32)) >= n_sel, jnp.int32(0), jnp.int32(-32768))

        def step(i, t):
            cand = t | jnp.left_shift(jnp.int32(1), 14 - i)
            return jnp.where(count_ge16(x16, cand) >= n_sel, cand, t)

        return lax.fori_loop(0, 15, step, t)

    hi = (key >> 16).astype(i16)
    lo = ((key & jnp.int32(0xFFFF)) - 32768).astype(i16)
    tau_hi = kth_largest16(hi)
    th16 = tau_hi.astype(i16)
    lo_sel = jnp.where(hi == th16, lo, jnp.where(hi > th16, i16(32767), i16(-32768)))
    tau = tau_hi * 65536 + (kth_largest16(lo_sel) + 32768)
    gt = key > tau
    eq = key == tau
    n_gt = jnp.sum(jnp.where(gt, one, zero), axis=1, keepdims=True)
    n_eq = jnp.sum(jnp.where(eq, one, zero), axis=1, keepdims=True)
    need = n_sel - n_gt
    idx = lax.broadcasted_iota(I32, (tq, s_len), 1)
    nbits = max(1, (s_len - 1).bit_length())

    def cut_search():
        def cut_step(i, lo):
            cand = lo | jnp.left_shift(jnp.int32(1), nbits - 1 - i)
            cnt = jnp.sum(jnp.where(eq, jnp.where(idx < cand, one, zero), zero), axis=1, keepdims=True)
            return jnp.where(cnt < need, cand, lo)
        return lax.fori_loop(0, nbits, cut_step, jnp.zeros((tq, 1), I32))

    surplus = jnp.where(tau > jnp.int32(INT_MIN), jnp.where(n_eq > need, 1, 0), 0)
    cut = lax.cond(jnp.max(surplus) > 0, cut_search, lambda: jnp.full((tq, 1), s_len, I32))
    sel = jnp.where(gt, one, jnp.where(eq, jnp.where(idx <= cut, one, zero), zero))
    return jnp.where(causal, sel, zero) > 0


def _dsa_attn_kernel(q_ref, qi_ref, wq_ref, k_ref, v_ref, ki_ref, mix_ref, o_ref, *, n_sel, idx_heads, q_lo):
    del mix_ref
    tq = q_ref.shape[0]
    s_len = k_ref.shape[0]
    t0 = (pl.program_id(1) + q_lo) * tq
    ki = ki_ref[...]
    wq = wq_ref[...]
    sc = jnp.zeros((tq, s_len), F32)
    for h in range(idx_heads):
        s_h = _dot(qi_ref[:, h * LANE:(h + 1) * LANE], ki, NT)
        sc = sc + jnp.maximum(s_h, 0.0) * wq[:, h:h + 1]
    qpos = t0 + lax.broadcasted_iota(I32, (tq, s_len), 0)
    kpos = lax.broadcasted_iota(I32, (tq, s_len), 1)
    mask = _topk_mask(sc, kpos <= qpos, n_sel)
    bias = jnp.where(mask, 0.0, -jnp.inf)
    for h in range(q_ref.shape[1] // AT_HEAD):
        sl = slice(h * AT_HEAD, (h + 1) * AT_HEAD)
        lg = _dot(q_ref[:, sl], k_ref[:, sl], NT) + bias
        p = jnp.exp2(lg - jnp.max(lg, axis=1, keepdims=True))
        den = jnp.sum(p, axis=1, keepdims=True)
        o_ref[:, sl] = (_dot(p.astype(BF16), v_ref[:, sl]) / den).astype(o_ref.dtype)


def _dsa_attn(q, qi, wq, k, v, ki, mix, col, n_sel, idx_heads):
    bsz, p_len, _ = mix.shape
    aw, iw = q.shape[1], qi.shape[1]
    assert col % aw == 0
    big_rows = min(p_len, DSA_BIG_S) // DSA_TQ * DSA_TQ
    calls = []
    for tq, row0, row1, n_groups in ((DSA_TQ, 0, big_rows, DSA_GROUPS),
                                     (ROW_TILE, big_rows, p_len, DSA_TAIL_GROUPS)):
        nq = (row1 - row0) // tq
        n_groups = min(n_groups, nq)
        bounds = [row0 // tq + round(nq * g / n_groups) for g in range(n_groups + 1)] if nq else []
        calls += [(tq, lo, hi) for lo, hi in zip(bounds[:-1], bounds[1:])]
    r3 = lambda a: a.reshape(bsz, p_len, a.shape[1])
    args = (r3(q), r3(qi), r3(wq), r3(k), r3(v), r3(ki))
    for tq, lo, hi in calls:
        s_len = hi * tq
        assert s_len >= n_sel
        qspec = lambda w, lo=lo, tq=tq: pl.BlockSpec((None, tq, w), lambda b, i: (b, i + lo, 0))
        kspec = lambda w, s_len=s_len: pl.BlockSpec((None, s_len, w), lambda b, i: (b, 0, 0),
                                                    pipeline_mode=pl.Buffered(1))
        kern = functools.partial(_dsa_attn_kernel, n_sel=n_sel, idx_heads=idx_heads, q_lo=lo)
        mix = pl.pallas_call(
            kern,
            grid=(bsz, hi - lo),
            in_specs=[qspec(aw), qspec(iw), qspec(LANE), kspec(aw), kspec(aw), kspec(LANE),
                      pl.BlockSpec(memory_space=pl.ANY)],
            out_specs=pl.BlockSpec((None, tq, aw), lambda b, i, lo=lo: (b, i + lo, col // aw)),
            out_shape=jax.ShapeDtypeStruct(mix.shape, mix.dtype),
            input_output_aliases={6: 0},
            compiler_params=_cparams(("arbitrary", "arbitrary"),
                                     VMEM_LIMIT_WIDE if tq == DSA_TQ else VMEM_LIMIT),
        )(*args, mix)
    return mix


def _pool_kernel(p_ref, h_ref, w_ref, s_ref, mix_ref, o_ref):
    del mix_ref
    tm, width = p_ref.shape
    grp = width // len(POOL_WINDOWS)
    t0 = pl.program_id(1) * tm
    x = p_ref[...]
    halo = jnp.where(pl.program_id(1) == 0, 0.0, h_ref[...])
    ext = jnp.concatenate([halo, x], axis=0)
    tpos = t0 + lax.broadcasted_iota(I32, (tm, 1), 0)
    acc = ext
    have = 1
    for gi, win in enumerate(POOL_WINDOWS):
        while have < win:
            shifted = jnp.concatenate([jnp.zeros((have, width), F32), acc[:-have]], axis=0)
            acc = acc + shifted
            have *= 2
        sl = slice(gi * grp, (gi + 1) * grp)
        cnt = jnp.minimum(tpos + 1, win).astype(F32)
        pooled = acc[POOL_HALO:, sl] / cnt - x[:, sl]
        y = _dot(pooled.astype(BF16), w_ref[gi])
        o_ref[:, sl] = (y * s_ref[:, sl]).astype(o_ref.dtype)


def _pool_mix(proj, off, pw, w_pool, scale, mix, col):
    bsz, p_len, d_mix = mix.shape
    width = pw["pin"]
    assert col % width == 0
    tm = _pick_tile(p_len, (256, 128))
    nb = p_len // tm
    c = off["pin"] // width
    hb = tm // POOL_HALO
    mix2 = mix.reshape(bsz * p_len, d_mix)
    out = pl.pallas_call(
        _pool_kernel,
        grid=(bsz, nb),
        in_specs=[pl.BlockSpec((tm, width), lambda b, i: (b * nb + i, c)),
                  pl.BlockSpec((POOL_HALO, width), lambda b, i: (jnp.maximum((b * nb + i) * hb - 1, 0), c)),
                  pl.BlockSpec(w_pool.shape, lambda b, i: (0, 0, 0)),
                  pl.BlockSpec(scale.shape, lambda b, i: (0, 0)),
                  pl.BlockSpec(memory_space=pl.ANY)],
        out_specs=pl.BlockSpec((tm, width), lambda b, i: (b * nb + i, col // width)),
        out_shape=jax.ShapeDtypeStruct(mix2.shape, mix2.dtype),
        input_output_aliases={4: 0},
        compiler_params=_cparams(("arbitrary", "arbitrary")),
    )(proj, proj, w_pool, scale, mix2)
    return out


def _layernorm_rows(x, g, b):
    mu = jnp.mean(x, axis=-1, keepdims=True)
    var = jnp.mean(jnp.square(x - mu), axis=-1, keepdims=True)
    return (x - mu) * lax.rsqrt(var + LN_EPS) * g + b


def _outproj_ln_kernel(x_ref, w_ref, h_ref, g_ref, b_ref, o_ref, ob_ref, pre_ref, *, alpha):
    j = pl.program_id(1)
    nj, _, tn = pre_ref.shape
    pre_ref[j] = alpha * h_ref[...] + _dot(x_ref[...], w_ref[...])

    @pl.when(j == nj - 1)
    def _():
        d = nj * tn
        total = pre_ref[0]
        for c in range(1, nj):
            total = total + pre_ref[c]
        mu = jnp.sum(total, axis=-1, keepdims=True) / d
        sq = jnp.square(pre_ref[0] - mu)
        for c in range(1, nj):
            sq = sq + jnp.square(pre_ref[c] - mu)
        inv = lax.rsqrt(jnp.sum(sq, axis=-1, keepdims=True) / d + LN_EPS)
        for c in range(nj):
            cs = slice(c * tn, (c + 1) * tn)
            y = (pre_ref[c] - mu) * inv * g_ref[:, cs] + b_ref[:, cs]
            o_ref[:, cs] = y
            ob_ref[:, cs] = y.astype(ob_ref.dtype)


def _outproj_ln(mix, w_out, h, g, b, alpha):
    r, k = mix.shape
    d = w_out.shape[1]
    tm = _pick_tile(r, (384, 256, 128))
    tn = _pick_tile(d, (512, 256, 128))
    kern = functools.partial(_outproj_ln_kernel, alpha=alpha)
    return pl.pallas_call(
        kern,
        grid=(r // tm, d // tn),
        in_specs=[pl.BlockSpec((tm, k), lambda i, j: (i, 0)),
                  pl.BlockSpec((k, tn), lambda i, j: (0, j)),
                  pl.BlockSpec((tm, tn), lambda i, j: (i, j)),
                  pl.BlockSpec((1, d), lambda i, j: (0, 0)),
                  pl.BlockSpec((1, d), lambda i, j: (0, 0))],
        out_specs=[pl.BlockSpec((tm, d), lambda i, j: (i, 0))] * 2,
        out_shape=[jax.ShapeDtypeStruct((r, d), F32), jax.ShapeDtypeStruct((r, d), BF16)],
        scratch_shapes=[pltpu.VMEM((d // tn, tm, tn), F32)],
        compiler_params=_cparams(("arbitrary", "arbitrary")),
    )(mix, w_out, h, g, b)


def _router_kernel(x_ref, w_ref, b_ref, e_o, g_o, *, n_groups, per_group):
    logits = _dot(x_ref[...], w_ref[...]) + b_ref[...]
    lane = lax.broadcasted_iota(I32, logits.shape, 1)
    big = jnp.int32(LANE)
    ninf = -jnp.inf
    gl = jnp.where(lane < n_groups, logits, ninf)
    ge = jnp.exp(gl - jnp.max(gl, axis=1, keepdims=True))
    gp = ge / jnp.sum(ge, axis=1, keepdims=True)
    g_val = jnp.max(gp, axis=1, keepdims=True)
    g_idx = jnp.min(jnp.where(gp == g_val, lane, big), axis=1, keepdims=True)
    lane_grp = jnp.where(lane >= n_groups, (lane - n_groups) // per_group, -1)
    in_grp = lane_grp == g_idx
    el = jnp.where(in_grp, logits, ninf)
    ee = jnp.exp(el - jnp.max(el, axis=1, keepdims=True))
    ep = jnp.where(in_grp, ee / jnp.sum(ee, axis=1, keepdims=True), -1.0)
    v1 = jnp.max(ep, axis=1, keepdims=True)
    i1 = jnp.min(jnp.where(ep == v1, lane, big), axis=1, keepdims=True)
    ep2 = jnp.where(lane == i1, -1.0, ep)
    v2 = jnp.max(ep2, axis=1, keepdims=True)
    i2 = jnp.min(jnp.where(ep2 == v2, lane, big), axis=1, keepdims=True)
    tot = v1 + v2
    e_o[...] = jnp.where(lane == 0, i1 - n_groups, jnp.where(lane == 1, i2 - n_groups, 0))
    g_o[...] = jnp.where(lane == 0, g_val * v1 / tot, jnp.where(lane == 1, g_val * v2 / tot, 0.0))


def _router(hb, w, b, n_groups, per_group):
    r, d = hb.shape
    tm = _pick_tile(r, (256, 128))
    kern = functools.partial(_router_kernel, n_groups=n_groups, per_group=per_group)
    return pl.pallas_call(
        kern,
        grid=(r // tm,),
        in_specs=[pl.BlockSpec((tm, d), lambda i: (i, 0)),
                  pl.BlockSpec((d, LANE), lambda i: (0, 0)),
                  pl.BlockSpec((1, LANE), lambda i: (0, 0))],
        out_specs=[pl.BlockSpec((tm, LANE), lambda i: (i, 0))] * 2,
        out_shape=[jax.ShapeDtypeStruct((r, LANE), I32), jax.ShapeDtypeStruct((r, LANE), F32)],
        compiler_params=_cparams(("arbitrary",)),
    )(hb, w, b)


def _gather_start(src_hbm, idx_ref, dst, sem):
    def issue(j, carry):
        pltpu.make_async_copy(src_hbm.at[pl.ds(idx_ref[0, 0, j], 1), :], dst.at[pl.ds(j, 1), :], sem).start()
        return carry
    lax.fori_loop(0, dst.shape[0], issue, 0)


def _gather_wait(src_hbm, dst, sem):
    pltpu.make_async_copy(src_hbm.at[pl.ds(0, dst.shape[0]), :], dst, sem).wait()


def _moe_up_kernel(be_ref, nb_ref, tok_ref, tok_next_ref, h_hbm, w1_ref, w3_ref, act_ref, xbuf, sem):
    del be_ref
    i = pl.program_id(0)
    n_used = nb_ref[0]
    slot = i % 2
    d = xbuf.shape[2]

    @pl.when(i == 0)
    def _():
        _gather_start(h_hbm, tok_ref, xbuf.at[0], sem.at[0])

    @pl.when(i + 1 < n_used)
    def _():
        _gather_start(h_hbm, tok_next_ref, xbuf.at[1 - slot], sem.at[1 - slot])

    @pl.when(i < n_used)
    def _():
        _gather_wait(h_hbm, xbuf.at[slot], sem.at[slot])
        h1 = jnp.zeros(act_ref.shape, F32)
        h3 = jnp.zeros(act_ref.shape, F32)
        for c in range(d // MOE_KC):
            ks = slice(c * MOE_KC, (c + 1) * MOE_KC)
            x = xbuf[slot, :, ks].astype(BF16)
            h1 = h1 + _dot(x, w1_ref[ks, :].astype(BF16))
            h3 = h3 + _dot(x, w3_ref[ks, :].astype(BF16))
        act_ref[...] = (h1 * _sigmoid(h1) * h3).astype(act_ref.dtype)

    @pl.when(i >= n_used)
    def _():
        act_ref[...] = jnp.zeros_like(act_ref)


def _moe_down_kernel(be_ref, nb_ref, act_ref, w2_ref, y_ref):
    del be_ref
    i = pl.program_id(0)

    @pl.when(i < nb_ref[0])
    def _():
        act = act_ref[...]
        for c in range(y_ref.shape[1] // MOE_KC):
            ks = slice(c * MOE_KC, (c + 1) * MOE_KC)
            y_ref[:, ks] = _dot(act, w2_ref[:, ks].astype(BF16))

    @pl.when(i >= nb_ref[0])
    def _():
        y_ref[...] = jnp.zeros_like(y_ref)


def _moe_ffn(h, w1, w3, w2, layer, blk_e, n_used, row_tok):
    r, d = h.shape
    nblk = blk_e.shape[0]
    de = w1.shape[3]
    bm = row_tok.shape[2]
    wspec = lambda a, b: pl.BlockSpec((None, None, a, b), lambda i, be, nb: (layer, be[i], 0, 0))
    up_spec = pltpu.PrefetchScalarGridSpec(
        num_scalar_prefetch=2,
        grid=(nblk,),
        in_specs=[pl.BlockSpec((1, 1, bm), lambda i, be, nb: (i, 0, 0), memory_space=pltpu.SMEM),
                  pl.BlockSpec((1, 1, bm), lambda i, be, nb: (jnp.minimum(i + 1, nblk - 1), 0, 0),
                               memory_space=pltpu.SMEM),
                  pl.BlockSpec(memory_space=pl.ANY), wspec(d, de), wspec(d, de)],
        out_specs=pl.BlockSpec((bm, de), lambda i, be, nb: (i, 0)),
        scratch_shapes=[pltpu.VMEM((2, bm, d), F32), pltpu.SemaphoreType.DMA((2,))],
    )
    act = pl.pallas_call(
        _moe_up_kernel,
        grid_spec=up_spec,
        out_shape=jax.ShapeDtypeStruct((nblk * bm, de), BF16),
        compiler_params=_cparams(("arbitrary",)),
    )(blk_e, n_used, row_tok, row_tok, h, w1, w3)
    down_spec = pltpu.PrefetchScalarGridSpec(
        num_scalar_prefetch=2,
        grid=(nblk,),
        in_specs=[pl.BlockSpec((bm, de), lambda i, be, nb: (i, 0)), wspec(de, d)],
        out_specs=pl.BlockSpec((bm, d), lambda i, be, nb: (i, 0)),
    )
    return pl.pallas_call(
        _moe_down_kernel,
        grid_spec=down_spec,
        out_shape=jax.ShapeDtypeStruct((nblk * bm, d), F32),
        compiler_params=_cparams(("arbitrary",)),
    )(blk_e, n_used, act, w2)


def _combine_ln_kernel(d0_ref, d1_ref, d0n_ref, d1n_ref, y_hbm, h_ref, gate_ref, g_ref, b_ref, o_ref, ob_ref,
                       ybuf, sem, *, alpha):
    i = pl.program_id(0)
    slot = i % 2

    @pl.when(i == 0)
    def _():
        _gather_start(y_hbm, d0_ref, ybuf.at[0, 0], sem.at[0, 0])
        _gather_start(y_hbm, d1_ref, ybuf.at[0, 1], sem.at[0, 1])

    @pl.when(i + 1 < pl.num_programs(0))
    def _():
        _gather_start(y_hbm, d0n_ref, ybuf.at[1 - slot, 0], sem.at[1 - slot, 0])
        _gather_start(y_hbm, d1n_ref, ybuf.at[1 - slot, 1], sem.at[1 - slot, 1])

    _gather_wait(y_hbm, ybuf.at[slot, 0], sem.at[slot, 0])
    _gather_wait(y_hbm, ybuf.at[slot, 1], sem.at[slot, 1])
    gate = gate_ref[...]
    ff = ybuf[slot, 0] * gate[:, 0:1] + ybuf[slot, 1] * gate[:, 1:2]
    y = _layernorm_rows(alpha * h_ref[...] + ff, g_ref[...], b_ref[...])
    o_ref[...] = y
    ob_ref[...] = y.astype(ob_ref.dtype)


def _combine_ln(yb, h, gates, dest0, dest1, g, b, alpha):
    r, d = h.shape
    tm = dest0.shape[2]
    nblk = r // tm
    kern = functools.partial(_combine_ln_kernel, alpha=alpha)
    idx_spec = pl.BlockSpec((1, 1, tm), lambda i: (i, 0, 0), memory_space=pltpu.SMEM)
    nxt_spec = pl.BlockSpec((1, 1, tm), lambda i: (jnp.minimum(i + 1, nblk - 1), 0, 0), memory_space=pltpu.SMEM)
    return pl.pallas_call(
        kern,
        grid=(nblk,),
        in_specs=[idx_spec, idx_spec, nxt_spec, nxt_spec,
                  pl.BlockSpec(memory_space=pl.ANY),
                  pl.BlockSpec((tm, d), lambda i: (i, 0)),
                  pl.BlockSpec((tm, LANE), lambda i: (i, 0)),
                  pl.BlockSpec((1, d), lambda i: (0, 0)),
                  pl.BlockSpec((1, d), lambda i: (0, 0))],
        out_specs=[pl.BlockSpec((tm, d), lambda i: (i, 0))] * 2,
        out_shape=[jax.ShapeDtypeStruct((r, d), F32), jax.ShapeDtypeStruct((r, d), BF16)],
        scratch_shapes=[pltpu.VMEM((2, 2, tm, d), F32), pltpu.SemaphoreType.DMA((2, 2))],
        compiler_params=_cparams(("arbitrary",)),
    )(dest0, dest1, dest0, dest1, yb, h, gates, g, b)


def _dispatch_plan(eid, n_experts, bm):
    r = eid.shape[0]
    a_tot = r * TOP_K
    e_flat = eid[:, :TOP_K].reshape(a_tot)
    onehot = (e_flat[:, None] == jnp.arange(n_experts, dtype=I32)[None, :]).astype(I32)
    before = jnp.cumsum(onehot, axis=0) - onehot
    rank = jnp.sum(before * onehot, axis=1)
    counts = jnp.sum(onehot, axis=0)
    pcounts = (counts + bm - 1) // bm * bm
    pends = jnp.cumsum(pcounts)
    pstarts = pends - pcounts
    dest = (pstarts[e_flat] + rank).astype(I32)
    nblk = -(-(a_tot + n_experts * (bm - 1)) // bm)
    row_tok = jnp.zeros((nblk * bm,), I32).at[dest].set(jnp.arange(a_tot, dtype=I32) // TOP_K)
    n_used = (pends[-1] // bm).astype(I32)
    blk = jnp.arange(nblk, dtype=I32)
    blk_e = jnp.searchsorted(pends, jnp.minimum(blk, n_used - 1) * bm, side="right").astype(I32)
    blk_e = jnp.minimum(blk_e, n_experts - 1)
    dest2 = dest.reshape(r, TOP_K)
    return row_tok.reshape(nblk, 1, bm), blk_e, n_used.reshape(1), dest2[:, 0], dest2[:, 1]


def kernel(x, meta, w_in, rw_mu, rw_w2, rw_w0, rw_a2, rw_a0, rw_g2, rw_kk, rw_ka, rw_rk, rw_gn_g, rw_gn_b,
           at_qnorm_g, at_w_uq, at_w_iq, at_kidx_g, at_kidx_b, pool_w, pool_scale, w_out, ln1_g, ln1_b,
           router_g_w, router_g_b, router_e_w, router_e_b, exp_w1, exp_w3, exp_w2, ln2_g, ln2_b):
    bsz, seq_len, d = x.shape
    depth = w_in.shape[0]
    n_meta = meta.shape[0]
    t_len = seq_len + n_meta
    p_len = _round_up(t_len, ROW_TILE)
    rows = bsz * p_len
    n_sel = min(TOPK_MAX, seq_len // 4)
    alpha = float((2 * depth) ** 0.25)

    rw = rw_w2.shape[2]
    aw = at_w_uq.shape[2]
    idx_head = at_kidx_g.shape[1]
    idx_heads = at_w_iq.shape[2] // idx_head
    assert idx_head == LANE and aw % AT_HEAD == 0 and rw % LANE == 0
    n_groups = router_g_w.shape[2]
    n_experts = router_e_w.shape[2]
    per_group = n_experts // n_groups
    assert n_groups + n_experts <= LANE
    pool_width = pool_w.shape[1] * pool_w.shape[2]

    src = {"r": rw, "k": rw, "v": rw, "dw": rw_w2.shape[1], "da": rw_a2.shape[1], "dg": rw_g2.shape[1],
           "cq": at_w_uq.shape[1], "ka": aw, "va": aw, "kidx": idx_head, "widx": idx_heads, "pin": pool_width}
    off, pw, n_proj = _layout(src)
    n_proj = _round_up(n_proj, 512)
    rw_names = ("r", "k", "v", "dw", "da", "dg")
    rw_src = {n: src[n] for n in rw_names}
    mu_starts = np.cumsum([0] + [rw_src[n] for n in rw_names])

    pos = jnp.arange(p_len, dtype=F32)
    inv = ROPE_THETA ** (-jnp.arange(0, AT_HEAD, 2, dtype=F32) / AT_HEAD)
    ang = pos[:, None] * inv[None, :]
    cosf = jnp.concatenate([jnp.cos(ang), jnp.cos(ang)], axis=1)
    sinf = jnp.concatenate([-jnp.sin(ang), jnp.sin(ang)], axis=1)

    h = jnp.concatenate([jnp.broadcast_to(meta.astype(x.dtype)[None], (bsz, n_meta, d)), x,
                         jnp.zeros((bsz, p_len - t_len, d), x.dtype)], axis=1).reshape(rows, d)
    hb = h.astype(BF16)
    row1 = lambda a: a.reshape(1, -1)
    pad_rows = lambda a, n: jnp.concatenate([a, jnp.zeros((n - a.shape[0],) + a.shape[1:], a.dtype)], axis=0)
    tm_c = ROW_TILE

    w_in_p = _relayout_cols(w_in, src, off, n_proj)

    for l in range(depth):
        proj = _matmul(hb, w_in_p, l, F32)

        mu = {}
        for i, n in enumerate(rw_names):
            seg = rw_mu[l, int(mu_starts[i]):int(mu_starts[i + 1])]
            mu[n] = row1(jnp.concatenate([seg, jnp.zeros((pw[n] - src[n],), F32)]))
        g2 = pad_rows(rw_g2[l], pw["dg"]).astype(BF16)
        seqs = _rwkv_prep(proj, bsz, p_len, off, pw, mu, rw_w2[l].astype(BF16), row1(rw_w0[l]),
                          rw_a2[l].astype(BF16), row1(rw_a0[l]), g2, row1(rw_kk[l]), row1(rw_ka[l]),
                          row1(rw_rk[l]))
        mix = jnp.zeros((bsz, p_len, rw + aw + pool_width), BF16)
        mix = _rwkv_scan(seqs, row1(rw_gn_g[l]), row1(rw_gn_b[l]), mix)

        q, qi, k_at, v_at, ki, wq = _dsa_prep(proj, bsz, p_len, off, pw, cosf, sinf, row1(at_qnorm_g[l]),
                                              at_w_uq[l].astype(BF16), at_w_iq[l].astype(BF16),
                                              row1(at_kidx_g[l]), row1(at_kidx_b[l]), idx_heads)
        mix = _dsa_attn(q, qi, wq, k_at, v_at, ki, mix, rw, n_sel, idx_heads)
        mix = _pool_mix(proj, off, pw, pool_w[l].astype(BF16), row1(pool_scale[l]), mix, rw + aw)
        h, hb = _outproj_ln(mix, w_out[l].astype(BF16), h, row1(ln1_g[l]), row1(ln1_b[l]), alpha)

        n_r = n_groups + n_experts
        w_r = jnp.concatenate([router_g_w[l], router_e_w[l], jnp.zeros((d, LANE - n_r), F32)], axis=1).astype(BF16)
        b_r = row1(jnp.concatenate([router_g_b[l], router_e_b[l], jnp.zeros((LANE - n_r,), F32)]))
        eid, gates = _router(hb, w_r, b_r, n_groups, per_group)
        row_tok, blk_e, n_used, dest0, dest1 = _dispatch_plan(eid, n_experts, MOE_BM)
        yb = _moe_ffn(h, exp_w1, exp_w3, exp_w2, l, blk_e, n_used, row_tok)
        h, hb = _combine_ln(yb, h, gates, dest0.reshape(rows // tm_c, 1, tm_c), dest1.reshape(rows // tm_c, 1, tm_c),
                            row1(ln2_g[l]), row1(ln2_b[l]), alpha)

    return h.reshape(bsz, p_len, d)[:, n_meta:t_len]
```

```python
import functools
import math

import jax
import jax.numpy as jnp
import numpy as np
from jax import lax
from jax.experimental import pallas as pl
from jax.experimental.pallas import tpu as pltpu

F32 = jnp.float32
BF16 = jnp.bfloat16
I32 = jnp.int32

LANE = 128
ROW_TILE = 128
RW_HEAD = 64
AT_HEAD = 128
TOPK_MAX = 256
ROPE_THETA = 10000.0
POOL_WINDOWS = (2, 4, 8, 16)
POOL_HALO = 16
GN_EPS = 64e-5
LN_EPS = 1e-5
RW_CHUNK = 64
MOE_BM = 384
MOE_KC = 512
DSA_TQ = 384
DSA_BIG_S = 3072
DSA_GROUPS = 8
DSA_TAIL_GROUPS = 3
Q_LOGIT_SCALE = AT_HEAD ** -0.5 * math.log2(math.e)
TOP_K = 2
INT_MIN = -(2 ** 31)
VMEM_LIMIT = 56 * 1024 * 1024
VMEM_LIMIT_WIDE = 60 * 1024 * 1024

HIGHEST = lax.Precision.HIGHEST
NN = (((1,), (0,)), ((), ()))
NT = (((1,), (1,)), ((), ()))
TN = (((0,), (0,)), ((), ()))


def _dot(a, b, dims=NN, precision=None):
    return lax.dot_general(a, b, dims, precision=precision, preferred_element_type=F32)


def _sigmoid(x):
    return 1.0 / (1.0 + jnp.exp(-x))


def _cparams(sem, vmem_limit=VMEM_LIMIT):
    return pltpu.CompilerParams(dimension_semantics=sem, vmem_limit_bytes=vmem_limit)


def _round_up(x, m):
    return -(-x // m) * m


def _pick_tile(n, candidates):
    for c in candidates:
        if n % c == 0:
            return c
    return n


def _layout(sizes):
    pw = {k: _round_up(v, LANE) for k, v in sizes.items()}
    order = sorted(sizes, key=lambda k: -pw[k])
    off, gaps, cur = {}, [], 0
    for k in order:
        w = pw[k]
        placed = False
        for gi, (g0, g1) in enumerate(gaps):
            s = _round_up(g0, w)
            if s + w <= g1:
                off[k] = s
                new = [(g0, s), (s + w, g1)]
                gaps[gi:gi + 1] = [g for g in new if g[1] > g[0]]
                placed = True
                break
        if not placed:
            s = _round_up(cur, w)
            if s > cur:
                gaps.append((cur, s))
            off[k] = s
            cur = s + w
    return off, pw, cur


def _relayout_kernel(w_ref, o_ref, *, moves):
    cur = 0
    rows, total = o_ref.shape
    for s0, w, d0 in moves:
        if d0 > cur:
            o_ref[:, cur:d0] = jnp.zeros((rows, d0 - cur), o_ref.dtype)
        o_ref[:, d0:d0 + w] = w_ref[:, s0:s0 + w].astype(o_ref.dtype)
        cur = d0 + w
    if total > cur:
        o_ref[:, cur:total] = jnp.zeros((rows, total - cur), o_ref.dtype)


def _relayout_cols(w, src_sizes, off, total):
    depth, k, n_src = w.shape
    names = list(src_sizes)
    starts = np.cumsum([0] + [src_sizes[n] for n in names])
    moves = tuple(sorted(((int(starts[i]), src_sizes[n], off[n]) for i, n in enumerate(names)), key=lambda m: m[2]))
    tk = _pick_tile(k, (256, 128))
    return pl.pallas_call(
        functools.partial(_relayout_kernel, moves=moves),
        grid=(depth, k // tk),
        in_specs=[pl.BlockSpec((None, tk, n_src), lambda l, i: (l, i, 0))],
        out_specs=pl.BlockSpec((None, tk, total), lambda l, i: (l, i, 0)),
        out_shape=jax.ShapeDtypeStruct((depth, k, total), BF16),
        compiler_params=_cparams(("arbitrary", "arbitrary")),
    )(w)


def _mm_kernel(x_ref, w_ref, o_ref):
    o_ref[...] = _dot(x_ref[...], w_ref[...]).astype(o_ref.dtype)


def _matmul(x, w, layer, out_dtype):
    r, k = x.shape
    n = w.shape[2]
    tm = _pick_tile(r, (768, 512, 384, 256, 128))
    tn = _pick_tile(n, (1024, 512, 256, 128))
    return pl.pallas_call(
        _mm_kernel,
        grid=(n // tn, r // tm),
        in_specs=[pl.BlockSpec((tm, k), lambda j, i: (i, 0)),
                  pl.BlockSpec((None, k, tn), lambda j, i: (layer, 0, j))],
        out_specs=pl.BlockSpec((tm, tn), lambda j, i: (i, j)),
        out_shape=jax.ShapeDtypeStruct((r, n), out_dtype),
        compiler_params=_cparams(("arbitrary", "arbitrary")),
    )(x, w)


def _head_sums(x, hsum):
    tiles = [_dot(x[:, c * LANE:(c + 1) * LANE], hsum, precision=HIGHEST) for c in range(x.shape[1] // LANE)]
    return jnp.concatenate(tiles, axis=1)


def _split3(x):
    hi = x.astype(BF16)
    r1 = x - hi.astype(F32)
    mid = r1.astype(BF16)
    lo = (r1 - mid.astype(F32)).astype(BF16)
    return hi, mid, lo


def _rwkv_prep_kernel(r_ref, k_ref, v_ref, dw_ref, da_ref, dg_ref,
                      rh_ref, kh_ref, vh_ref, dwh_ref, dah_ref, dgh_ref,
                      mur_ref, muk_ref, muv_ref, mudw_ref, muda_ref, mudg_ref,
                      w2_ref, w0_ref, a2_ref, a0_ref, g2_ref, kkp_ref, kap_ref, rk_ref, hsum_ref,
                      tri_ref, blk_ref,
                      at_o, bt_o, kt_o, rt_o, bc_o, kc_o, v_o, epc_o, bv_o, g_o):
    first = pl.program_id(1) == 0

    def shift(x_ref, h_ref, mu_ref):
        x = x_ref[...]
        prev_row = jnp.where(first, 0.0, h_ref[7:8, :])
        row = lax.broadcasted_iota(I32, x.shape, 0)
        prev = jnp.where(row == 0, prev_row, pltpu.roll(x, 1, axis=0))
        return x + (prev - x) * mu_ref[...]

    r = shift(r_ref, rh_ref, mur_ref)
    k = shift(k_ref, kh_ref, muk_ref)
    v = shift(v_ref, vh_ref, muv_ref)
    dw = shift(dw_ref, dwh_ref, mudw_ref)
    da = shift(da_ref, dah_ref, muda_ref)
    dg = shift(dg_ref, dgh_ref, mudg_ref)

    wl = w0_ref[...] + _dot(jnp.tanh(dw).astype(BF16), w2_ref[...])
    neg = -wl
    softplus = jnp.maximum(neg, 0.0) + jnp.log(1.0 + jnp.exp(-jnp.abs(neg)))
    lw = -jnp.exp(-softplus - 0.5)
    a = _sigmoid(a0_ref[...] + _dot(da.astype(BF16), a2_ref[...]))
    g_o[...] = _dot(_sigmoid(dg).astype(BF16), g2_ref[...])
    hsum = hsum_ref[...]
    kk = k * kkp_ref[...]
    kk = kk / jnp.maximum(jnp.sqrt(_head_sums(kk * kk, hsum)), 1e-12)
    k = k * (1.0 + (a - 1.0) * kap_ref[...])
    bv_o[...] = _head_sums(r * k * rk_ref[...], hsum) * v

    parts = _split3(lw)
    tri, blk = tri_ref[...], blk_ref[...]
    cl = _dot(tri, parts[0]) + _dot(tri, parts[1]) + _dot(tri, parts[2])
    clc = _dot(blk, parts[0]) + _dot(blk, parts[1]) + _dot(blk, parts[2])
    e_n = jnp.exp(-cl)
    e_nc = jnp.exp(clc - cl)
    kka = kk * a
    at_o[...] = (-kk * jnp.exp(cl - lw)).astype(at_o.dtype)
    bt_o[...] = (kka * e_n).astype(bt_o.dtype)
    kt_o[...] = (k * e_n).astype(kt_o.dtype)
    rt_o[...] = (r * jnp.exp(cl)).astype(rt_o.dtype)
    bc_o[...] = (kka * e_nc).astype(bc_o.dtype)
    kc_o[...] = (k * e_nc).astype(kc_o.dtype)
    v_o[...] = v.astype(v_o.dtype)
    epc_o[...] = jnp.exp(clc)


def _rwkv_prep(proj, bsz, p_len, off, pw, mu, w2, w0, a2, a0, g2, kkp, kap, rk):
    rw = w2.shape[1]
    tm = ROW_TILE
    nb = p_len // tm
    names = ("r", "k", "v", "dw", "da", "dg")

    def cur_spec(n):
        w, c = pw[n], off[n] // pw[n]
        return pl.BlockSpec((tm, w), lambda b, i, c=c: (b * nb + i, c))

    def halo_spec(n):
        w, c = pw[n], off[n] // pw[n]
        return pl.BlockSpec((8, w), lambda b, i, c=c: (jnp.maximum((b * nb + i) * (tm // 8) - 1, 0), c))

    def full(a):
        return pl.BlockSpec(a.shape, lambda b, i: (0,) * a.ndim)

    lane = np.arange(LANE)
    hsum = jnp.asarray((lane[:, None] // RW_HEAD == lane[None, :] // RW_HEAD).astype(np.float32))
    t = np.arange(tm)
    same = t[:, None] // RW_CHUNK == t[None, :] // RW_CHUNK
    tri = jnp.asarray((same & (t[None, :] <= t[:, None])).astype(np.float32)).astype(BF16)
    blk = jnp.asarray(same.astype(np.float32)).astype(BF16)
    params = [mu[n] for n in names] + [w2, w0, a2, a0, g2, kkp, kap, rk, hsum, tri, blk]
    out_spec = pl.BlockSpec((tm, rw), lambda b, i: (b * nb + i, 0))
    sd = lambda dt: jax.ShapeDtypeStruct((bsz * p_len, rw), dt)
    return pl.pallas_call(
        _rwkv_prep_kernel,
        grid=(bsz, nb),
        in_specs=[cur_spec(n) for n in names] + [halo_spec(n) for n in names] + [full(a) for a in params],
        out_specs=[out_spec] * 10,
        out_shape=[sd(BF16)] * 7 + [sd(F32)] * 3,
        compiler_params=_cparams(("arbitrary", "arbitrary")),
    )(*([proj] * 12), *params)


def _rwkv_heads(at, bt, kt, rt, bc, kc, v, p_row, zt):
    n = len(at)
    hs = range(n)
    c = at[0].shape[0]
    b16 = lambda x: x.astype(BF16)
    row = lax.broadcasted_iota(I32, (2 * c, c), 0)
    col = lax.broadcasted_iota(I32, (2 * c, c), 1)
    keep = col < (row & (c - 1)) + jnp.where(row < c, 0, 1)
    lhs2 = [jnp.concatenate([at[h], rt[h]], axis=0) for h in hs]
    x_b = [jnp.where(keep, _dot(lhs2[h], bt[h], NT), 0.0) for h in hs]
    x_k = [jnp.where(keep, _dot(lhs2[h], kt[h], NT), 0.0) for h in hs]
    lkmv = [_dot(b16(x_k[h]), v[h]) for h in hs]
    l_ba = [x[:c] for x in x_b]
    m_b = [b16(x[c:]) for x in x_b]

    ti = lax.broadcasted_iota(I32, (c, c), 0)
    si = lax.broadcasted_iota(I32, (c, c), 1)
    zero = jnp.zeros((c, c), F32)
    eye = (ti == si).astype(F32)
    same_lo = (ti >> 4) == (si >> 4)
    lp = [jnp.where(same_lo, l, zero) for l in l_ba]
    t = [eye + l for l in lp]
    for _ in range(3):
        lp = [_dot(b16(l), b16(l)) for l in lp]
        t = [t[h] + _dot(b16(lp[h]), b16(t[h])) for h in hs]
    size = 32
    while size <= c:
        shift = size.bit_length() - 1
        same_hi = (ti >> shift) == (si >> shift)
        l_off = [b16(jnp.where(same_hi, jnp.where(same_lo, zero, l), zero)) for l in l_ba]
        tl = [_dot(b16(t[h]), l_off[h]) for h in hs]
        t = [t[h] + _dot(b16(tl[h]), b16(t[h])) for h in hs]
        same_lo = same_hi
        size *= 2

    tb = [b16(x) for x in t]
    a_z = [b16(_dot(tb[h], at[h])) for h in hs]
    w_u = [b16(_dot(tb[h], b16(lkmv[h][:c]))) for h in hs]
    ztb = [b16(z) for z in zt]
    q_z = [b16(rt[h].astype(F32) + _dot(m_b[h], a_z[h])) for h in hs]
    y = [_dot(q_z[h], ztb[h], NT) + _dot(m_b[h], w_u[h]) + lkmv[h][c:] for h in hs]
    g_t = [b16(_dot(a_z[h], bc[h], TN)) for h in hs]
    h_t = [_dot(w_u[h], bc[h], TN) + _dot(v[h], kc[h], TN) for h in hs]
    zt_new = [p_row[h] * zt[h] + _dot(ztb[h], g_t[h]) + h_t[h] for h in hs]
    return y, zt_new


def _rwkv_scan_kernel(at_ref, bt_ref, kt_ref, rt_ref, bc_ref, kc_ref, v_ref, epc_ref, bv_ref, g_ref,
                      gg_ref, gb_ref, mix_ref, o_ref, z_ref):
    del mix_ref
    @pl.when(pl.program_id(2) == 0)
    def _():
        z_ref[...] = jnp.zeros_like(z_ref)

    n_heads = o_ref.shape[1] // RW_HEAD
    sls = [slice(h * RW_HEAD, (h + 1) * RW_HEAD) for h in range(n_heads)]
    cut = lambda ref: [ref[:, sl] for sl in sls]
    y, z_new = _rwkv_heads(cut(at_ref), cut(bt_ref), cut(kt_ref), cut(rt_ref), cut(bc_ref), cut(kc_ref),
                           cut(v_ref), [epc_ref[0:1, sl] for sl in sls], [z_ref[h] for h in range(n_heads)])
    for h in range(n_heads):
        z_ref[h] = z_new[h]
    outs = []
    for h, sl in enumerate(sls):
        mu = jnp.mean(y[h], axis=-1, keepdims=True)
        var = jnp.mean(jnp.square(y[h] - mu), axis=-1, keepdims=True)
        yn = (y[h] - mu) * lax.rsqrt(var + GN_EPS) * gg_ref[:, sl] + gb_ref[:, sl]
        outs.append((yn + bv_ref[:, sl]) * g_ref[:, sl])
    per = LANE // RW_HEAD
    for p in range(n_heads // per):
        o_ref[:, p * LANE:(p + 1) * LANE] = jnp.concatenate(outs[p * per:(p + 1) * per], axis=1).astype(o_ref.dtype)


def _rwkv_scan(seqs, gn_g, gn_b, mix):
    bsz, p_len, _ = mix.shape
    rw = gn_g.shape[1]
    width = _pick_tile(rw, (12 * LANE, 4 * LANE, 3 * LANE, 2 * LANE, LANE))
    nch = p_len // RW_CHUNK
    seqs = [s.reshape(bsz, p_len, rw) for s in seqs]
    seq_spec = pl.BlockSpec((None, RW_CHUNK, width), lambda b, p, c: (b, c, p))
    par_spec = pl.BlockSpec((1, width), lambda b, p, c: (0, p))
    return pl.pallas_call(
        _rwkv_scan_kernel,
        grid=(bsz, rw // width, nch),
        in_specs=[seq_spec] * len(seqs) + [par_spec] * 2 + [pl.BlockSpec(memory_space=pl.ANY)],
        out_specs=seq_spec,
        out_shape=jax.ShapeDtypeStruct(mix.shape, mix.dtype),
        input_output_aliases={len(seqs) + 2: 0},
        scratch_shapes=[pltpu.VMEM((width // RW_HEAD, RW_HEAD, RW_HEAD), F32)],
        compiler_params=_cparams(("arbitrary", "arbitrary", "arbitrary")),
    )(*seqs, gn_g, gn_b, mix)


def _rope_tiles(x, cosf, sinf):
    outs = []
    for c in range(x.shape[1] // LANE):
        xt = x[:, c * LANE:(c + 1) * LANE]
        outs.append(xt * cosf + pltpu.roll(xt, LANE // 2, axis=1) * sinf)
    return outs


def _dsa_prep_kernel(cq_ref, ka_ref, va_ref, kidx_ref, widx_ref, cos_ref, sin_ref, qg_ref, wuq_ref, wiq_ref,
                     kg_ref, kb_ref, q_o, qi_o, k_o, v_o, ki_o, wq_o, *, wq_scale):
    cosf, sinf = cos_ref[...], sin_ref[...]
    v_o[...] = va_ref[...].astype(v_o.dtype)
    cq = cq_ref[...]
    cqn = cq * lax.rsqrt(jnp.mean(jnp.square(cq), axis=-1, keepdims=True) + 1e-6) * qg_ref[...]
    cqb = cqn.astype(BF16)
    for c, t in enumerate(_rope_tiles(_dot(cqb, wuq_ref[...]), cosf, sinf)):
        q_o[:, c * LANE:(c + 1) * LANE] = (t * Q_LOGIT_SCALE).astype(q_o.dtype)
    for c, t in enumerate(_rope_tiles(_dot(cqb, wiq_ref[...]), cosf, sinf)):
        qi_o[:, c * LANE:(c + 1) * LANE] = t.astype(qi_o.dtype)
    for c, t in enumerate(_rope_tiles(ka_ref[...], cosf, sinf)):
        k_o[:, c * LANE:(c + 1) * LANE] = t.astype(k_o.dtype)
    kx = kidx_ref[...]
    mu = jnp.mean(kx, axis=-1, keepdims=True)
    var = jnp.mean(jnp.square(kx - mu), axis=-1, keepdims=True)
    kn = (kx - mu) * lax.rsqrt(var + LN_EPS) * kg_ref[...] + kb_ref[...]
    ki_o[...] = _rope_tiles(kn, cosf, sinf)[0].astype(ki_o.dtype)
    wq_o[...] = widx_ref[...] * wq_scale


def _dsa_prep(proj, bsz, p_len, off, pw, cosf, sinf, qg, wuq, wiq, kg, kb, idx_heads):
    tm = _pick_tile(p_len, (256, 128))
    nb = p_len // tm
    rows = bsz * p_len
    aw, iw = wuq.shape[1], wiq.shape[1]

    def seg(n):
        w, c = pw[n], off[n] // pw[n]
        return pl.BlockSpec((tm, w), lambda i, c=c: (i, c))

    def full(a):
        return pl.BlockSpec(a.shape, lambda i: (0,) * a.ndim)

    pos_spec = pl.BlockSpec((tm, LANE), lambda i: (i % nb, 0))
    row_spec = lambda w: pl.BlockSpec((tm, w), lambda i: (i, 0))
    kern = functools.partial(_dsa_prep_kernel, wq_scale=float(idx_heads ** -0.5 * AT_HEAD ** -0.5))
    return pl.pallas_call(
        kern,
        grid=(rows // tm,),
        in_specs=[seg("cq"), seg("ka"), seg("va"), seg("kidx"), seg("widx"), pos_spec, pos_spec,
                  full(qg), full(wuq), full(wiq), full(kg), full(kb)],
        out_specs=[row_spec(aw), row_spec(iw), row_spec(aw), row_spec(aw), row_spec(LANE), row_spec(LANE)],
        out_shape=[jax.ShapeDtypeStruct((rows, aw), BF16), jax.ShapeDtypeStruct((rows, iw), BF16),
                   jax.ShapeDtypeStruct((rows, aw), BF16), jax.ShapeDtypeStruct((rows, aw), BF16),
                   jax.ShapeDtypeStruct((rows, LANE), BF16), jax.ShapeDtypeStruct((rows, LANE), F32)],
        compiler_params=_cparams(("arbitrary",)),
    )(proj, proj, proj, proj, proj, cosf, sinf, qg, wuq, wiq, kg, kb)


def _topk_mask(sc, causal, n_sel):
    tq, s_len = sc.shape
    bits = lax.bitcast_convert_type(jnp.where(sc == 0.0, 0.0, sc), I32)
    key = jnp.where(bits < 0, bits ^ jnp.int32(0x7FFFFFFF), bits)
    key = jnp.where(causal, key, jnp.int32(INT_MIN))
    one = jnp.ones((tq, s_len), I32)
    zero = jnp.zeros((tq, s_len), I32)

    i16 = jnp.int16
    one16 = jnp.ones((tq, s_len), i16)
    zero16 = jnp.zeros((tq, s_len), i16)

    def count_ge16(x16, th):
        hit = jnp.where(x16 >= th.astype(i16), one16, zero16)
        acc = hit[:, :LANE]
        for j in range(1, s_len // LANE):
            acc = acc + hit[:, j * LANE:(j + 1) * LANE]
        return jnp.sum(acc.astype(I32), axis=1, keepdims=True)

    def kth_largest16(x16):
        t = jnp.where(count_ge16(x16, jnp.zeros((tq, 1), I32)) >= n_sel, jnp.int32(0), jnp.int32(-32768))

        def step(i, t):
            cand = t | jnp.left_shift(jnp.int32(1), 14 - i)
            return jnp.where(count_ge16(x16, cand) >= n_sel, cand, t)

        return lax.fori_loop(0, 15, step, t)

    hi = (key >> 16).astype(i16)
    lo = ((key & jnp.int32(0xFFFF)) - 32768).astype(i16)
    tau_hi = kth_largest16(hi)
    th16 = tau_hi.astype(i16)
    lo_sel = jnp.where(hi == th16, lo, jnp.where(hi > th16, i16(32767), i16(-32768)))
    tau = tau_hi * 65536 + (kth_largest16(lo_sel) + 32768)
    gt = key > tau
    eq = key == tau
    n_gt = jnp.sum(jnp.where(gt, one, zero), axis=1, keepdims=True)
    n_eq = jnp.sum(jnp.where(eq, one, zero), axis=1, keepdims=True)
    need = n_sel - n_gt
    idx = lax.broadcasted_iota(I32, (tq, s_len), 1)
    nbits = max(1, (s_len - 1).bit_length())

    def cut_search():
        def cut_step(i, lo):
            cand = lo | jnp.left_shift(jnp.int32(1), nbits - 1 - i)
            cnt = jnp.sum(jnp.where(eq, jnp.where(idx < cand, one, zero), zero), axis=1, keepdims=True)
            return jnp.where(cnt < need, cand, lo)
        return lax.fori_loop(0, nbits, cut_step, jnp.zeros((tq, 1), I32))

    surplus = jnp.where(tau > jnp.int32(INT_MIN), jnp.where(n_eq > need, 1, 0), 0)
    cut = lax.cond(jnp.max(surplus) > 0, cut_search, lambda: jnp.full((tq, 1), s_len, I32))
    sel = jnp.where(gt, one, jnp.where(eq, jnp.where(idx <= cut, one, zero), zero))
    return jnp.where(causal, sel, zero) > 0


def _dsa_attn_kernel(q_ref, qi_ref, wq_ref, k_ref, v_ref, ki_ref, mix_ref, o_ref, *, n_sel, idx_heads, q_lo):
    del mix_ref
    tq = q_ref.shape[0]
    s_len = k_ref.shape[0]
    t0 = (pl.program_id(1) + q_lo) * tq
    ki = ki_ref[...]
    wq = wq_ref[...]
    sc = jnp.zeros((tq, s_len), F32)
    for h in range(idx_heads):
        s_h = _dot(qi_ref[:, h * LANE:(h + 1) * LANE], ki, NT)
        sc = sc + jnp.maximum(s_h, 0.0) * wq[:, h:h + 1]
    qpos = t0 + lax.broadcasted_iota(I32, (tq, s_len), 0)
    kpos = lax.broadcasted_iota(I32, (tq, s_len), 1)
    mask = _topk_mask(sc, kpos <= qpos, n_sel)
    bias = jnp.where(mask, 0.0, -jnp.inf)
    for h in range(q_ref.shape[1] // AT_HEAD):
        sl = slice(h * AT_HEAD, (h + 1) * AT_HEAD)
        lg = _dot(q_ref[:, sl], k_ref[:, sl], NT) + bias
        p = jnp.exp2(lg - jnp.max(lg, axis=1, keepdims=True))
        den = jnp.sum(p, axis=1, keepdims=True)
        o_ref[:, sl] = (_dot(p.astype(BF16), v_ref[:, sl]) / den).astype(o_ref.dtype)


def _dsa_attn(q, qi, wq, k, v, ki, mix, col, n_sel, idx_heads):
    bsz, p_len, _ = mix.shape
    aw, iw = q.shape[1], qi.shape[1]
    assert col % aw == 0
    big_rows = min(p_len, DSA_BIG_S) // DSA_TQ * DSA_TQ
    calls = []
    for tq, row0, row1, n_groups in ((DSA_TQ, 0, big_rows, DSA_GROUPS),
                                     (ROW_TILE, big_rows, p_len, DSA_TAIL_GROUPS)):
        nq = (row1 - row0) // tq
        n_groups = min(n_groups, nq)
        bounds = [row0 // tq + round(nq * g / n_groups) for g in range(n_groups + 1)] if nq else []
        calls += [(tq, lo, hi) for lo, hi in zip(bounds[:-1], bounds[1:])]
    r3 = lambda a: a.reshape(bsz, p_len, a.shape[1])
    args = (r3(q), r3(qi), r3(wq), r3(k), r3(v), r3(ki))
    for tq, lo, hi in calls:
        s_len = hi * tq
        assert s_len >= n_sel
        qspec = lambda w, lo=lo, tq=tq: pl.BlockSpec((None, tq, w), lambda b, i: (b, i + lo, 0))
        kspec = lambda w, s_len=s_len: pl.BlockSpec((None, s_len, w), lambda b, i: (b, 0, 0),
                                                    pipeline_mode=pl.Buffered(1))
        kern = functools.partial(_dsa_attn_kernel, n_sel=n_sel, idx_heads=idx_heads, q_lo=lo)
        mix = pl.pallas_call(
            kern,
            grid=(bsz, hi - lo),
            in_specs=[qspec(aw), qspec(iw), qspec(LANE), kspec(aw), kspec(aw), kspec(LANE),
                      pl.BlockSpec(memory_space=pl.ANY)],
            out_specs=pl.BlockSpec((None, tq, aw), lambda b, i, lo=lo: (b, i + lo, col // aw)),
            out_shape=jax.ShapeDtypeStruct(mix.shape, mix.dtype),
            input_output_aliases={6: 0},
            compiler_params=_cparams(("arbitrary", "arbitrary"),
                                     VMEM_LIMIT_WIDE if tq == DSA_TQ else VMEM_LIMIT),
        )(*args, mix)
    return mix


def _pool_kernel(p_ref, h_ref, w_ref, s_ref, mix_ref, o_ref):
    del mix_ref
    tm, width = p_ref.shape
    grp = width // len(POOL_WINDOWS)
    t0 = pl.program_id(1) * tm
    x = p_ref[...]
    halo = jnp.where(pl.program_id(1) == 0, 0.0, h_ref[...])
    ext = jnp.concatenate([halo, x], axis=0)
    tpos = t0 + lax.broadcasted_iota(I32, (tm, 1), 0)
    acc = ext
    have = 1
    for gi, win in enumerate(POOL_WINDOWS):
        while have < win:
            shifted = jnp.concatenate([jnp.zeros((have, width), F32), acc[:-have]], axis=0)
            acc = acc + shifted
            have *= 2
        sl = slice(gi * grp, (gi + 1) * grp)
        cnt = jnp.minimum(tpos + 1, win).astype(F32)
        pooled = acc[POOL_HALO:, sl] / cnt - x[:, sl]
        y = _dot(pooled.astype(BF16), w_ref[gi])
        o_ref[:, sl] = (y * s_ref[:, sl]).astype(o_ref.dtype)


def _pool_mix(proj, off, pw, w_pool, scale, mix, col):
    bsz, p_len, d_mix = mix.shape
    width = pw["pin"]
    assert col % width == 0
    tm = _pick_tile(p_len, (256, 128))
    nb = p_len // tm
    c = off["pin"] // width
    hb = tm // POOL_HALO
    mix2 = mix.reshape(bsz * p_len, d_mix)
    out = pl.pallas_call(
        _pool_kernel,
        grid=(bsz, nb),
        in_specs=[pl.BlockSpec((tm, width), lambda b, i: (b * nb + i, c)),
                  pl.BlockSpec((POOL_HALO, width), lambda b, i: (jnp.maximum((b * nb + i) * hb - 1, 0), c)),
                  pl.BlockSpec(w_pool.shape, lambda b, i: (0, 0, 0)),
                  pl.BlockSpec(scale.shape, lambda b, i: (0, 0)),
                  pl.BlockSpec(memory_space=pl.ANY)],
        out_specs=pl.BlockSpec((tm, width), lambda b, i: (b * nb + i, col // width)),
        out_shape=jax.ShapeDtypeStruct(mix2.shape, mix2.dtype),
        input_output_aliases={4: 0},
        compiler_params=_cparams(("arbitrary", "arbitrary")),
    )(proj, proj, w_pool, scale, mix2)
    return out


def _layernorm_rows(x, g, b):
    mu = jnp.mean(x, axis=-1, keepdims=True)
    var = jnp.mean(jnp.square(x - mu), axis=-1, keepdims=True)
    return (x - mu) * lax.rsqrt(var + LN_EPS) * g + b


def _outproj_ln_kernel(x_ref, w_ref, h_ref, g_ref, b_ref, o_ref, ob_ref, pre_ref, *, alpha):
    j = pl.program_id(1)
    nj, _, tn = pre_ref.shape
    pre_ref[j] = alpha * h_ref[...] + _dot(x_ref[...], w_ref[...])

    @pl.when(j == nj - 1)
    def _():
        d = nj * tn
        total = pre_ref[0]
        for c in range(1, nj):
            total = total + pre_ref[c]
        mu = jnp.sum(total, axis=-1, keepdims=True) / d
        sq = jnp.square(pre_ref[0] - mu)
        for c in range(1, nj):
            sq = sq + jnp.square(pre_ref[c] - mu)
        inv = lax.rsqrt(jnp.sum(sq, axis=-1, keepdims=True) / d + LN_EPS)
        for c in range(nj):
            cs = slice(c * tn, (c + 1) * tn)
            y = (pre_ref[c] - mu) * inv * g_ref[:, cs] + b_ref[:, cs]
            o_ref[:, cs] = y
            ob_ref[:, cs] = y.astype(ob_ref.dtype)


def _outproj_ln(mix, w_out, h, g, b, alpha):
    r, k = mix.shape
    d = w_out.shape[1]
    tm = _pick_tile(r, (384, 256, 128))
    tn = _pick_tile(d, (512, 256, 128))
    kern = functools.partial(_outproj_ln_kernel, alpha=alpha)
    return pl.pallas_call(
        kern,
        grid=(r // tm, d // tn),
        in_specs=[pl.BlockSpec((tm, k), lambda i, j: (i, 0)),
                  pl.BlockSpec((k, tn), lambda i, j: (0, j)),
                  pl.BlockSpec((tm, tn), lambda i, j: (i, j)),
                  pl.BlockSpec((1, d), lambda i, j: (0, 0)),
                  pl.BlockSpec((1, d), lambda i, j: (0, 0))],
        out_specs=[pl.BlockSpec((tm, d), lambda i, j: (i, 0))] * 2,
        out_shape=[jax.ShapeDtypeStruct((r, d), F32), jax.ShapeDtypeStruct((r, d), BF16)],
        scratch_shapes=[pltpu.VMEM((d // tn, tm, tn), F32)],
        compiler_params=_cparams(("arbitrary", "arbitrary")),
    )(mix, w_out, h, g, b)


def _router_kernel(x_ref, w_ref, b_ref, e_o, g_o, *, n_groups, per_group):
    logits = _dot(x_ref[...], w_ref[...]) + b_ref[...]
    lane = lax.broadcasted_iota(I32, logits.shape, 1)
    big = jnp.int32(LANE)
    ninf = -jnp.inf
    gl = jnp.where(lane < n_groups, logits, ninf)
    ge = jnp.exp(gl - jnp.max(gl, axis=1, keepdims=True))
    gp = ge / jnp.sum(ge, axis=1, keepdims=True)
    g_val = jnp.max(gp, axis=1, keepdims=True)
    g_idx = jnp.min(jnp.where(gp == g_val, lane, big), axis=1, keepdims=True)
    lane_grp = jnp.where(lane >= n_groups, (lane - n_groups) // per_group, -1)
    in_grp = lane_grp == g_idx
    el = jnp.where(in_grp, logits, ninf)
    ee = jnp.exp(el - jnp.max(el, axis=1, keepdims=True))
    ep = jnp.where(in_grp, ee / jnp.sum(ee, axis=1, keepdims=True), -1.0)
    v1 = jnp.max(ep, axis=1, keepdims=True)
    i1 = jnp.min(jnp.where(ep == v1, lane, big), axis=1, keepdims=True)
    ep2 = jnp.where(lane == i1, -1.0, ep)
    v2 = jnp.max(ep2, axis=1, keepdims=True)
    i2 = jnp.min(jnp.where(ep2 == v2, lane, big), axis=1, keepdims=True)
    tot = v1 + v2
    e_o[...] = jnp.where(lane == 0, i1 - n_groups, jnp.where(lane == 1, i2 - n_groups, 0))
    g_o[...] = jnp.where(lane == 0, g_val * v1 / tot, jnp.where(lane == 1, g_val * v2 / tot, 0.0))


def _router(hb, w, b, n_groups, per_group):
    r, d = hb.shape
    tm = _pick_tile(r, (256, 128))
    kern = functools.partial(_router_kernel, n_groups=n_groups, per_group=per_group)
    return pl.pallas_call(
        kern,
        grid=(r // tm,),
        in_specs=[pl.BlockSpec((tm, d), lambda i: (i, 0)),
                  pl.BlockSpec((d, LANE), lambda i: (0, 0)),
                  pl.BlockSpec((1, LANE), lambda i: (0, 0))],
        out_specs=[pl.BlockSpec((tm, LANE), lambda i: (i, 0))] * 2,
        out_shape=[jax.ShapeDtypeStruct((r, LANE), I32), jax.ShapeDtypeStruct((r, LANE), F32)],
        compiler_params=_cparams(("arbitrary",)),
    )(hb, w, b)


def _slab_pitch(ns):
    tiles = -(-ns // 8)
    return 8 * (tiles + 1 - tiles % 2)


def _to_slabs_kernel(x_ref, o_ref):
    tm, pitch, _ = o_ref.shape
    ns = x_ref.shape[1] // LANE
    for c in range(ns):
        o_ref[:, c, :] = x_ref[:, c * LANE:(c + 1) * LANE]
    if pitch > ns:
        o_ref[:, ns:, :] = jnp.zeros((tm, pitch - ns, LANE), o_ref.dtype)


def _to_slabs(x):
    r, d = x.shape
    pitch = _slab_pitch(d // LANE)
    tm = _pick_tile(r, (256, 128))
    return pl.pallas_call(
        _to_slabs_kernel,
        grid=(r // tm,),
        in_specs=[pl.BlockSpec((tm, d), lambda i: (i, 0))],
        out_specs=pl.BlockSpec((tm, pitch, LANE), lambda i: (i, 0, 0)),
        out_shape=jax.ShapeDtypeStruct((r, pitch, LANE), x.dtype),
        compiler_params=_cparams(("arbitrary",)),
    )(x)


def _gather_start(src_hbm, idx_ref, dst, sem):
    def issue(j, carry):
        pltpu.make_async_copy(src_hbm.at[pl.ds(idx_ref[0, 0, j], 1)], dst.at[pl.ds(j, 1)], sem).start()
        return carry
    lax.fori_loop(0, dst.shape[0], issue, 0)


def _gather_wait(src_hbm, dst, sem):
    pltpu.make_async_copy(src_hbm.at[pl.ds(0, dst.shape[0])], dst, sem).wait()


def _slab_cols(ref3, lo, hi):
    return jnp.concatenate([ref3[:, c, :] for c in range(lo, hi)], axis=1)


def _moe_up_kernel(be_ref, nb_ref, tok_ref, tok_next_ref, h_hbm, w1_ref, w3_ref, act_ref, xbuf, sem):
    del be_ref
    i = pl.program_id(0)
    n_used = nb_ref[0]
    slot = i % 2
    d = w1_ref.shape[0]
    ns = d // LANE
    per = MOE_KC // LANE
    src = h_hbm.at[:, pl.ds(0, ns)]
    rows = lambda s: xbuf.at[s, :, pl.ds(0, ns)]

    @pl.when(i == 0)
    def _():
        _gather_start(src, tok_ref, rows(0), sem.at[0])

    @pl.when(i + 1 < n_used)
    def _():
        _gather_start(src, tok_next_ref, rows(1 - slot), sem.at[1 - slot])

    @pl.when(i < n_used)
    def _():
        _gather_wait(src, rows(slot), sem.at[slot])
        h1 = jnp.zeros(act_ref.shape, F32)
        h3 = jnp.zeros(act_ref.shape, F32)
        for c in range(d // MOE_KC):
            ks = slice(c * MOE_KC, (c + 1) * MOE_KC)
            x = _slab_cols(xbuf.at[slot], c * per, (c + 1) * per).astype(BF16)
            h1 = h1 + _dot(x, w1_ref[ks, :].astype(BF16))
            h3 = h3 + _dot(x, w3_ref[ks, :].astype(BF16))
        act_ref[...] = (h1 * _sigmoid(h1) * h3).astype(act_ref.dtype)

    @pl.when(i >= n_used)
    def _():
        act_ref[...] = jnp.zeros_like(act_ref)


def _moe_down_kernel(be_ref, nb_ref, act_ref, w2_ref, y_ref):
    del be_ref
    i = pl.program_id(0)

    @pl.when(i < nb_ref[0])
    def _():
        act = act_ref[...]
        for c in range(y_ref.shape[1] // MOE_KC):
            ks = slice(c * MOE_KC, (c + 1) * MOE_KC)
            y_ref[:, ks] = _dot(act, w2_ref[:, ks].astype(BF16))

    @pl.when(i >= nb_ref[0])
    def _():
        y_ref[...] = jnp.zeros_like(y_ref)


def _moe_ffn(hs, w1, w3, w2, layer, blk_e, n_used, row_tok):
    r, pitch, _ = hs.shape
    d = w1.shape[2]
    nblk = blk_e.shape[0]
    de = w1.shape[3]
    bm = row_tok.shape[2]
    wspec = lambda a, b: pl.BlockSpec((None, None, a, b), lambda i, be, nb: (layer, be[i], 0, 0))
    up_spec = pltpu.PrefetchScalarGridSpec(
        num_scalar_prefetch=2,
        grid=(nblk,),
        in_specs=[pl.BlockSpec((1, 1, bm), lambda i, be, nb: (i, 0, 0), memory_space=pltpu.SMEM),
                  pl.BlockSpec((1, 1, bm), lambda i, be, nb: (jnp.minimum(i + 1, nblk - 1), 0, 0),
                               memory_space=pltpu.SMEM),
                  pl.BlockSpec(memory_space=pl.ANY), wspec(d, de), wspec(d, de)],
        out_specs=pl.BlockSpec((bm, de), lambda i, be, nb: (i, 0)),
        scratch_shapes=[pltpu.VMEM((2, bm, pitch, LANE), F32), pltpu.SemaphoreType.DMA((2,))],
    )
    act = pl.pallas_call(
        _moe_up_kernel,
        grid_spec=up_spec,
        out_shape=jax.ShapeDtypeStruct((nblk * bm, de), BF16),
        compiler_params=_cparams(("arbitrary",)),
    )(blk_e, n_used, row_tok, row_tok, hs, w1, w3)
    down_spec = pltpu.PrefetchScalarGridSpec(
        num_scalar_prefetch=2,
        grid=(nblk,),
        in_specs=[pl.BlockSpec((bm, de), lambda i, be, nb: (i, 0)), wspec(de, d)],
        out_specs=pl.BlockSpec((bm, d), lambda i, be, nb: (i, 0)),
    )
    return pl.pallas_call(
        _moe_down_kernel,
        grid_spec=down_spec,
        out_shape=jax.ShapeDtypeStruct((nblk * bm, d), F32),
        compiler_params=_cparams(("arbitrary",)),
    )(blk_e, n_used, act, w2)


def _combine_ln_kernel(d0_ref, d1_ref, d0n_ref, d1n_ref, y_hbm, h_ref, gate_ref, g_ref, b_ref, o_ref, ob_ref,
                       ybuf, sem, *, alpha):
    i = pl.program_id(0)
    slot = i % 2

    @pl.when(i == 0)
    def _():
        _gather_start(y_hbm, d0_ref, ybuf.at[0, 0], sem.at[0, 0])
        _gather_start(y_hbm, d1_ref, ybuf.at[0, 1], sem.at[0, 1])

    @pl.when(i + 1 < pl.num_programs(0))
    def _():
        _gather_start(y_hbm, d0n_ref, ybuf.at[1 - slot, 0], sem.at[1 - slot, 0])
        _gather_start(y_hbm, d1n_ref, ybuf.at[1 - slot, 1], sem.at[1 - slot, 1])

    _gather_wait(y_hbm, ybuf.at[slot, 0], sem.at[slot, 0])
    _gather_wait(y_hbm, ybuf.at[slot, 1], sem.at[slot, 1])
    gate = gate_ref[...]
    ff = ybuf[slot, 0] * gate[:, 0:1] + ybuf[slot, 1] * gate[:, 1:2]
    y = _layernorm_rows(alpha * h_ref[...] + ff, g_ref[...], b_ref[...])
    o_ref[...] = y
    ob_ref[...] = y.astype(ob_ref.dtype)


def _combine_ln(yb, h, gates, dest0, dest1, g, b, alpha):
    r, d = h.shape
    tm = dest0.shape[2]
    nblk = r // tm
    kern = functools.partial(_combine_ln_kernel, alpha=alpha)
    idx_spec = pl.BlockSpec((1, 1, tm), lambda i: (i, 0, 0), memory_space=pltpu.SMEM)
    nxt_spec = pl.BlockSpec((1, 1, tm), lambda i: (jnp.minimum(i + 1, nblk - 1), 0, 0), memory_space=pltpu.SMEM)
    return pl.pallas_call(
        kern,
        grid=(nblk,),
        in_specs=[idx_spec, idx_spec, nxt_spec, nxt_spec,
                  pl.BlockSpec(memory_space=pl.ANY),
                  pl.BlockSpec((tm, d), lambda i: (i, 0)),
                  pl.BlockSpec((tm, LANE), lambda i: (i, 0)),
                  pl.BlockSpec((1, d), lambda i: (0, 0)),
                  pl.BlockSpec((1, d), lambda i: (0, 0))],
        out_specs=[pl.BlockSpec((tm, d), lambda i: (i, 0))] * 2,
        out_shape=[jax.ShapeDtypeStruct((r, d), F32), jax.ShapeDtypeStruct((r, d), BF16)],
        scratch_shapes=[pltpu.VMEM((2, 2, tm, d), F32), pltpu.SemaphoreType.DMA((2, 2))],
        compiler_params=_cparams(("arbitrary",)),
    )(dest0, dest1, dest0, dest1, yb, h, gates, g, b)


def _dispatch_plan(eid, n_experts, bm):
    r = eid.shape[0]
    a_tot = r * TOP_K
    e_flat = eid[:, :TOP_K].reshape(a_tot)
    onehot = (e_flat[:, None] == jnp.arange(n_experts, dtype=I32)[None, :]).astype(I32)
    before = jnp.cumsum(onehot, axis=0) - onehot
    rank = jnp.sum(before * onehot, axis=1)
    counts = jnp.sum(onehot, axis=0)
    pcounts = (counts + bm - 1) // bm * bm
    pends = jnp.cumsum(pcounts)
    pstarts = pends - pcounts
    dest = (pstarts[e_flat] + rank).astype(I32)
    nblk = -(-(a_tot + n_experts * (bm - 1)) // bm)
    row_tok = jnp.zeros((nblk * bm,), I32).at[dest].set(jnp.arange(a_tot, dtype=I32) // TOP_K)
    n_used = (pends[-1] // bm).astype(I32)
    blk = jnp.arange(nblk, dtype=I32)
    blk_e = jnp.searchsorted(pends, jnp.minimum(blk, n_used - 1) * bm, side="right").astype(I32)
    blk_e = jnp.minimum(blk_e, n_experts - 1)
    dest2 = dest.reshape(r, TOP_K)
    return row_tok.reshape(nblk, 1, bm), blk_e, n_used.reshape(1), dest2[:, 0], dest2[:, 1]


def kernel(x, meta, w_in, rw_mu, rw_w2, rw_w0, rw_a2, rw_a0, rw_g2, rw_kk, rw_ka, rw_rk, rw_gn_g, rw_gn_b,
           at_qnorm_g, at_w_uq, at_w_iq, at_kidx_g, at_kidx_b, pool_w, pool_scale, w_out, ln1_g, ln1_b,
           router_g_w, router_g_b, router_e_w, router_e_b, exp_w1, exp_w3, exp_w2, ln2_g, ln2_b):
    bsz, seq_len, d = x.shape
    depth = w_in.shape[0]
    n_meta = meta.shape[0]
    t_len = seq_len + n_meta
    p_len = _round_up(t_len, ROW_TILE)
    rows = bsz * p_len
    n_sel = min(TOPK_MAX, seq_len // 4)
    alpha = float((2 * depth) ** 0.25)

    rw = rw_w2.shape[2]
    aw = at_w_uq.shape[2]
    idx_head = at_kidx_g.shape[1]
    idx_heads = at_w_iq.shape[2] // idx_head
    assert idx_head == LANE and aw % AT_HEAD == 0 and rw % LANE == 0
    n_groups = router_g_w.shape[2]
    n_experts = router_e_w.shape[2]
    per_group = n_experts // n_groups
    assert n_groups + n_experts <= LANE
    pool_width = pool_w.shape[1] * pool_w.shape[2]

    src = {"r": rw, "k": rw, "v": rw, "dw": rw_w2.shape[1], "da": rw_a2.shape[1], "dg": rw_g2.shape[1],
           "cq": at_w_uq.shape[1], "ka": aw, "va": aw, "kidx": idx_head, "widx": idx_heads, "pin": pool_width}
    off, pw, n_proj = _layout(src)
    n_proj = _round_up(n_proj, 512)
    rw_names = ("r", "k", "v", "dw", "da", "dg")
    rw_src = {n: src[n] for n in rw_names}
    mu_starts = np.cumsum([0] + [rw_src[n] for n in rw_names])

    pos = jnp.arange(p_len, dtype=F32)
    inv = ROPE_THETA ** (-jnp.arange(0, AT_HEAD, 2, dtype=F32) / AT_HEAD)
    ang = pos[:, None] * inv[None, :]
    cosf = jnp.concatenate([jnp.cos(ang), jnp.cos(ang)], axis=1)
    sinf = jnp.concatenate([-jnp.sin(ang), jnp.sin(ang)], axis=1)

    h = jnp.concatenate([jnp.broadcast_to(meta.astype(x.dtype)[None], (bsz, n_meta, d)), x,
                         jnp.zeros((bsz, p_len - t_len, d), x.dtype)], axis=1).reshape(rows, d)
    hb = h.astype(BF16)
    row1 = lambda a: a.reshape(1, -1)
    pad_rows = lambda a, n: jnp.concatenate([a, jnp.zeros((n - a.shape[0],) + a.shape[1:], a.dtype)], axis=0)
    tm_c = ROW_TILE

    w_in_p = _relayout_cols(w_in, src, off, n_proj)

    for l in range(depth):
        proj = _matmul(hb, w_in_p, l, F32)

        mu = {}
        for i, n in enumerate(rw_names):
            seg = rw_mu[l, int(mu_starts[i]):int(mu_starts[i + 1])]
            mu[n] = row1(jnp.concatenate([seg, jnp.zeros((pw[n] - src[n],), F32)]))
        g2 = pad_rows(rw_g2[l], pw["dg"]).astype(BF16)
        seqs = _rwkv_prep(proj, bsz, p_len, off, pw, mu, rw_w2[l].astype(BF16), row1(rw_w0[l]),
                          rw_a2[l].astype(BF16), row1(rw_a0[l]), g2, row1(rw_kk[l]), row1(rw_ka[l]),
                          row1(rw_rk[l]))
        mix = jnp.zeros((bsz, p_len, rw + aw + pool_width), BF16)
        mix = _rwkv_scan(seqs, row1(rw_gn_g[l]), row1(rw_gn_b[l]), mix)

        q, qi, k_at, v_at, ki, wq = _dsa_prep(proj, bsz, p_len, off, pw, cosf, sinf, row1(at_qnorm_g[l]),
                                              at_w_uq[l].astype(BF16), at_w_iq[l].astype(BF16),
                                              row1(at_kidx_g[l]), row1(at_kidx_b[l]), idx_heads)
        mix = _dsa_attn(q, qi, wq, k_at, v_at, ki, mix, rw, n_sel, idx_heads)
        mix = _pool_mix(proj, off, pw, pool_w[l].astype(BF16), row1(pool_scale[l]), mix, rw + aw)
        h, hb = _outproj_ln(mix, w_out[l].astype(BF16), h, row1(ln1_g[l]), row1(ln1_b[l]), alpha)

        n_r = n_groups + n_experts
        w_r = jnp.concatenate([router_g_w[l], router_e_w[l], jnp.zeros((d, LANE - n_r), F32)], axis=1).astype(BF16)
        b_r = row1(jnp.concatenate([router_g_b[l], router_e_b[l], jnp.zeros((LANE - n_r,), F32)]))
        eid, gates = _router(hb, w_r, b_r, n_groups, per_group)
        row_tok, blk_e, n_used, dest0, dest1 = _dispatch_plan(eid, n_experts, MOE_BM)
        yb = _moe_ffn(_to_slabs(h), exp_w1, exp_w3, exp_w2, l, blk_e, n_used, row_tok)
        h, hb = _combine_ln(yb, h, gates, dest0.reshape(rows // tm_c, 1, tm_c), dest1.reshape(rows // tm_c, 1, tm_c),
                            row1(ln2_g[l]), row1(ln2_b[l]), alpha)

    return h.reshape(bsz, p_len, d)[:, n_meta:t_len]
```

```python
import functools
import math

import jax
import jax.numpy as jnp
import numpy as np
from jax import lax
from jax.experimental import pallas as pl
from jax.experimental.pallas import tpu as pltpu

F32 = jnp.float32
BF16 = jnp.bfloat16
I32 = jnp.int32

LANE = 128
ROW_TILE = 128
RW_HEAD = 64
AT_HEAD = 128
TOPK_MAX = 256
ROPE_THETA = 10000.0
POOL_WINDOWS = (2, 4, 8, 16)
POOL_HALO = 16
GN_EPS = 64e-5
LN_EPS = 1e-5
RW_CHUNK = 64
MOE_BM = 384
MOE_KC = 512
DSA_TQ = 384
DSA_BIG_S = 3072
DSA_GROUPS = 8
DSA_TAIL_TQ = 192
DSA_TAIL_GROUPS = 3
Q_LOGIT_SCALE = AT_HEAD ** -0.5 * math.log2(math.e)
TOP_K = 2
INT_MIN = -(2 ** 31)
VMEM_LIMIT = 56 * 1024 * 1024
VMEM_LIMIT_WIDE = 60 * 1024 * 1024

HIGHEST = lax.Precision.HIGHEST
NN = (((1,), (0,)), ((), ()))
NT = (((1,), (1,)), ((), ()))
TN = (((0,), (0,)), ((), ()))


def _dot(a, b, dims=NN, precision=None):
    return lax.dot_general(a, b, dims, precision=precision, preferred_element_type=F32)


def _sigmoid(x):
    return 1.0 / (1.0 + jnp.exp(-x))


def _cparams(sem, vmem_limit=VMEM_LIMIT):
    return pltpu.CompilerParams(dimension_semantics=sem, vmem_limit_bytes=vmem_limit)


def _round_up(x, m):
    return -(-x // m) * m


def _pick_tile(n, candidates):
    for c in candidates:
        if n % c == 0:
            return c
    return n


def _layout(sizes):
    pw = {k: _round_up(v, LANE) for k, v in sizes.items()}
    order = sorted(sizes, key=lambda k: -pw[k])
    off, gaps, cur = {}, [], 0
    for k in order:
        w = pw[k]
        placed = False
        for gi, (g0, g1) in enumerate(gaps):
            s = _round_up(g0, w)
            if s + w <= g1:
                off[k] = s
                new = [(g0, s), (s + w, g1)]
                gaps[gi:gi + 1] = [g for g in new if g[1] > g[0]]
                placed = True
                break
        if not placed:
            s = _round_up(cur, w)
            if s > cur:
                gaps.append((cur, s))
            off[k] = s
            cur = s + w
    return off, pw, cur


def _relayout_kernel(w_ref, o_ref, *, moves):
    cur = 0
    total, cols = o_ref.shape
    for s0, w, d0 in moves:
        if d0 > cur:
            o_ref[cur:d0, :] = jnp.zeros((d0 - cur, cols), o_ref.dtype)
        end = _round_up(d0 + w, LANE)
        seg = w_ref[s0:s0 + w, :]
        if end > d0 + w:
            seg = jnp.concatenate([seg, jnp.zeros((end - d0 - w, cols), seg.dtype)], axis=0)
        o_ref[d0:end, :] = seg.astype(o_ref.dtype)
        cur = end
    if total > cur:
        o_ref[cur:total, :] = jnp.zeros((total - cur, cols), o_ref.dtype)


def _relayout_rows(w_t, src_sizes, off, total):
    depth, n_src, k = w_t.shape
    names = list(src_sizes)
    starts = np.cumsum([0] + [src_sizes[n] for n in names])
    moves = tuple(sorted(((int(starts[i]), src_sizes[n], off[n]) for i, n in enumerate(names)), key=lambda m: m[2]))
    assert all(s0 % 8 == 0 and w % 8 == 0 for s0, w, _ in moves)
    tc = _pick_tile(k, (256, 128))
    return pl.pallas_call(
        functools.partial(_relayout_kernel, moves=moves),
        grid=(depth, k // tc),
        in_specs=[pl.BlockSpec((None, n_src, tc), lambda l, i: (l, 0, i))],
        out_specs=pl.BlockSpec((None, total, tc), lambda l, i: (l, 0, i)),
        out_shape=jax.ShapeDtypeStruct((depth, total, k), BF16),
        compiler_params=_cparams(("arbitrary", "arbitrary")),
    )(w_t)


def _mm_kernel(x_ref, w_ref, o_ref):
    o_ref[...] = _dot(x_ref[...], w_ref[...], NT).astype(o_ref.dtype)


def _matmul(x, w, layer, out_dtype):
    r, k = x.shape
    n = w.shape[1]
    tm = _pick_tile(r, (768, 512, 384, 256, 128))
    tn = _pick_tile(n, (1024, 512, 256, 128))
    return pl.pallas_call(
        _mm_kernel,
        grid=(n // tn, r // tm),
        in_specs=[pl.BlockSpec((tm, k), lambda j, i: (i, 0)),
                  pl.BlockSpec((None, tn, k), lambda j, i: (layer, j, 0))],
        out_specs=pl.BlockSpec((tm, tn), lambda j, i: (i, j)),
        out_shape=jax.ShapeDtypeStruct((r, n), out_dtype),
        compiler_params=_cparams(("arbitrary", "arbitrary")),
    )(x, w)


def _head_sums(x, hsum):
    tiles = [_dot(x[:, c * LANE:(c + 1) * LANE], hsum, precision=HIGHEST) for c in range(x.shape[1] // LANE)]
    return jnp.concatenate(tiles, axis=1)


def _split3(x):
    hi = x.astype(BF16)
    r1 = x - hi.astype(F32)
    mid = r1.astype(BF16)
    lo = (r1 - mid.astype(F32)).astype(BF16)
    return hi, mid, lo


def _rwkv_prep_kernel(r_ref, k_ref, v_ref, dw_ref, da_ref, dg_ref,
                      rh_ref, kh_ref, vh_ref, dwh_ref, dah_ref, dgh_ref,
                      mur_ref, muk_ref, muv_ref, mudw_ref, muda_ref, mudg_ref,
                      w2_ref, w0_ref, a2_ref, a0_ref, g2_ref, kkp_ref, kap_ref, rk_ref, hsum_ref,
                      tri_ref, blk_ref,
                      at_o, bt_o, kt_o, rt_o, bc_o, kc_o, v_o, epc_o, bv_o, g_o):
    first = pl.program_id(1) == 0

    def shift(x_ref, h_ref, mu_ref):
        x = x_ref[...]
        prev_row = jnp.where(first, 0.0, h_ref[7:8, :])
        row = lax.broadcasted_iota(I32, x.shape, 0)
        prev = jnp.where(row == 0, prev_row, pltpu.roll(x, 1, axis=0))
        return x + (prev - x) * mu_ref[...]

    r = shift(r_ref, rh_ref, mur_ref)
    k = shift(k_ref, kh_ref, muk_ref)
    v = shift(v_ref, vh_ref, muv_ref)
    dw = shift(dw_ref, dwh_ref, mudw_ref)
    da = shift(da_ref, dah_ref, muda_ref)
    dg = shift(dg_ref, dgh_ref, mudg_ref)

    wl = w0_ref[...] + _dot(jnp.tanh(dw).astype(BF16), w2_ref[...])
    neg = -wl
    softplus = jnp.maximum(neg, 0.0) + jnp.log(1.0 + jnp.exp(-jnp.abs(neg)))
    lw = -jnp.exp(-softplus - 0.5)
    a = _sigmoid(a0_ref[...] + _dot(da.astype(BF16), a2_ref[...]))
    g_o[...] = _dot(_sigmoid(dg).astype(BF16), g2_ref[...])
    hsum = hsum_ref[...]
    kk = k * kkp_ref[...]
    kk = kk / jnp.maximum(jnp.sqrt(_head_sums(kk * kk, hsum)), 1e-12)
    k = k * (1.0 + (a - 1.0) * kap_ref[...])
    bv_o[...] = _head_sums(r * k * rk_ref[...], hsum) * v

    parts = _split3(lw)
    tri, blk = tri_ref[...], blk_ref[...]
    cl = _dot(tri, parts[0]) + _dot(tri, parts[1]) + _dot(tri, parts[2])
    clc = _dot(blk, parts[0]) + _dot(blk, parts[1]) + _dot(blk, parts[2])
    e_n = jnp.exp(-cl)
    e_nc = jnp.exp(clc - cl)
    kka = kk * a
    at_o[...] = (-kk * jnp.exp(cl - lw)).astype(at_o.dtype)
    bt_o[...] = (kka * e_n).astype(bt_o.dtype)
    kt_o[...] = (k * e_n).astype(kt_o.dtype)
    rt_o[...] = (r * jnp.exp(cl)).astype(rt_o.dtype)
    bc_o[...] = (kka * e_nc).astype(bc_o.dtype)
    kc_o[...] = (k * e_nc).astype(kc_o.dtype)
    v_o[...] = v.astype(v_o.dtype)
    epc_o[...] = jnp.exp(clc)


def _rwkv_prep(proj, bsz, p_len, off, pw, mu, w2, w0, a2, a0, g2, kkp, kap, rk):
    rw = w2.shape[1]
    tm = ROW_TILE
    nb = p_len // tm
    names = ("r", "k", "v", "dw", "da", "dg")

    def cur_spec(n):
        w, c = pw[n], off[n] // pw[n]
        return pl.BlockSpec((tm, w), lambda b, i, c=c: (b * nb + i, c))

    def halo_spec(n):
        w, c = pw[n], off[n] // pw[n]
        return pl.BlockSpec((8, w), lambda b, i, c=c: (jnp.maximum((b * nb + i) * (tm // 8) - 1, 0), c))

    def full(a):
        return pl.BlockSpec(a.shape, lambda b, i: (0,) * a.ndim)

    lane = np.arange(LANE)
    hsum = jnp.asarray((lane[:, None] // RW_HEAD == lane[None, :] // RW_HEAD).astype(np.float32))
    t = np.arange(tm)
    same = t[:, None] // RW_CHUNK == t[None, :] // RW_CHUNK
    tri = jnp.asarray((same & (t[None, :] <= t[:, None])).astype(np.float32)).astype(BF16)
    blk = jnp.asarray(same.astype(np.float32)).astype(BF16)
    params = [mu[n] for n in names] + [w2, w0, a2, a0, g2, kkp, kap, rk, hsum, tri, blk]
    out_spec = pl.BlockSpec((tm, rw), lambda b, i: (b * nb + i, 0))
    sd = lambda dt: jax.ShapeDtypeStruct((bsz * p_len, rw), dt)
    return pl.pallas_call(
        _rwkv_prep_kernel,
        grid=(bsz, nb),
        in_specs=[cur_spec(n) for n in names] + [halo_spec(n) for n in names] + [full(a) for a in params],
        out_specs=[out_spec] * 10,
        out_shape=[sd(BF16)] * 7 + [sd(F32)] * 3,
        compiler_params=_cparams(("arbitrary", "arbitrary")),
    )(*([proj] * 12), *params)


def _rwkv_heads(at, bt, kt, rt, bc, kc, v, p_row, zt):
    n = len(at)
    hs = range(n)
    c = at[0].shape[0]
    b16 = lambda x: x.astype(BF16)
    row = lax.broadcasted_iota(I32, (2 * c, c), 0)
    col = lax.broadcasted_iota(I32, (2 * c, c), 1)
    keep = col < (row & (c - 1)) + jnp.where(row < c, 0, 1)
    lhs2 = [jnp.concatenate([at[h], rt[h]], axis=0) for h in hs]
    x_b = [jnp.where(keep, _dot(lhs2[h], bt[h], NT), 0.0) for h in hs]
    x_k = [jnp.where(keep, _dot(lhs2[h], kt[h], NT), 0.0) for h in hs]
    lkmv = [_dot(b16(x_k[h]), v[h]) for h in hs]
    l_ba = [x[:c] for x in x_b]
    m_b = [b16(x[c:]) for x in x_b]

    ti = lax.broadcasted_iota(I32, (c, c), 0)
    si = lax.broadcasted_iota(I32, (c, c), 1)
    zero = jnp.zeros((c, c), F32)
    eye = (ti == si).astype(F32)
    same_lo = (ti >> 4) == (si >> 4)
    lp = [jnp.where(same_lo, l, zero) for l in l_ba]
    t = [eye + l for l in lp]
    for _ in range(3):
        lp = [_dot(b16(l), b16(l)) for l in lp]
        t = [t[h] + _dot(b16(lp[h]), b16(t[h])) for h in hs]
    size = 32
    while size <= c:
        shift = size.bit_length() - 1
        same_hi = (ti >> shift) == (si >> shift)
        l_off = [b16(jnp.where(same_hi, jnp.where(same_lo, zero, l), zero)) for l in l_ba]
        tl = [_dot(b16(t[h]), l_off[h]) for h in hs]
        t = [t[h] + _dot(b16(tl[h]), b16(t[h])) for h in hs]
        same_lo = same_hi
        size *= 2

    tb = [b16(x) for x in t]
    a_z = [b16(_dot(tb[h], at[h])) for h in hs]
    w_u = [b16(_dot(tb[h], b16(lkmv[h][:c]))) for h in hs]
    ztb = [b16(z) for z in zt]
    q_z = [b16(rt[h].astype(F32) + _dot(m_b[h], a_z[h])) for h in hs]
    y = [_dot(q_z[h], ztb[h], NT) + _dot(m_b[h], w_u[h]) + lkmv[h][c:] for h in hs]
    g_t = [b16(_dot(a_z[h], bc[h], TN)) for h in hs]
    h_t = [_dot(w_u[h], bc[h], TN) + _dot(v[h], kc[h], TN) for h in hs]
    zt_new = [p_row[h] * zt[h] + _dot(ztb[h], g_t[h]) + h_t[h] for h in hs]
    return y, zt_new


def _rwkv_scan_kernel(at_ref, bt_ref, kt_ref, rt_ref, bc_ref, kc_ref, v_ref, epc_ref, bv_ref, g_ref,
                      gg_ref, gb_ref, mix_ref, o_ref, z_ref):
    del mix_ref
    @pl.when(pl.program_id(2) == 0)
    def _():
        z_ref[...] = jnp.zeros_like(z_ref)

    n_heads = o_ref.shape[1] // RW_HEAD
    sls = [slice(h * RW_HEAD, (h + 1) * RW_HEAD) for h in range(n_heads)]
    cut = lambda ref: [ref[:, sl] for sl in sls]
    y, z_new = _rwkv_heads(cut(at_ref), cut(bt_ref), cut(kt_ref), cut(rt_ref), cut(bc_ref), cut(kc_ref),
                           cut(v_ref), [epc_ref[0:1, sl] for sl in sls], [z_ref[h] for h in range(n_heads)])
    for h in range(n_heads):
        z_ref[h] = z_new[h]
    outs = []
    for h, sl in enumerate(sls):
        mu = jnp.mean(y[h], axis=-1, keepdims=True)
        var = jnp.mean(jnp.square(y[h] - mu), axis=-1, keepdims=True)
        yn = (y[h] - mu) * lax.rsqrt(var + GN_EPS) * gg_ref[:, sl] + gb_ref[:, sl]
        outs.append((yn + bv_ref[:, sl]) * g_ref[:, sl])
    per = LANE // RW_HEAD
    for p in range(n_heads // per):
        o_ref[:, p * LANE:(p + 1) * LANE] = jnp.concatenate(outs[p * per:(p + 1) * per], axis=1).astype(o_ref.dtype)


def _rwkv_scan(seqs, gn_g, gn_b, mix):
    bsz, p_len, _ = mix.shape
    rw = gn_g.shape[1]
    width = _pick_tile(rw, (12 * LANE, 4 * LANE, 3 * LANE, 2 * LANE, LANE))
    nch = p_len // RW_CHUNK
    seqs = [s.reshape(bsz, p_len, rw) for s in seqs]
    seq_spec = pl.BlockSpec((None, RW_CHUNK, width), lambda b, p, c: (b, c, p))
    par_spec = pl.BlockSpec((1, width), lambda b, p, c: (0, p))
    return pl.pallas_call(
        _rwkv_scan_kernel,
        grid=(bsz, rw // width, nch),
        in_specs=[seq_spec] * len(seqs) + [par_spec] * 2 + [pl.BlockSpec(memory_space=pl.ANY)],
        out_specs=seq_spec,
        out_shape=jax.ShapeDtypeStruct(mix.shape, mix.dtype),
        input_output_aliases={len(seqs) + 2: 0},
        scratch_shapes=[pltpu.VMEM((width // RW_HEAD, RW_HEAD, RW_HEAD), F32)],
        compiler_params=_cparams(("arbitrary", "arbitrary", "arbitrary")),
    )(*seqs, gn_g, gn_b, mix)


def _rope_tiles(x, cosf, sinf):
    outs = []
    for c in range(x.shape[1] // LANE):
        xt = x[:, c * LANE:(c + 1) * LANE]
        outs.append(xt * cosf + pltpu.roll(xt, LANE // 2, axis=1) * sinf)
    return outs


def _dsa_prep_kernel(cq_ref, ka_ref, va_ref, kidx_ref, widx_ref, cos_ref, sin_ref, qg_ref, wuq_ref, wiq_ref,
                     kg_ref, kb_ref, q_o, qi_o, k_o, v_o, ki_o, wq_o, *, wq_scale):
    cosf, sinf = cos_ref[...], sin_ref[...]
    v_o[...] = va_ref[...].astype(v_o.dtype)
    cq = cq_ref[...]
    cqn = cq * lax.rsqrt(jnp.mean(jnp.square(cq), axis=-1, keepdims=True) + 1e-6) * qg_ref[...]
    cqb = cqn.astype(BF16)
    for c, t in enumerate(_rope_tiles(_dot(cqb, wuq_ref[...]), cosf, sinf)):
        q_o[:, c * LANE:(c + 1) * LANE] = (t * Q_LOGIT_SCALE).astype(q_o.dtype)
    for c, t in enumerate(_rope_tiles(_dot(cqb, wiq_ref[...]), cosf, sinf)):
        qi_o[:, c * LANE:(c + 1) * LANE] = t.astype(qi_o.dtype)
    for c, t in enumerate(_rope_tiles(ka_ref[...], cosf, sinf)):
        k_o[:, c * LANE:(c + 1) * LANE] = t.astype(k_o.dtype)
    kx = kidx_ref[...]
    mu = jnp.mean(kx, axis=-1, keepdims=True)
    var = jnp.mean(jnp.square(kx - mu), axis=-1, keepdims=True)
    kn = (kx - mu) * lax.rsqrt(var + LN_EPS) * kg_ref[...] + kb_ref[...]
    ki_o[...] = _rope_tiles(kn, cosf, sinf)[0].astype(ki_o.dtype)
    wq_o[...] = widx_ref[...] * wq_scale


def _dsa_prep(proj, bsz, p_len, off, pw, cosf, sinf, qg, wuq, wiq, kg, kb, idx_heads):
    tm = _pick_tile(p_len, (256, 128))
    nb = p_len // tm
    rows = bsz * p_len
    aw, iw = wuq.shape[1], wiq.shape[1]

    def seg(n):
        w, c = pw[n], off[n] // pw[n]
        return pl.BlockSpec((tm, w), lambda i, c=c: (i, c))

    def full(a):
        return pl.BlockSpec(a.shape, lambda i: (0,) * a.ndim)

    pos_spec = pl.BlockSpec((tm, LANE), lambda i: (i % nb, 0))
    row_spec = lambda w: pl.BlockSpec((tm, w), lambda i: (i, 0))
    kern = functools.partial(_dsa_prep_kernel, wq_scale=float(idx_heads ** -0.5 * AT_HEAD ** -0.5))
    return pl.pallas_call(
        kern,
        grid=(rows // tm,),
        in_specs=[seg("cq"), seg("ka"), seg("va"), seg("kidx"), seg("widx"), pos_spec, pos_spec,
                  full(qg), full(wuq), full(wiq), full(kg), full(kb)],
        out_specs=[row_spec(aw), row_spec(iw), row_spec(aw), row_spec(aw), row_spec(LANE), row_spec(LANE)],
        out_shape=[jax.ShapeDtypeStruct((rows, aw), BF16), jax.ShapeDtypeStruct((rows, iw), BF16),
                   jax.ShapeDtypeStruct((rows, aw), BF16), jax.ShapeDtypeStruct((rows, aw), BF16),
                   jax.ShapeDtypeStruct((rows, LANE), BF16), jax.ShapeDtypeStruct((rows, LANE), F32)],
        compiler_params=_cparams(("arbitrary",)),
    )(proj, proj, proj, proj, proj, cosf, sinf, qg, wuq, wiq, kg, kb)


def _topk_mask(sc, causal, n_sel):
    tq, s_len = sc.shape
    bits = lax.bitcast_convert_type(jnp.where(sc == 0.0, 0.0, sc), I32)
    key = jnp.where(bits < 0, bits ^ jnp.int32(0x7FFFFFFF), bits)
    key = jnp.where(causal, key, jnp.int32(INT_MIN))
    one = jnp.ones((tq, s_len), I32)
    zero = jnp.zeros((tq, s_len), I32)

    i16 = jnp.int16
    one16 = jnp.ones((tq, s_len), i16)
    zero16 = jnp.zeros((tq, s_len), i16)

    def count_ge16(x16, th):
        hit = jnp.where(x16 >= th.astype(i16), one16, zero16)
        acc = hit[:, :LANE]
        for j in range(1, s_len // LANE):
            acc = acc + hit[:, j * LANE:(j + 1) * LANE]
        return jnp.sum(acc.astype(I32), axis=1, keepdims=True)

    def kth_largest16(x16):
        t = jnp.where(count_ge16(x16, jnp.zeros((tq, 1), I32)) >= n_sel, jnp.int32(0), jnp.int32(-32768))

        def step(i, t):
            cand = t | jnp.left_shift(jnp.int32(1), 14 - i)
            return jnp.where(count_ge16(x16, cand) >= n_sel, cand, t)

        return lax.fori_loop(0, 15, step, t)

    hi = (key >> 16).astype(i16)
    lo = ((key & jnp.int32(0xFFFF)) - 32768).astype(i16)
    tau_hi = kth_largest16(hi)
    th16 = tau_hi.astype(i16)
    lo_sel = jnp.where(hi == th16, lo, jnp.where(hi > th16, i16(32767), i16(-32768)))
    tau = tau_hi * 65536 + (kth_largest16(lo_sel) + 32768)
    gt = key > tau
    eq = key == tau
    n_gt = jnp.sum(jnp.where(gt, one, zero), axis=1, keepdims=True)
    n_eq = jnp.sum(jnp.where(eq, one, zero), axis=1, keepdims=True)
    need = n_sel - n_gt
    idx = lax.broadcasted_iota(I32, (tq, s_len), 1)
    nbits = max(1, (s_len - 1).bit_length())

    def cut_search():
        def cut_step(i, lo):
            cand = lo | jnp.left_shift(jnp.int32(1), nbits - 1 - i)
            cnt = jnp.sum(jnp.where(eq, jnp.where(idx < cand, one, zero), zero), axis=1, keepdims=True)
            return jnp.where(cnt < need, cand, lo)
        return lax.fori_loop(0, nbits, cut_step, jnp.zeros((tq, 1), I32))

    surplus = jnp.where(tau > jnp.int32(INT_MIN), jnp.where(n_eq > need, 1, 0), 0)
    cut = lax.cond(jnp.max(surplus) > 0, cut_search, lambda: jnp.full((tq, 1), s_len, I32))
    sel = jnp.where(gt, one, jnp.where(eq, jnp.where(idx <= cut, one, zero), zero))
    return jnp.where(causal, sel, zero) > 0


def _dsa_attn_kernel(q_ref, qi_ref, wq_ref, k_ref, v_ref, ki_ref, mix_ref, o_ref, *, n_sel, idx_heads, q_lo):
    del mix_ref
    tq = q_ref.shape[0]
    s_len = k_ref.shape[0]
    t0 = (pl.program_id(1) + q_lo) * tq
    ki = ki_ref[...]
    wq = wq_ref[...]
    sc = jnp.zeros((tq, s_len), F32)
    for h in range(idx_heads):
        s_h = _dot(qi_ref[:, h * LANE:(h + 1) * LANE], ki, NT)
        sc = sc + jnp.maximum(s_h, 0.0) * wq[:, h:h + 1]
    qpos = t0 + lax.broadcasted_iota(I32, (tq, s_len), 0)
    kpos = lax.broadcasted_iota(I32, (tq, s_len), 1)
    mask = _topk_mask(sc, kpos <= qpos, n_sel)
    bias = jnp.where(mask, 0.0, -jnp.inf)
    for h in range(q_ref.shape[1] // AT_HEAD):
        sl = slice(h * AT_HEAD, (h + 1) * AT_HEAD)
        lg = _dot(q_ref[:, sl], k_ref[:, sl], NT) + bias
        p = jnp.exp2(lg - jnp.max(lg, axis=1, keepdims=True))
        den = jnp.sum(p, axis=1, keepdims=True)
        o_ref[:, sl] = (_dot(p.astype(BF16), v_ref[:, sl]) / den).astype(o_ref.dtype)


def _dsa_attn(q, qi, wq, k, v, ki, mix, col, n_sel, idx_heads):
    bsz, p_len, _ = mix.shape
    aw, iw = q.shape[1], qi.shape[1]
    assert col % aw == 0
    big_rows = min(p_len, DSA_BIG_S) // DSA_TQ * DSA_TQ
    calls = []
    for tq, row0, row1, n_groups in ((DSA_TQ, 0, big_rows, DSA_GROUPS),
                                     (_pick_tile(p_len - big_rows, (DSA_TAIL_TQ, ROW_TILE)), big_rows, p_len,
                                      DSA_TAIL_GROUPS)):
        nq = (row1 - row0) // tq
        n_groups = min(n_groups, nq)
        bounds = [row0 // tq + round(nq * g / n_groups) for g in range(n_groups + 1)] if nq else []
        calls += [(tq, lo, hi) for lo, hi in zip(bounds[:-1], bounds[1:])]
    r3 = lambda a: a.reshape(bsz, p_len, a.shape[1])
    args = (r3(q), r3(qi), r3(wq), r3(k), r3(v), r3(ki))
    for tq, lo, hi in calls:
        s_len = hi * tq
        assert s_len >= n_sel
        qspec = lambda w, lo=lo, tq=tq: pl.BlockSpec((None, tq, w), lambda b, i: (b, i + lo, 0))
        kspec = lambda w, s_len=s_len: pl.BlockSpec((None, s_len, w), lambda b, i: (b, 0, 0),
                                                    pipeline_mode=pl.Buffered(1))
        kern = functools.partial(_dsa_attn_kernel, n_sel=n_sel, idx_heads=idx_heads, q_lo=lo)
        mix = pl.pallas_call(
            kern,
            grid=(bsz, hi - lo),
            in_specs=[qspec(aw), qspec(iw), qspec(LANE), kspec(aw), kspec(aw), kspec(LANE),
                      pl.BlockSpec(memory_space=pl.ANY)],
            out_specs=pl.BlockSpec((None, tq, aw), lambda b, i, lo=lo: (b, i + lo, col // aw)),
            out_shape=jax.ShapeDtypeStruct(mix.shape, mix.dtype),
            input_output_aliases={6: 0},
            compiler_params=_cparams(("arbitrary", "arbitrary"),
                                     VMEM_LIMIT_WIDE if tq == DSA_TQ else VMEM_LIMIT),
        )(*args, mix)
    return mix


def _pool_kernel(p_ref, h_ref, w_ref, s_ref, mix_ref, o_ref):
    del mix_ref
    tm, width = p_ref.shape
    grp = width // len(POOL_WINDOWS)
    t0 = pl.program_id(1) * tm
    x = p_ref[...]
    halo = jnp.where(pl.program_id(1) == 0, 0.0, h_ref[...])
    ext = jnp.concatenate([halo, x], axis=0)
    tpos = t0 + lax.broadcasted_iota(I32, (tm, 1), 0)
    acc = ext
    have = 1
    for gi, win in enumerate(POOL_WINDOWS):
        while have < win:
            shifted = jnp.concatenate([jnp.zeros((have, width), F32), acc[:-have]], axis=0)
            acc = acc + shifted
            have *= 2
        sl = slice(gi * grp, (gi + 1) * grp)
        cnt = jnp.minimum(tpos + 1, win).astype(F32)
        pooled = acc[POOL_HALO:, sl] / cnt - x[:, sl]
        y = _dot(pooled.astype(BF16), w_ref[gi])
        o_ref[:, sl] = (y * s_ref[:, sl]).astype(o_ref.dtype)


def _pool_mix(proj, off, pw, w_pool, scale, mix, col):
    bsz, p_len, d_mix = mix.shape
    width = pw["pin"]
    assert col % width == 0
    tm = _pick_tile(p_len, (256, 128))
    nb = p_len // tm
    c = off["pin"] // width
    hb = tm // POOL_HALO
    mix2 = mix.reshape(bsz * p_len, d_mix)
    out = pl.pallas_call(
        _pool_kernel,
        grid=(bsz, nb),
        in_specs=[pl.BlockSpec((tm, width), lambda b, i: (b * nb + i, c)),
                  pl.BlockSpec((POOL_HALO, width), lambda b, i: (jnp.maximum((b * nb + i) * hb - 1, 0), c)),
                  pl.BlockSpec(w_pool.shape, lambda b, i: (0, 0, 0)),
                  pl.BlockSpec(scale.shape, lambda b, i: (0, 0)),
                  pl.BlockSpec(memory_space=pl.ANY)],
        out_specs=pl.BlockSpec((tm, width), lambda b, i: (b * nb + i, col // width)),
        out_shape=jax.ShapeDtypeStruct(mix2.shape, mix2.dtype),
        input_output_aliases={4: 0},
        compiler_params=_cparams(("arbitrary", "arbitrary")),
    )(proj, proj, w_pool, scale, mix2)
    return out


def _layernorm_rows(x, g, b):
    mu = jnp.mean(x, axis=-1, keepdims=True)
    var = jnp.mean(jnp.square(x - mu), axis=-1, keepdims=True)
    return (x - mu) * lax.rsqrt(var + LN_EPS) * g + b


def _outproj_ln_kernel(x_ref, w_ref, h_ref, g_ref, b_ref, o_ref, ob_ref, pre_ref, *, alpha):
    j = pl.program_id(1)
    nj, _, tn = pre_ref.shape
    pre_ref[j] = alpha * h_ref[...] + _dot(x_ref[...], w_ref[...])

    @pl.when(j == nj - 1)
    def _():
        d = nj * tn
        total = pre_ref[0]
        for c in range(1, nj):
            total = total + pre_ref[c]
        mu = jnp.sum(total, axis=-1, keepdims=True) / d
        sq = jnp.square(pre_ref[0] - mu)
        for c in range(1, nj):
            sq = sq + jnp.square(pre_ref[c] - mu)
        inv = lax.rsqrt(jnp.sum(sq, axis=-1, keepdims=True) / d + LN_EPS)
        for c in range(nj):
            cs = slice(c * tn, (c + 1) * tn)
            y = (pre_ref[c] - mu) * inv * g_ref[:, cs] + b_ref[:, cs]
            o_ref[:, cs] = y
            ob_ref[:, cs] = y.astype(ob_ref.dtype)


def _outproj_ln(mix, w_out, h, g, b, alpha):
    r, k = mix.shape
    d = w_out.shape[1]
    tm = _pick_tile(r, (384, 256, 128))
    tn = _pick_tile(d, (512, 256, 128))
    kern = functools.partial(_outproj_ln_kernel, alpha=alpha)
    return pl.pallas_call(
        kern,
        grid=(r // tm, d // tn),
        in_specs=[pl.BlockSpec((tm, k), lambda i, j: (i, 0)),
                  pl.BlockSpec((k, tn), lambda i, j: (0, j)),
                  pl.BlockSpec((tm, tn), lambda i, j: (i, j)),
                  pl.BlockSpec((1, d), lambda i, j: (0, 0)),
                  pl.BlockSpec((1, d), lambda i, j: (0, 0))],
        out_specs=[pl.BlockSpec((tm, d), lambda i, j: (i, 0))] * 2,
        out_shape=[jax.ShapeDtypeStruct((r, d), F32), jax.ShapeDtypeStruct((r, d), BF16)],
        scratch_shapes=[pltpu.VMEM((d // tn, tm, tn), F32)],
        compiler_params=_cparams(("arbitrary", "arbitrary")),
    )(mix, w_out, h, g, b)


def _router_kernel(x_ref, w_ref, b_ref, e_o, g_o, *, n_groups, per_group):
    logits = _dot(x_ref[...], w_ref[...]) + b_ref[...]
    lane = lax.broadcasted_iota(I32, logits.shape, 1)
    big = jnp.int32(LANE)
    ninf = -jnp.inf
    gl = jnp.where(lane < n_groups, logits, ninf)
    ge = jnp.exp(gl - jnp.max(gl, axis=1, keepdims=True))
    gp = ge / jnp.sum(ge, axis=1, keepdims=True)
    g_val = jnp.max(gp, axis=1, keepdims=True)
    g_idx = jnp.min(jnp.where(gp == g_val, lane, big), axis=1, keepdims=True)
    lane_grp = jnp.where(lane >= n_groups, (lane - n_groups) // per_group, -1)
    in_grp = lane_grp == g_idx
    el = jnp.where(in_grp, logits, ninf)
    ee = jnp.exp(el - jnp.max(el, axis=1, keepdims=True))
    ep = jnp.where(in_grp, ee / jnp.sum(ee, axis=1, keepdims=True), -1.0)
    v1 = jnp.max(ep, axis=1, keepdims=True)
    i1 = jnp.min(jnp.where(ep == v1, lane, big), axis=1, keepdims=True)
    ep2 = jnp.where(lane == i1, -1.0, ep)
    v2 = jnp.max(ep2, axis=1, keepdims=True)
    i2 = jnp.min(jnp.where(ep2 == v2, lane, big), axis=1, keepdims=True)
    tot = v1 + v2
    e_o[...] = jnp.where(lane == 0, i1 - n_groups, jnp.where(lane == 1, i2 - n_groups, 0))
    g_o[...] = jnp.where(lane == 0, g_val * v1 / tot, jnp.where(lane == 1, g_val * v2 / tot, 0.0))


def _router(hb, w, b, n_groups, per_group):
    r, d = hb.shape
    tm = _pick_tile(r, (256, 128))
    kern = functools.partial(_router_kernel, n_groups=n_groups, per_group=per_group)
    return pl.pallas_call(
        kern,
        grid=(r // tm,),
        in_specs=[pl.BlockSpec((tm, d), lambda i: (i, 0)),
                  pl.BlockSpec((d, LANE), lambda i: (0, 0)),
                  pl.BlockSpec((1, LANE), lambda i: (0, 0))],
        out_specs=[pl.BlockSpec((tm, LANE), lambda i: (i, 0))] * 2,
        out_shape=[jax.ShapeDtypeStruct((r, LANE), I32), jax.ShapeDtypeStruct((r, LANE), F32)],
        compiler_params=_cparams(("arbitrary",)),
    )(hb, w, b)


def _slab_pitch(ns):
    tiles = -(-ns // 8)
    return 8 * (tiles + 1 - tiles % 2)


def _to_slabs_kernel(x_ref, o_ref):
    tm, pitch, _ = o_ref.shape
    ns = x_ref.shape[1] // LANE
    for c in range(ns):
        o_ref[:, c, :] = x_ref[:, c * LANE:(c + 1) * LANE]
    if pitch > ns:
        o_ref[:, ns:, :] = jnp.zeros((tm, pitch - ns, LANE), o_ref.dtype)


def _to_slabs(x):
    r, d = x.shape
    pitch = _slab_pitch(d // LANE)
    tm = _pick_tile(r, (256, 128))
    return pl.pallas_call(
        _to_slabs_kernel,
        grid=(r // tm,),
        in_specs=[pl.BlockSpec((tm, d), lambda i: (i, 0))],
        out_specs=pl.BlockSpec((tm, pitch, LANE), lambda i: (i, 0, 0)),
        out_shape=jax.ShapeDtypeStruct((r, pitch, LANE), x.dtype),
        compiler_params=_cparams(("arbitrary",)),
    )(x)


def _gather_start(src_hbm, idx_ref, dst, sem):
    def issue(j, carry):
        pltpu.make_async_copy(src_hbm.at[pl.ds(idx_ref[0, 0, j], 1)], dst.at[pl.ds(j, 1)], sem).start()
        return carry
    lax.fori_loop(0, dst.shape[0], issue, 0)


def _gather_wait(src_hbm, dst, sem):
    pltpu.make_async_copy(src_hbm.at[pl.ds(0, dst.shape[0])], dst, sem).wait()


def _slab_cols(ref3, lo, hi):
    return jnp.concatenate([ref3[:, c, :] for c in range(lo, hi)], axis=1)


def _moe_up_kernel(be_ref, nb_ref, tok_ref, tok_next_ref, h_hbm, w1_ref, w3_ref, act_ref, xbuf, sem):
    del be_ref
    i = pl.program_id(0)
    n_used = nb_ref[0]
    slot = i % 2
    d = w1_ref.shape[0]
    ns = d // LANE
    per = MOE_KC // LANE
    src = h_hbm.at[:, pl.ds(0, ns)]
    rows = lambda s: xbuf.at[s, :, pl.ds(0, ns)]

    @pl.when(i == 0)
    def _():
        _gather_start(src, tok_ref, rows(0), sem.at[0])

    @pl.when(i + 1 < n_used)
    def _():
        _gather_start(src, tok_next_ref, rows(1 - slot), sem.at[1 - slot])

    @pl.when(i < n_used)
    def _():
        _gather_wait(src, rows(slot), sem.at[slot])
        h1 = jnp.zeros(act_ref.shape, F32)
        h3 = jnp.zeros(act_ref.shape, F32)
        for c in range(d // MOE_KC):
            ks = slice(c * MOE_KC, (c + 1) * MOE_KC)
            x = _slab_cols(xbuf.at[slot], c * per, (c + 1) * per).astype(BF16)
            h1 = h1 + _dot(x, w1_ref[ks, :].astype(BF16))
            h3 = h3 + _dot(x, w3_ref[ks, :].astype(BF16))
        act_ref[...] = (h1 * _sigmoid(h1) * h3).astype(act_ref.dtype)

    @pl.when(i >= n_used)
    def _():
        act_ref[...] = jnp.zeros_like(act_ref)


def _moe_down_kernel(be_ref, nb_ref, act_ref, w2_ref, y_ref):
    del be_ref
    i = pl.program_id(0)

    @pl.when(i < nb_ref[0])
    def _():
        act = act_ref[...]
        for c in range(y_ref.shape[1] // MOE_KC):
            ks = slice(c * MOE_KC, (c + 1) * MOE_KC)
            y_ref[:, ks] = _dot(act, w2_ref[:, ks].astype(BF16))

    @pl.when(i >= nb_ref[0])
    def _():
        y_ref[...] = jnp.zeros_like(y_ref)


def _moe_ffn(hs, w1, w3, w2, layer, blk_e, n_used, row_tok):
    r, pitch, _ = hs.shape
    d = w1.shape[2]
    nblk = blk_e.shape[0]
    de = w1.shape[3]
    bm = row_tok.shape[2]
    wspec = lambda a, b: pl.BlockSpec((None, None, a, b), lambda i, be, nb: (layer, be[i], 0, 0))
    up_spec = pltpu.PrefetchScalarGridSpec(
        num_scalar_prefetch=2,
        grid=(nblk,),
        in_specs=[pl.BlockSpec((1, 1, bm), lambda i, be, nb: (i, 0, 0), memory_space=pltpu.SMEM),
                  pl.BlockSpec((1, 1, bm), lambda i, be, nb: (jnp.minimum(i + 1, nblk - 1), 0, 0),
                               memory_space=pltpu.SMEM),
                  pl.BlockSpec(memory_space=pl.ANY), wspec(d, de), wspec(d, de)],
        out_specs=pl.BlockSpec((bm, de), lambda i, be, nb: (i, 0)),
        scratch_shapes=[pltpu.VMEM((2, bm, pitch, LANE), F32), pltpu.SemaphoreType.DMA((2,))],
    )
    act = pl.pallas_call(
        _moe_up_kernel,
        grid_spec=up_spec,
        out_shape=jax.ShapeDtypeStruct((nblk * bm, de), BF16),
        compiler_params=_cparams(("arbitrary",)),
    )(blk_e, n_used, row_tok, row_tok, hs, w1, w3)
    down_spec = pltpu.PrefetchScalarGridSpec(
        num_scalar_prefetch=2,
        grid=(nblk,),
        in_specs=[pl.BlockSpec((bm, de), lambda i, be, nb: (i, 0)), wspec(de, d)],
        out_specs=pl.BlockSpec((bm, d), lambda i, be, nb: (i, 0)),
    )
    return pl.pallas_call(
        _moe_down_kernel,
        grid_spec=down_spec,
        out_shape=jax.ShapeDtypeStruct((nblk * bm, d), F32),
        compiler_params=_cparams(("arbitrary",)),
    )(blk_e, n_used, act, w2)


def _combine_ln_kernel(d0_ref, d1_ref, d0n_ref, d1n_ref, y_hbm, h_ref, gate_ref, g_ref, b_ref, o_ref, ob_ref,
                       ybuf, sem, *, alpha):
    i = pl.program_id(0)
    slot = i % 2

    @pl.when(i == 0)
    def _():
        _gather_start(y_hbm, d0_ref, ybuf.at[0, 0], sem.at[0, 0])
        _gather_start(y_hbm, d1_ref, ybuf.at[0, 1], sem.at[0, 1])

    @pl.when(i + 1 < pl.num_programs(0))
    def _():
        _gather_start(y_hbm, d0n_ref, ybuf.at[1 - slot, 0], sem.at[1 - slot, 0])
        _gather_start(y_hbm, d1n_ref, ybuf.at[1 - slot, 1], sem.at[1 - slot, 1])

    _gather_wait(y_hbm, ybuf.at[slot, 0], sem.at[slot, 0])
    _gather_wait(y_hbm, ybuf.at[slot, 1], sem.at[slot, 1])
    gate = gate_ref[...]
    ff = ybuf[slot, 0] * gate[:, 0:1] + ybuf[slot, 1] * gate[:, 1:2]
    y = _layernorm_rows(alpha * h_ref[...] + ff, g_ref[...], b_ref[...])
    o_ref[...] = y
    ob_ref[...] = y.astype(ob_ref.dtype)


def _combine_ln(yb, h, gates, dest0, dest1, g, b, alpha):
    r, d = h.shape
    tm = dest0.shape[2]
    nblk = r // tm
    kern = functools.partial(_combine_ln_kernel, alpha=alpha)
    idx_spec = pl.BlockSpec((1, 1, tm), lambda i: (i, 0, 0), memory_space=pltpu.SMEM)
    nxt_spec = pl.BlockSpec((1, 1, tm), lambda i: (jnp.minimum(i + 1, nblk - 1), 0, 0), memory_space=pltpu.SMEM)
    return pl.pallas_call(
        kern,
        grid=(nblk,),
        in_specs=[idx_spec, idx_spec, nxt_spec, nxt_spec,
                  pl.BlockSpec(memory_space=pl.ANY),
                  pl.BlockSpec((tm, d), lambda i: (i, 0)),
                  pl.BlockSpec((tm, LANE), lambda i: (i, 0)),
                  pl.BlockSpec((1, d), lambda i: (0, 0)),
                  pl.BlockSpec((1, d), lambda i: (0, 0))],
        out_specs=[pl.BlockSpec((tm, d), lambda i: (i, 0))] * 2,
        out_shape=[jax.ShapeDtypeStruct((r, d), F32), jax.ShapeDtypeStruct((r, d), BF16)],
        scratch_shapes=[pltpu.VMEM((2, 2, tm, d), F32), pltpu.SemaphoreType.DMA((2, 2))],
        compiler_params=_cparams(("arbitrary",)),
    )(dest0, dest1, dest0, dest1, yb, h, gates, g, b)


def _dispatch_plan(eid, n_experts, bm):
    r = eid.shape[0]
    a_tot = r * TOP_K
    e_flat = eid[:, :TOP_K].reshape(a_tot)
    onehot = (e_flat[:, None] == jnp.arange(n_experts, dtype=I32)[None, :]).astype(I32)
    before = jnp.cumsum(onehot, axis=0) - onehot
    rank = jnp.sum(before * onehot, axis=1)
    counts = jnp.sum(onehot, axis=0)
    pcounts = (counts + bm - 1) // bm * bm
    pends = jnp.cumsum(pcounts)
    pstarts = pends - pcounts
    dest = (pstarts[e_flat] + rank).astype(I32)
    nblk = -(-(a_tot + n_experts * (bm - 1)) // bm)
    row_tok = jnp.zeros((nblk * bm,), I32).at[dest].set(jnp.arange(a_tot, dtype=I32) // TOP_K)
    n_used = (pends[-1] // bm).astype(I32)
    blk = jnp.arange(nblk, dtype=I32)
    blk_e = jnp.searchsorted(pends, jnp.minimum(blk, n_used - 1) * bm, side="right").astype(I32)
    blk_e = jnp.minimum(blk_e, n_experts - 1)
    dest2 = dest.reshape(r, TOP_K)
    return row_tok.reshape(nblk, 1, bm), blk_e, n_used.reshape(1), dest2[:, 0], dest2[:, 1]


def kernel(x, meta, w_in, rw_mu, rw_w2, rw_w0, rw_a2, rw_a0, rw_g2, rw_kk, rw_ka, rw_rk, rw_gn_g, rw_gn_b,
           at_qnorm_g, at_w_uq, at_w_iq, at_kidx_g, at_kidx_b, pool_w, pool_scale, w_out, ln1_g, ln1_b,
           router_g_w, router_g_b, router_e_w, router_e_b, exp_w1, exp_w3, exp_w2, ln2_g, ln2_b):
    bsz, seq_len, d = x.shape
    depth = w_in.shape[0]
    n_meta = meta.shape[0]
    t_len = seq_len + n_meta
    p_len = _round_up(t_len, ROW_TILE)
    rows = bsz * p_len
    n_sel = min(TOPK_MAX, seq_len // 4)
    alpha = float((2 * depth) ** 0.25)

    rw = rw_w2.shape[2]
    aw = at_w_uq.shape[2]
    idx_head = at_kidx_g.shape[1]
    idx_heads = at_w_iq.shape[2] // idx_head
    assert idx_head == LANE and aw % AT_HEAD == 0 and rw % LANE == 0
    n_groups = router_g_w.shape[2]
    n_experts = router_e_w.shape[2]
    per_group = n_experts // n_groups
    assert n_groups + n_experts <= LANE
    pool_width = pool_w.shape[1] * pool_w.shape[2]

    src = {"r": rw, "k": rw, "v": rw, "dw": rw_w2.shape[1], "da": rw_a2.shape[1], "dg": rw_g2.shape[1],
           "cq": at_w_uq.shape[1], "ka": aw, "va": aw, "kidx": idx_head, "widx": idx_heads, "pin": pool_width}
    off, pw, n_proj = _layout(src)
    n_proj = _round_up(n_proj, 512)
    rw_names = ("r", "k", "v", "dw", "da", "dg")
    rw_src = {n: src[n] for n in rw_names}
    mu_starts = np.cumsum([0] + [rw_src[n] for n in rw_names])

    pos = jnp.arange(p_len, dtype=F32)
    inv = ROPE_THETA ** (-jnp.arange(0, AT_HEAD, 2, dtype=F32) / AT_HEAD)
    ang = pos[:, None] * inv[None, :]
    cosf = jnp.concatenate([jnp.cos(ang), jnp.cos(ang)], axis=1)
    sinf = jnp.concatenate([-jnp.sin(ang), jnp.sin(ang)], axis=1)

    h = jnp.concatenate([jnp.broadcast_to(meta.astype(x.dtype)[None], (bsz, n_meta, d)), x,
                         jnp.zeros((bsz, p_len - t_len, d), x.dtype)], axis=1).reshape(rows, d)
    hb = h.astype(BF16)
    row1 = lambda a: a.reshape(1, -1)
    pad_rows = lambda a, n: jnp.concatenate([a, jnp.zeros((n - a.shape[0],) + a.shape[1:], a.dtype)], axis=0)
    tm_c = ROW_TILE

    w_in_p = _relayout_rows(jnp.swapaxes(w_in, 1, 2), src, off, n_proj)

    for l in range(depth):
        proj = _matmul(hb, w_in_p, l, F32)

        mu = {}
        for i, n in enumerate(rw_names):
            seg = rw_mu[l, int(mu_starts[i]):int(mu_starts[i + 1])]
            mu[n] = row1(jnp.concatenate([seg, jnp.zeros((pw[n] - src[n],), F32)]))
        g2 = pad_rows(rw_g2[l], pw["dg"]).astype(BF16)
        seqs = _rwkv_prep(proj, bsz, p_len, off, pw, mu, rw_w2[l].astype(BF16), row1(rw_w0[l]),
                          rw_a2[l].astype(BF16), row1(rw_a0[l]), g2, row1(rw_kk[l]), row1(rw_ka[l]),
                          row1(rw_rk[l]))
        mix = jnp.zeros((bsz, p_len, rw + aw + pool_width), BF16)
        mix = _rwkv_scan(seqs, row1(rw_gn_g[l]), row1(rw_gn_b[l]), mix)

        q, qi, k_at, v_at, ki, wq = _dsa_prep(proj, bsz, p_len, off, pw, cosf, sinf, row1(at_qnorm_g[l]),
                                              at_w_uq[l].astype(BF16), at_w_iq[l].astype(BF16),
                                              row1(at_kidx_g[l]), row1(at_kidx_b[l]), idx_heads)
        mix = _dsa_attn(q, qi, wq, k_at, v_at, ki, mix, rw, n_sel, idx_heads)
        mix = _pool_mix(proj, off, pw, pool_w[l].astype(BF16), row1(pool_scale[l]), mix, rw + aw)
        h, hb = _outproj_ln(mix, w_out[l].astype(BF16), h, row1(ln1_g[l]), row1(ln1_b[l]), alpha)

        n_r = n_groups + n_experts
        w_r = jnp.concatenate([router_g_w[l], router_e_w[l], jnp.zeros((d, LANE - n_r), F32)], axis=1).astype(BF16)
        b_r = row1(jnp.concatenate([router_g_b[l], router_e_b[l], jnp.zeros((LANE - n_r,), F32)]))
        eid, gates = _router(hb, w_r, b_r, n_groups, per_group)
        row_tok, blk_e, n_used, dest0, dest1 = _dispatch_plan(eid, n_experts, MOE_BM)
        yb = _moe_ffn(_to_slabs(h), exp_w1, exp_w3, exp_w2, l, blk_e, n_used, row_tok)
        h, hb = _combine_ln(yb, h, gates, dest0.reshape(rows // tm_c, 1, tm_c), dest1.reshape(rows // tm_c, 1, tm_c),
                            row1(ln2_g[l]), row1(ln2_b[l]), alpha)

    return h.reshape(bsz, p_len, d)[:, n_meta:t_len]
```

```python
import functools
import math

import jax
import jax.numpy as jnp
import numpy as np
from jax import lax
from jax.experimental import pallas as pl
from jax.experimental.pallas import tpu as pltpu

F32 = jnp.float32
BF16 = jnp.bfloat16
I32 = jnp.int32

LANE = 128
ROW_TILE = 128
RW_HEAD = 64
AT_HEAD = 128
TOPK_MAX = 256
ROPE_THETA = 10000.0
POOL_WINDOWS = (2, 4, 8, 16)
POOL_HALO = 16
GN_EPS = 64e-5
LN_EPS = 1e-5
RW_CHUNK = 64
MOE_BM = 384
MOE_KC = 512
DSA_TQ = 384
DSA_BIG_S = 3072
DSA_GROUPS = 8
DSA_TAIL_TQ = 192
DSA_TAIL_GROUPS = 3
Q_LOGIT_SCALE = AT_HEAD ** -0.5 * math.log2(math.e)
TOP_K = 2
INT_MIN = -(2 ** 31)
VMEM_LIMIT = 56 * 1024 * 1024
VMEM_LIMIT_WIDE = 60 * 1024 * 1024

HIGHEST = lax.Precision.HIGHEST
NN = (((1,), (0,)), ((), ()))
NT = (((1,), (1,)), ((), ()))
TN = (((0,), (0,)), ((), ()))


def _dot(a, b, dims=NN, precision=None):
    return lax.dot_general(a, b, dims, precision=precision, preferred_element_type=F32)


def _sigmoid(x):
    return 1.0 / (1.0 + jnp.exp(-x))


def _cparams(sem, vmem_limit=VMEM_LIMIT):
    return pltpu.CompilerParams(dimension_semantics=sem, vmem_limit_bytes=vmem_limit)


def _round_up(x, m):
    return -(-x // m) * m


def _pick_tile(n, candidates):
    for c in candidates:
        if n % c == 0:
            return c
    return n


def _layout(sizes):
    pw = {k: _round_up(v, LANE) for k, v in sizes.items()}
    order = sorted(sizes, key=lambda k: -pw[k])
    off, gaps, cur = {}, [], 0
    for k in order:
        w = pw[k]
        placed = False
        for gi, (g0, g1) in enumerate(gaps):
            s = _round_up(g0, w)
            if s + w <= g1:
                off[k] = s
                new = [(g0, s), (s + w, g1)]
                gaps[gi:gi + 1] = [g for g in new if g[1] > g[0]]
                placed = True
                break
        if not placed:
            s = _round_up(cur, w)
            if s > cur:
                gaps.append((cur, s))
            off[k] = s
            cur = s + w
    return off, pw, cur


def _relayout_kernel(w_ref, o_ref, *, moves):
    cur = 0
    total, cols = o_ref.shape
    for s0, w, d0 in moves:
        if d0 > cur:
            o_ref[cur:d0, :] = jnp.zeros((d0 - cur, cols), o_ref.dtype)
        end = _round_up(d0 + w, LANE)
        seg = w_ref[s0:s0 + w, :]
        if end > d0 + w:
            seg = jnp.concatenate([seg, jnp.zeros((end - d0 - w, cols), seg.dtype)], axis=0)
        o_ref[d0:end, :] = seg.astype(o_ref.dtype)
        cur = end
    if total > cur:
        o_ref[cur:total, :] = jnp.zeros((total - cur, cols), o_ref.dtype)


def _relayout_rows(w_t, src_sizes, off, total):
    depth, n_src, k = w_t.shape
    names = list(src_sizes)
    starts = np.cumsum([0] + [src_sizes[n] for n in names])
    moves = tuple(sorted(((int(starts[i]), src_sizes[n], off[n]) for i, n in enumerate(names)), key=lambda m: m[2]))
    assert all(s0 % 8 == 0 and w % 8 == 0 for s0, w, _ in moves)
    tc = _pick_tile(k, (256, 128))
    return pl.pallas_call(
        functools.partial(_relayout_kernel, moves=moves),
        grid=(depth, k // tc),
        in_specs=[pl.BlockSpec((None, n_src, tc), lambda l, i: (l, 0, i))],
        out_specs=pl.BlockSpec((None, total, tc), lambda l, i: (l, 0, i)),
        out_shape=jax.ShapeDtypeStruct((depth, total, k), BF16),
        compiler_params=_cparams(("arbitrary", "arbitrary")),
    )(w_t)


def _mm_kernel(x_ref, w_ref, o_ref):
    o_ref[...] = _dot(x_ref[...], w_ref[...], NT).astype(o_ref.dtype)


def _matmul(x, w, layer, out_dtype):
    r, k = x.shape
    n = w.shape[1]
    tm = _pick_tile(r, (768, 512, 384, 256, 128))
    tn = _pick_tile(n, (1024, 512, 256, 128))
    return pl.pallas_call(
        _mm_kernel,
        grid=(n // tn, r // tm),
        in_specs=[pl.BlockSpec((tm, k), lambda j, i: (i, 0)),
                  pl.BlockSpec((None, tn, k), lambda j, i: (layer, j, 0))],
        out_specs=pl.BlockSpec((tm, tn), lambda j, i: (i, j)),
        out_shape=jax.ShapeDtypeStruct((r, n), out_dtype),
        compiler_params=_cparams(("arbitrary", "arbitrary")),
    )(x, w)


def _head_sums(x, hsum):
    tiles = [_dot(x[:, c * LANE:(c + 1) * LANE], hsum, precision=HIGHEST) for c in range(x.shape[1] // LANE)]
    return jnp.concatenate(tiles, axis=1)


def _split3(x):
    hi = x.astype(BF16)
    r1 = x - hi.astype(F32)
    mid = r1.astype(BF16)
    lo = (r1 - mid.astype(F32)).astype(BF16)
    return hi, mid, lo


def _rwkv_prep_kernel(r_ref, k_ref, v_ref, dw_ref, da_ref, dg_ref,
                      rh_ref, kh_ref, vh_ref, dwh_ref, dah_ref, dgh_ref,
                      mur_ref, muk_ref, muv_ref, mudw_ref, muda_ref, mudg_ref,
                      w2_ref, w0_ref, a2_ref, a0_ref, g2_ref, kkp_ref, kap_ref, rk_ref, hsum_ref,
                      tri_ref, blk_ref,
                      at_o, bt_o, kt_o, rt_o, bc_o, kc_o, v_o, epc_o, bv_o, g_o):
    first = pl.program_id(1) == 0

    def shift(x_ref, h_ref, mu_ref):
        x = x_ref[...]
        prev_row = jnp.where(first, 0.0, h_ref[7:8, :])
        row = lax.broadcasted_iota(I32, x.shape, 0)
        prev = jnp.where(row == 0, prev_row, pltpu.roll(x, 1, axis=0))
        return x + (prev - x) * mu_ref[...]

    r = shift(r_ref, rh_ref, mur_ref)
    k = shift(k_ref, kh_ref, muk_ref)
    v = shift(v_ref, vh_ref, muv_ref)
    dw = shift(dw_ref, dwh_ref, mudw_ref)
    da = shift(da_ref, dah_ref, muda_ref)
    dg = shift(dg_ref, dgh_ref, mudg_ref)

    wl = w0_ref[...] + _dot(jnp.tanh(dw).astype(BF16), w2_ref[...])
    neg = -wl
    softplus = jnp.maximum(neg, 0.0) + jnp.log(1.0 + jnp.exp(-jnp.abs(neg)))
    lw = -jnp.exp(-softplus - 0.5)
    a = _sigmoid(a0_ref[...] + _dot(da.astype(BF16), a2_ref[...]))
    g_o[...] = _dot(_sigmoid(dg).astype(BF16), g2_ref[...])
    hsum = hsum_ref[...]
    kk = k * kkp_ref[...]
    kk = kk / jnp.maximum(jnp.sqrt(_head_sums(kk * kk, hsum)), 1e-12)
    k = k * (1.0 + (a - 1.0) * kap_ref[...])
    bv_o[...] = _head_sums(r * k * rk_ref[...], hsum) * v

    parts = _split3(lw)
    tri, blk = tri_ref[...], blk_ref[...]
    cl = _dot(tri, parts[0]) + _dot(tri, parts[1]) + _dot(tri, parts[2])
    clc = _dot(blk, parts[0]) + _dot(blk, parts[1]) + _dot(blk, parts[2])
    e_n = jnp.exp(-cl)
    e_nc = jnp.exp(clc - cl)
    kka = kk * a
    at_o[...] = (-kk * jnp.exp(cl - lw)).astype(at_o.dtype)
    bt_o[...] = (kka * e_n).astype(bt_o.dtype)
    kt_o[...] = (k * e_n).astype(kt_o.dtype)
    rt_o[...] = (r * jnp.exp(cl)).astype(rt_o.dtype)
    bc_o[...] = (kka * e_nc).astype(bc_o.dtype)
    kc_o[...] = (k * e_nc).astype(kc_o.dtype)
    v_o[...] = v.astype(v_o.dtype)
    epc_o[...] = jnp.exp(clc)


def _rwkv_prep(proj, bsz, p_len, off, pw, mu, w2, w0, a2, a0, g2, kkp, kap, rk):
    rw = w2.shape[1]
    tm = ROW_TILE
    nb = p_len // tm
    names = ("r", "k", "v", "dw", "da", "dg")

    def cur_spec(n):
        w, c = pw[n], off[n] // pw[n]
        return pl.BlockSpec((tm, w), lambda b, i, c=c: (b * nb + i, c))

    def halo_spec(n):
        w, c = pw[n], off[n] // pw[n]
        return pl.BlockSpec((8, w), lambda b, i, c=c: (jnp.maximum((b * nb + i) * (tm // 8) - 1, 0), c))

    def full(a):
        return pl.BlockSpec(a.shape, lambda b, i: (0,) * a.ndim)

    lane = np.arange(LANE)
    hsum = jnp.asarray((lane[:, None] // RW_HEAD == lane[None, :] // RW_HEAD).astype(np.float32))
    t = np.arange(tm)
    same = t[:, None] // RW_CHUNK == t[None, :] // RW_CHUNK
    tri = jnp.asarray((same & (t[None, :] <= t[:, None])).astype(np.float32)).astype(BF16)
    blk = jnp.asarray(same.astype(np.float32)).astype(BF16)
    params = [mu[n] for n in names] + [w2, w0, a2, a0, g2, kkp, kap, rk, hsum, tri, blk]
    out_spec = pl.BlockSpec((tm, rw), lambda b, i: (b * nb + i, 0))
    sd = lambda dt: jax.ShapeDtypeStruct((bsz * p_len, rw), dt)
    return pl.pallas_call(
        _rwkv_prep_kernel,
        grid=(bsz, nb),
        in_specs=[cur_spec(n) for n in names] + [halo_spec(n) for n in names] + [full(a) for a in params],
        out_specs=[out_spec] * 10,
        out_shape=[sd(BF16)] * 7 + [sd(F32)] * 3,
        compiler_params=_cparams(("arbitrary", "arbitrary")),
    )(*([proj] * 12), *params)


def _rwkv_heads(at, bt, kt, rt, bc, kc, v, p_row, zt):
    n = len(at)
    hs = range(n)
    c = at[0].shape[0]
    b16 = lambda x: x.astype(BF16)
    row = lax.broadcasted_iota(I32, (2 * c, c), 0)
    col = lax.broadcasted_iota(I32, (2 * c, c), 1)
    keep = col < (row & (c - 1)) + jnp.where(row < c, 0, 1)
    lhs2 = [jnp.concatenate([at[h], rt[h]], axis=0) for h in hs]
    x_b = [jnp.where(keep, _dot(lhs2[h], bt[h], NT), 0.0) for h in hs]
    x_k = [jnp.where(keep, _dot(lhs2[h], kt[h], NT), 0.0) for h in hs]
    lkmv = [_dot(b16(x_k[h]), v[h]) for h in hs]
    l_ba = [x[:c] for x in x_b]
    m_b = [b16(x[c:]) for x in x_b]

    ti = lax.broadcasted_iota(I32, (c, c), 0)
    si = lax.broadcasted_iota(I32, (c, c), 1)
    zero = jnp.zeros((c, c), F32)
    eye = (ti == si).astype(F32)
    same_lo = (ti >> 4) == (si >> 4)
    lp = [jnp.where(same_lo, l, zero) for l in l_ba]
    t = [eye + l for l in lp]
    for _ in range(3):
        lp = [_dot(b16(l), b16(l)) for l in lp]
        t = [t[h] + _dot(b16(lp[h]), b16(t[h])) for h in hs]
    size = 32
    while size <= c:
        shift = size.bit_length() - 1
        same_hi = (ti >> shift) == (si >> shift)
        l_off = [b16(jnp.where(same_hi, jnp.where(same_lo, zero, l), zero)) for l in l_ba]
        tl = [_dot(b16(t[h]), l_off[h]) for h in hs]
        t = [t[h] + _dot(b16(tl[h]), b16(t[h])) for h in hs]
        same_lo = same_hi
        size *= 2

    tb = [b16(x) for x in t]
    a_z = [b16(_dot(tb[h], at[h])) for h in hs]
    w_u = [b16(_dot(tb[h], b16(lkmv[h][:c]))) for h in hs]
    ztb = [b16(z) for z in zt]
    q_z = [b16(rt[h].astype(F32) + _dot(m_b[h], a_z[h])) for h in hs]
    y = [_dot(q_z[h], ztb[h], NT) + _dot(m_b[h], w_u[h]) + lkmv[h][c:] for h in hs]
    g_t = [b16(_dot(a_z[h], bc[h], TN)) for h in hs]
    h_t = [_dot(w_u[h], bc[h], TN) + _dot(v[h], kc[h], TN) for h in hs]
    zt_new = [p_row[h] * zt[h] + _dot(ztb[h], g_t[h]) + h_t[h] for h in hs]
    return y, zt_new


def _rwkv_scan_kernel(at_ref, bt_ref, kt_ref, rt_ref, bc_ref, kc_ref, v_ref, epc_ref, bv_ref, g_ref,
                      gg_ref, gb_ref, mix_ref, o_ref, z_ref):
    del mix_ref
    @pl.when(pl.program_id(2) == 0)
    def _():
        z_ref[...] = jnp.zeros_like(z_ref)

    n_heads = o_ref.shape[1] // RW_HEAD
    sls = [slice(h * RW_HEAD, (h + 1) * RW_HEAD) for h in range(n_heads)]
    cut = lambda ref: [ref[:, sl] for sl in sls]
    y, z_new = _rwkv_heads(cut(at_ref), cut(bt_ref), cut(kt_ref), cut(rt_ref), cut(bc_ref), cut(kc_ref),
                           cut(v_ref), [epc_ref[0:1, sl] for sl in sls], [z_ref[h] for h in range(n_heads)])
    for h in range(n_heads):
        z_ref[h] = z_new[h]
    outs = []
    for h, sl in enumerate(sls):
        mu = jnp.mean(y[h], axis=-1, keepdims=True)
        var = jnp.mean(jnp.square(y[h] - mu), axis=-1, keepdims=True)
        yn = (y[h] - mu) * lax.rsqrt(var + GN_EPS) * gg_ref[:, sl] + gb_ref[:, sl]
        outs.append((yn + bv_ref[:, sl]) * g_ref[:, sl])
    per = LANE // RW_HEAD
    for p in range(n_heads // per):
        o_ref[:, p * LANE:(p + 1) * LANE] = jnp.concatenate(outs[p * per:(p + 1) * per], axis=1).astype(o_ref.dtype)


def _rwkv_scan(seqs, gn_g, gn_b, mix):
    bsz, p_len, _ = mix.shape
    rw = gn_g.shape[1]
    width = _pick_tile(rw, (12 * LANE, 4 * LANE, 3 * LANE, 2 * LANE, LANE))
    nch = p_len // RW_CHUNK
    seqs = [s.reshape(bsz, p_len, rw) for s in seqs]
    seq_spec = pl.BlockSpec((None, RW_CHUNK, width), lambda b, p, c: (b, c, p))
    par_spec = pl.BlockSpec((1, width), lambda b, p, c: (0, p))
    return pl.pallas_call(
        _rwkv_scan_kernel,
        grid=(bsz, rw // width, nch),
        in_specs=[seq_spec] * len(seqs) + [par_spec] * 2 + [pl.BlockSpec(memory_space=pl.ANY)],
        out_specs=seq_spec,
        out_shape=jax.ShapeDtypeStruct(mix.shape, mix.dtype),
        input_output_aliases={len(seqs) + 2: 0},
        scratch_shapes=[pltpu.VMEM((width // RW_HEAD, RW_HEAD, RW_HEAD), F32)],
        compiler_params=_cparams(("arbitrary", "arbitrary", "arbitrary")),
    )(*seqs, gn_g, gn_b, mix)


def _rope_tiles(x, cosf, sinf):
    outs = []
    for c in range(x.shape[1] // LANE):
        xt = x[:, c * LANE:(c + 1) * LANE]
        outs.append(xt * cosf + pltpu.roll(xt, LANE // 2, axis=1) * sinf)
    return outs


def _dsa_prep_kernel(cq_ref, ka_ref, va_ref, kidx_ref, widx_ref, cos_ref, sin_ref, qg_ref, wuq_ref, wiq_ref,
                     kg_ref, kb_ref, q_o, qi_o, k_o, v_o, ki_o, wq_o, *, wq_scale):
    cosf, sinf = cos_ref[...], sin_ref[...]
    v_o[...] = va_ref[...].astype(v_o.dtype)
    cq = cq_ref[...]
    cqn = cq * lax.rsqrt(jnp.mean(jnp.square(cq), axis=-1, keepdims=True) + 1e-6) * qg_ref[...]
    cqb = cqn.astype(BF16)
    for c, t in enumerate(_rope_tiles(_dot(cqb, wuq_ref[...]), cosf, sinf)):
        q_o[:, c * LANE:(c + 1) * LANE] = (t * Q_LOGIT_SCALE).astype(q_o.dtype)
    for c, t in enumerate(_rope_tiles(_dot(cqb, wiq_ref[...]), cosf, sinf)):
        qi_o[:, c * LANE:(c + 1) * LANE] = t.astype(qi_o.dtype)
    for c, t in enumerate(_rope_tiles(ka_ref[...], cosf, sinf)):
        k_o[:, c * LANE:(c + 1) * LANE] = t.astype(k_o.dtype)
    kx = kidx_ref[...]
    mu = jnp.mean(kx, axis=-1, keepdims=True)
    var = jnp.mean(jnp.square(kx - mu), axis=-1, keepdims=True)
    kn = (kx - mu) * lax.rsqrt(var + LN_EPS) * kg_ref[...] + kb_ref[...]
    ki_o[...] = _rope_tiles(kn, cosf, sinf)[0].astype(ki_o.dtype)
    wq_o[...] = widx_ref[...] * wq_scale


def _dsa_prep(proj, bsz, p_len, off, pw, cosf, sinf, qg, wuq, wiq, kg, kb, idx_heads):
    tm = _pick_tile(p_len, (256, 128))
    nb = p_len // tm
    rows = bsz * p_len
    aw, iw = wuq.shape[1], wiq.shape[1]

    def seg(n):
        w, c = pw[n], off[n] // pw[n]
        return pl.BlockSpec((tm, w), lambda i, c=c: (i, c))

    def full(a):
        return pl.BlockSpec(a.shape, lambda i: (0,) * a.ndim)

    pos_spec = pl.BlockSpec((tm, LANE), lambda i: (i % nb, 0))
    row_spec = lambda w: pl.BlockSpec((tm, w), lambda i: (i, 0))
    kern = functools.partial(_dsa_prep_kernel, wq_scale=float(idx_heads ** -0.5 * AT_HEAD ** -0.5))
    return pl.pallas_call(
        kern,
        grid=(rows // tm,),
        in_specs=[seg("cq"), seg("ka"), seg("va"), seg("kidx"), seg("widx"), pos_spec, pos_spec,
                  full(qg), full(wuq), full(wiq), full(kg), full(kb)],
        out_specs=[row_spec(aw), row_spec(iw), row_spec(aw), row_spec(aw), row_spec(LANE), row_spec(LANE)],
        out_shape=[jax.ShapeDtypeStruct((rows, aw), BF16), jax.ShapeDtypeStruct((rows, iw), BF16),
                   jax.ShapeDtypeStruct((rows, aw), BF16), jax.ShapeDtypeStruct((rows, aw), BF16),
                   jax.ShapeDtypeStruct((rows, LANE), BF16), jax.ShapeDtypeStruct((rows, LANE), F32)],
        compiler_params=_cparams(("arbitrary",)),
    )(proj, proj, proj, proj, proj, cosf, sinf, qg, wuq, wiq, kg, kb)


def _topk_mask(sc, causal, n_sel):
    tq, s_len = sc.shape
    bits = lax.bitcast_convert_type(jnp.where(sc == 0.0, 0.0, sc), I32)
    key = jnp.where(bits < 0, bits ^ jnp.int32(0x7FFFFFFF), bits)
    key = jnp.where(causal, key, jnp.int32(INT_MIN))
    one = jnp.ones((tq, s_len), I32)
    zero = jnp.zeros((tq, s_len), I32)

    i16 = jnp.int16
    one16 = jnp.ones((tq, s_len), i16)
    zero16 = jnp.zeros((tq, s_len), i16)

    def count_ge16(x16, th):
        hit = jnp.where(x16 >= th.astype(i16), one16, zero16)
        acc = hit[:, :LANE]
        for j in range(1, s_len // LANE):
            acc = acc + hit[:, j * LANE:(j + 1) * LANE]
        return jnp.sum(acc.astype(I32), axis=1, keepdims=True)

    def kth_largest16(x16):
        t = jnp.where(count_ge16(x16, jnp.zeros((tq, 1), I32)) >= n_sel, jnp.int32(0), jnp.int32(-32768))

        def step(i, t):
            cand = t | jnp.left_shift(jnp.int32(1), 14 - i)
            return jnp.where(count_ge16(x16, cand) >= n_sel, cand, t)

        return lax.fori_loop(0, 15, step, t)

    hi = (key >> 16).astype(i16)
    lo = ((key & jnp.int32(0xFFFF)) - 32768).astype(i16)
    tau_hi = kth_largest16(hi)
    th16 = tau_hi.astype(i16)
    lo_sel = jnp.where(hi == th16, lo, jnp.where(hi > th16, i16(32767), i16(-32768)))
    tau = tau_hi * 65536 + (kth_largest16(lo_sel) + 32768)
    gt = key > tau
    eq = key == tau
    n_gt = jnp.sum(jnp.where(gt, one, zero), axis=1, keepdims=True)
    n_eq = jnp.sum(jnp.where(eq, one, zero), axis=1, keepdims=True)
    need = n_sel - n_gt
    idx = lax.broadcasted_iota(I32, (tq, s_len), 1)
    nbits = max(1, (s_len - 1).bit_length())

    def cut_search():
        def cut_step(i, lo):
            cand = lo | jnp.left_shift(jnp.int32(1), nbits - 1 - i)
            cnt = jnp.sum(jnp.where(eq, jnp.where(idx < cand, one, zero), zero), axis=1, keepdims=True)
            return jnp.where(cnt < need, cand, lo)
        return lax.fori_loop(0, nbits, cut_step, jnp.zeros((tq, 1), I32))

    surplus = jnp.where(tau > jnp.int32(INT_MIN), jnp.where(n_eq > need, 1, 0), 0)
    cut = lax.cond(jnp.max(surplus) > 0, cut_search, lambda: jnp.full((tq, 1), s_len, I32))
    sel = jnp.where(gt, one, jnp.where(eq, jnp.where(idx <= cut, one, zero), zero))
    return jnp.where(causal, sel, zero) > 0


def _dsa_attn_kernel(q_ref, qi_ref, wq_ref, k_ref, v_ref, ki_ref, mix_ref, o_ref, *, n_sel, idx_heads, q_lo):
    del mix_ref
    tq = q_ref.shape[0]
    s_len = k_ref.shape[0]
    t0 = (pl.program_id(1) + q_lo) * tq
    ki = ki_ref[...]
    wq = wq_ref[...]
    sc = jnp.zeros((tq, s_len), F32)
    for h in range(idx_heads):
        s_h = _dot(qi_ref[:, h * LANE:(h + 1) * LANE], ki, NT)
        sc = sc + jnp.maximum(s_h, 0.0) * wq[:, h:h + 1]
    qpos = t0 + lax.broadcasted_iota(I32, (tq, s_len), 0)
    kpos = lax.broadcasted_iota(I32, (tq, s_len), 1)
    mask = _topk_mask(sc, kpos <= qpos, n_sel)
    bias = jnp.where(mask, 0.0, -jnp.inf)
    for h in range(q_ref.shape[1] // AT_HEAD):
        sl = slice(h * AT_HEAD, (h + 1) * AT_HEAD)
        lg = _dot(q_ref[:, sl], k_ref[:, sl], NT) + bias
        p = jnp.exp2(lg - jnp.max(lg, axis=1, keepdims=True))
        den = jnp.sum(p, axis=1, keepdims=True)
        o_ref[:, sl] = (_dot(p.astype(BF16), v_ref[:, sl]) / den).astype(o_ref.dtype)


def _dsa_attn(q, qi, wq, k, v, ki, mix, col, n_sel, idx_heads):
    bsz, p_len, _ = mix.shape
    aw, iw = q.shape[1], qi.shape[1]
    assert col % aw == 0
    big_rows = min(p_len, DSA_BIG_S) // DSA_TQ * DSA_TQ
    calls = []
    for tq, row0, row1, n_groups in ((DSA_TQ, 0, big_rows, DSA_GROUPS),
                                     (_pick_tile(p_len - big_rows, (DSA_TAIL_TQ, ROW_TILE)), big_rows, p_len,
                                      DSA_TAIL_GROUPS)):
        nq = (row1 - row0) // tq
        n_groups = min(n_groups, nq)
        bounds = [row0 // tq + round(nq * g / n_groups) for g in range(n_groups + 1)] if nq else []
        calls += [(tq, lo, hi) for lo, hi in zip(bounds[:-1], bounds[1:])]
    r3 = lambda a: a.reshape(bsz, p_len, a.shape[1])
    args = (r3(q), r3(qi), r3(wq), r3(k), r3(v), r3(ki))
    for tq, lo, hi in calls:
        s_len = hi * tq
        assert s_len >= n_sel
        qspec = lambda w, lo=lo, tq=tq: pl.BlockSpec((None, tq, w), lambda b, i: (b, i + lo, 0))
        kspec = lambda w, s_len=s_len: pl.BlockSpec((None, s_len, w), lambda b, i: (b, 0, 0),
                                                    pipeline_mode=pl.Buffered(1))
        kern = functools.partial(_dsa_attn_kernel, n_sel=n_sel, idx_heads=idx_heads, q_lo=lo)
        mix = pl.pallas_call(
            kern,
            grid=(bsz, hi - lo),
            in_specs=[qspec(aw), qspec(iw), qspec(LANE), kspec(aw), kspec(aw), kspec(LANE),
                      pl.BlockSpec(memory_space=pl.ANY)],
            out_specs=pl.BlockSpec((None, tq, aw), lambda b, i, lo=lo: (b, i + lo, col // aw)),
            out_shape=jax.ShapeDtypeStruct(mix.shape, mix.dtype),
            input_output_aliases={6: 0},
            compiler_params=_cparams(("arbitrary", "arbitrary"),
                                     VMEM_LIMIT_WIDE if tq == DSA_TQ else VMEM_LIMIT),
        )(*args, mix)
    return mix


def _pool_kernel(p_ref, h_ref, w_ref, s_ref, mix_ref, o_ref):
    del mix_ref
    tm, width = p_ref.shape
    grp = width // len(POOL_WINDOWS)
    t0 = pl.program_id(1) * tm
    x = p_ref[...]
    halo = jnp.where(pl.program_id(1) == 0, 0.0, h_ref[...])
    ext = jnp.concatenate([halo, x], axis=0)
    tpos = t0 + lax.broadcasted_iota(I32, (tm, 1), 0)
    acc = ext
    have = 1
    for gi, win in enumerate(POOL_WINDOWS):
        while have < win:
            shifted = jnp.concatenate([jnp.zeros((have, width), F32), acc[:-have]], axis=0)
            acc = acc + shifted
            have *= 2
        sl = slice(gi * grp, (gi + 1) * grp)
        cnt = jnp.minimum(tpos + 1, win).astype(F32)
        pooled = acc[POOL_HALO:, sl] / cnt - x[:, sl]
        y = _dot(pooled.astype(BF16), w_ref[gi])
        o_ref[:, sl] = (y * s_ref[:, sl]).astype(o_ref.dtype)


def _pool_mix(proj, off, pw, w_pool, scale, mix, col):
    bsz, p_len, d_mix = mix.shape
    width = pw["pin"]
    assert col % width == 0
    tm = _pick_tile(p_len, (256, 128))
    nb = p_len // tm
    c = off["pin"] // width
    hb = tm // POOL_HALO
    mix2 = mix.reshape(bsz * p_len, d_mix)
    out = pl.pallas_call(
        _pool_kernel,
        grid=(bsz, nb),
        in_specs=[pl.BlockSpec((tm, width), lambda b, i: (b * nb + i, c)),
                  pl.BlockSpec((POOL_HALO, width), lambda b, i: (jnp.maximum((b * nb + i) * hb - 1, 0), c)),
                  pl.BlockSpec(w_pool.shape, lambda b, i: (0, 0, 0)),
                  pl.BlockSpec(scale.shape, lambda b, i: (0, 0)),
                  pl.BlockSpec(memory_space=pl.ANY)],
        out_specs=pl.BlockSpec((tm, width), lambda b, i: (b * nb + i, col // width)),
        out_shape=jax.ShapeDtypeStruct(mix2.shape, mix2.dtype),
        input_output_aliases={4: 0},
        compiler_params=_cparams(("arbitrary", "arbitrary")),
    )(proj, proj, w_pool, scale, mix2)
    return out


def _layernorm_rows(x, g, b):
    mu = jnp.mean(x, axis=-1, keepdims=True)
    var = jnp.mean(jnp.square(x - mu), axis=-1, keepdims=True)
    return (x - mu) * lax.rsqrt(var + LN_EPS) * g + b


def _outproj_res_kernel(x_ref, w_ref, h_ref, o_ref, *, alpha):
    o_ref[...] = alpha * h_ref[...] + _dot(x_ref[...], w_ref[...])


def _ln_kernel(x_ref, g_ref, b_ref, o_ref, ob_ref):
    y = _layernorm_rows(x_ref[...], g_ref[...], b_ref[...])
    o_ref[...] = y
    ob_ref[...] = y.astype(ob_ref.dtype)


def _outproj_ln(mix, w_out, h, g, b, alpha):
    r, k = mix.shape
    d = w_out.shape[1]
    tm = _pick_tile(r, (768, 512, 384, 256, 128))
    tn = _pick_tile(d, (1024, 512, 256, 128))
    pre = pl.pallas_call(
        functools.partial(_outproj_res_kernel, alpha=alpha),
        grid=(d // tn, r // tm),
        in_specs=[pl.BlockSpec((tm, k), lambda j, i: (i, 0)),
                  pl.BlockSpec((k, tn), lambda j, i: (0, j)),
                  pl.BlockSpec((tm, tn), lambda j, i: (i, j))],
        out_specs=pl.BlockSpec((tm, tn), lambda j, i: (i, j)),
        out_shape=jax.ShapeDtypeStruct((r, d), F32),
        compiler_params=_cparams(("arbitrary", "arbitrary")),
    )(mix, w_out, h)
    tr = _pick_tile(r, (256, 128))
    return pl.pallas_call(
        _ln_kernel,
        grid=(r // tr,),
        in_specs=[pl.BlockSpec((tr, d), lambda i: (i, 0)),
                  pl.BlockSpec((1, d), lambda i: (0, 0)),
                  pl.BlockSpec((1, d), lambda i: (0, 0))],
        out_specs=[pl.BlockSpec((tr, d), lambda i: (i, 0))] * 2,
        out_shape=[jax.ShapeDtypeStruct((r, d), F32), jax.ShapeDtypeStruct((r, d), BF16)],
        compiler_params=_cparams(("arbitrary",)),
    )(pre, g, b)


def _router_kernel(x_ref, w_ref, b_ref, e_o, g_o, *, n_groups, per_group):
    logits = _dot(x_ref[...], w_ref[...]) + b_ref[...]
    lane = lax.broadcasted_iota(I32, logits.shape, 1)
    big = jnp.int32(LANE)
    ninf = -jnp.inf
    gl = jnp.where(lane < n_groups, logits, ninf)
    ge = jnp.exp(gl - jnp.max(gl, axis=1, keepdims=True))
    gp = ge / jnp.sum(ge, axis=1, keepdims=True)
    g_val = jnp.max(gp, axis=1, keepdims=True)
    g_idx = jnp.min(jnp.where(gp == g_val, lane, big), axis=1, keepdims=True)
    lane_grp = jnp.where(lane >= n_groups, (lane - n_groups) // per_group, -1)
    in_grp = lane_grp == g_idx
    el = jnp.where(in_grp, logits, ninf)
    ee = jnp.exp(el - jnp.max(el, axis=1, keepdims=True))
    ep = jnp.where(in_grp, ee / jnp.sum(ee, axis=1, keepdims=True), -1.0)
    v1 = jnp.max(ep, axis=1, keepdims=True)
    i1 = jnp.min(jnp.where(ep == v1, lane, big), axis=1, keepdims=True)
    ep2 = jnp.where(lane == i1, -1.0, ep)
    v2 = jnp.max(ep2, axis=1, keepdims=True)
    i2 = jnp.min(jnp.where(ep2 == v2, lane, big), axis=1, keepdims=True)
    tot = v1 + v2
    e_o[...] = jnp.where(lane == 0, i1 - n_groups, jnp.where(lane == 1, i2 - n_groups, 0))
    g_o[...] = jnp.where(lane == 0, g_val * v1 / tot, jnp.where(lane == 1, g_val * v2 / tot, 0.0))


def _router(hb, w, b, n_groups, per_group):
    r, d = hb.shape
    tm = _pick_tile(r, (256, 128))
    kern = functools.partial(_router_kernel, n_groups=n_groups, per_group=per_group)
    return pl.pallas_call(
        kern,
        grid=(r // tm,),
        in_specs=[pl.BlockSpec((tm, d), lambda i: (i, 0)),
                  pl.BlockSpec((d, LANE), lambda i: (0, 0)),
                  pl.BlockSpec((1, LANE), lambda i: (0, 0))],
        out_specs=[pl.BlockSpec((tm, LANE), lambda i: (i, 0))] * 2,
        out_shape=[jax.ShapeDtypeStruct((r, LANE), I32), jax.ShapeDtypeStruct((r, LANE), F32)],
        compiler_params=_cparams(("arbitrary",)),
    )(hb, w, b)


def _slab_pitch(ns):
    tiles = -(-ns // 8)
    return 8 * (tiles + 1 - tiles % 2)


def _to_slabs_kernel(x_ref, o_ref):
    tm, pitch, _ = o_ref.shape
    ns = x_ref.shape[1] // LANE
    for c in range(ns):
        o_ref[:, c, :] = x_ref[:, c * LANE:(c + 1) * LANE]
    if pitch > ns:
        o_ref[:, ns:, :] = jnp.zeros((tm, pitch - ns, LANE), o_ref.dtype)


def _to_slabs(x):
    r, d = x.shape
    pitch = _slab_pitch(d // LANE)
    tm = _pick_tile(r, (256, 128))
    return pl.pallas_call(
        _to_slabs_kernel,
        grid=(r // tm,),
        in_specs=[pl.BlockSpec((tm, d), lambda i: (i, 0))],
        out_specs=pl.BlockSpec((tm, pitch, LANE), lambda i: (i, 0, 0)),
        out_shape=jax.ShapeDtypeStruct((r, pitch, LANE), x.dtype),
        compiler_params=_cparams(("arbitrary",)),
    )(x)


def _gather_start(src_hbm, idx_ref, dst, sem):
    def issue(j, carry):
        pltpu.make_async_copy(src_hbm.at[pl.ds(idx_ref[0, 0, j], 1)], dst.at[pl.ds(j, 1)], sem).start()
        return carry
    lax.fori_loop(0, dst.shape[0], issue, 0)


def _gather_wait(src_hbm, dst, sem):
    pltpu.make_async_copy(src_hbm.at[pl.ds(0, dst.shape[0])], dst, sem).wait()


def _slab_cols(ref3, lo, hi):
    return jnp.concatenate([ref3[:, c, :] for c in range(lo, hi)], axis=1)


def _moe_up_kernel(be_ref, nb_ref, tok_ref, tok_next_ref, h_hbm, w1_ref, w3_ref, act_ref, xbuf, sem):
    del be_ref
    i = pl.program_id(0)
    n_used = nb_ref[0]
    slot = i % 2
    d = w1_ref.shape[0]
    ns = d // LANE
    per = MOE_KC // LANE
    src = h_hbm.at[:, pl.ds(0, ns)]
    rows = lambda s: xbuf.at[s, :, pl.ds(0, ns)]

    @pl.when(i == 0)
    def _():
        _gather_start(src, tok_ref, rows(0), sem.at[0])

    @pl.when(i + 1 < n_used)
    def _():
        _gather_start(src, tok_next_ref, rows(1 - slot), sem.at[1 - slot])

    @pl.when(i < n_used)
    def _():
        _gather_wait(src, rows(slot), sem.at[slot])
        h1 = jnp.zeros(act_ref.shape, F32)
        h3 = jnp.zeros(act_ref.shape, F32)
        for c in range(d // MOE_KC):
            ks = slice(c * MOE_KC, (c + 1) * MOE_KC)
            x = _slab_cols(xbuf.at[slot], c * per, (c + 1) * per).astype(BF16)
            h1 = h1 + _dot(x, w1_ref[ks, :].astype(BF16))
            h3 = h3 + _dot(x, w3_ref[ks, :].astype(BF16))
        act_ref[...] = (h1 * _sigmoid(h1) * h3).astype(act_ref.dtype)

    @pl.when(i >= n_used)
    def _():
        act_ref[...] = jnp.zeros_like(act_ref)


def _moe_down_kernel(be_ref, nb_ref, act_ref, w2_ref, y_ref):
    del be_ref
    i = pl.program_id(0)

    @pl.when(i < nb_ref[0])
    def _():
        act = act_ref[...]
        for c in range(y_ref.shape[1] // MOE_KC):
            ks = slice(c * MOE_KC, (c + 1) * MOE_KC)
            y_ref[:, ks] = _dot(act, w2_ref[:, ks].astype(BF16))

    @pl.when(i >= nb_ref[0])
    def _():
        y_ref[...] = jnp.zeros_like(y_ref)


def _moe_ffn(hs, w1, w3, w2, layer, blk_e, n_used, row_tok):
    r, pitch, _ = hs.shape
    d = w1.shape[2]
    nblk = blk_e.shape[0]
    de = w1.shape[3]
    bm = row_tok.shape[2]
    wspec = lambda a, b: pl.BlockSpec((None, None, a, b), lambda i, be, nb: (layer, be[i], 0, 0))
    up_spec = pltpu.PrefetchScalarGridSpec(
        num_scalar_prefetch=2,
        grid=(nblk,),
        in_specs=[pl.BlockSpec((1, 1, bm), lambda i, be, nb: (i, 0, 0), memory_space=pltpu.SMEM),
                  pl.BlockSpec((1, 1, bm), lambda i, be, nb: (jnp.minimum(i + 1, nblk - 1), 0, 0),
                               memory_space=pltpu.SMEM),
                  pl.BlockSpec(memory_space=pl.ANY), wspec(d, de), wspec(d, de)],
        out_specs=pl.BlockSpec((bm, de), lambda i, be, nb: (i, 0)),
        scratch_shapes=[pltpu.VMEM((2, bm, pitch, LANE), F32), pltpu.SemaphoreType.DMA((2,))],
    )
    act = pl.pallas_call(
        _moe_up_kernel,
        grid_spec=up_spec,
        out_shape=jax.ShapeDtypeStruct((nblk * bm, de), BF16),
        compiler_params=_cparams(("arbitrary",)),
    )(blk_e, n_used, row_tok, row_tok, hs, w1, w3)
    down_spec = pltpu.PrefetchScalarGridSpec(
        num_scalar_prefetch=2,
        grid=(nblk,),
        in_specs=[pl.BlockSpec((bm, de), lambda i, be, nb: (i, 0)), wspec(de, d)],
        out_specs=pl.BlockSpec((bm, d), lambda i, be, nb: (i, 0)),
    )
    return pl.pallas_call(
        _moe_down_kernel,
        grid_spec=down_spec,
        out_shape=jax.ShapeDtypeStruct((nblk * bm, d), F32),
        compiler_params=_cparams(("arbitrary",)),
    )(blk_e, n_used, act, w2)


def _combine_ln_kernel(d0_ref, d1_ref, d0n_ref, d1n_ref, y_hbm, h_ref, gate_ref, g_ref, b_ref, o_ref, ob_ref,
                       ybuf, sem, *, alpha):
    i = pl.program_id(0)
    slot = i % 2

    @pl.when(i == 0)
    def _():
        _gather_start(y_hbm, d0_ref, ybuf.at[0, 0], sem.at[0, 0])
        _gather_start(y_hbm, d1_ref, ybuf.at[0, 1], sem.at[0, 1])

    @pl.when(i + 1 < pl.num_programs(0))
    def _():
        _gather_start(y_hbm, d0n_ref, ybuf.at[1 - slot, 0], sem.at[1 - slot, 0])
        _gather_start(y_hbm, d1n_ref, ybuf.at[1 - slot, 1], sem.at[1 - slot, 1])

    _gather_wait(y_hbm, ybuf.at[slot, 0], sem.at[slot, 0])
    _gather_wait(y_hbm, ybuf.at[slot, 1], sem.at[slot, 1])
    gate = gate_ref[...]
    ff = ybuf[slot, 0] * gate[:, 0:1] + ybuf[slot, 1] * gate[:, 1:2]
    y = _layernorm_rows(alpha * h_ref[...] + ff, g_ref[...], b_ref[...])
    o_ref[...] = y
    ob_ref[...] = y.astype(ob_ref.dtype)


def _combine_ln(yb, h, gates, dest0, dest1, g, b, alpha):
    r, d = h.shape
    tm = dest0.shape[2]
    nblk = r // tm
    kern = functools.partial(_combine_ln_kernel, alpha=alpha)
    idx_spec = pl.BlockSpec((1, 1, tm), lambda i: (i, 0, 0), memory_space=pltpu.SMEM)
    nxt_spec = pl.BlockSpec((1, 1, tm), lambda i: (jnp.minimum(i + 1, nblk - 1), 0, 0), memory_space=pltpu.SMEM)
    return pl.pallas_call(
        kern,
        grid=(nblk,),
        in_specs=[idx_spec, idx_spec, nxt_spec, nxt_spec,
                  pl.BlockSpec(memory_space=pl.ANY),
                  pl.BlockSpec((tm, d), lambda i: (i, 0)),
                  pl.BlockSpec((tm, LANE), lambda i: (i, 0)),
                  pl.BlockSpec((1, d), lambda i: (0, 0)),
                  pl.BlockSpec((1, d), lambda i: (0, 0))],
        out_specs=[pl.BlockSpec((tm, d), lambda i: (i, 0))] * 2,
        out_shape=[jax.ShapeDtypeStruct((r, d), F32), jax.ShapeDtypeStruct((r, d), BF16)],
        scratch_shapes=[pltpu.VMEM((2, 2, tm, d), F32), pltpu.SemaphoreType.DMA((2, 2))],
        compiler_params=_cparams(("arbitrary",)),
    )(dest0, dest1, dest0, dest1, yb, h, gates, g, b)


def _dispatch_plan(eid, n_experts, bm):
    r = eid.shape[0]
    a_tot = r * TOP_K
    e_flat = eid[:, :TOP_K].reshape(a_tot)
    onehot = (e_flat[:, None] == jnp.arange(n_experts, dtype=I32)[None, :]).astype(I32)
    before = jnp.cumsum(onehot, axis=0) - onehot
    rank = jnp.sum(before * onehot, axis=1)
    counts = jnp.sum(onehot, axis=0)
    pcounts = (counts + bm - 1) // bm * bm
    pends = jnp.cumsum(pcounts)
    pstarts = pends - pcounts
    dest = (pstarts[e_flat] + rank).astype(I32)
    nblk = -(-(a_tot + n_experts * (bm - 1)) // bm)
    row_tok = jnp.zeros((nblk * bm,), I32).at[dest].set(jnp.arange(a_tot, dtype=I32) // TOP_K)
    n_used = (pends[-1] // bm).astype(I32)
    blk = jnp.arange(nblk, dtype=I32)
    blk_e = jnp.searchsorted(pends, jnp.minimum(blk, n_used - 1) * bm, side="right").astype(I32)
    blk_e = jnp.minimum(blk_e, n_experts - 1)
    dest2 = dest.reshape(r, TOP_K)
    return row_tok.reshape(nblk, 1, bm), blk_e, n_used.reshape(1), dest2[:, 0], dest2[:, 1]


def kernel(x, meta, w_in, rw_mu, rw_w2, rw_w0, rw_a2, rw_a0, rw_g2, rw_kk, rw_ka, rw_rk, rw_gn_g, rw_gn_b,
           at_qnorm_g, at_w_uq, at_w_iq, at_kidx_g, at_kidx_b, pool_w, pool_scale, w_out, ln1_g, ln1_b,
           router_g_w, router_g_b, router_e_w, router_e_b, exp_w1, exp_w3, exp_w2, ln2_g, ln2_b):
    bsz, seq_len, d = x.shape
    depth = w_in.shape[0]
    n_meta = meta.shape[0]
    t_len = seq_len + n_meta
    p_len = _round_up(t_len, ROW_TILE)
    rows = bsz * p_len
    n_sel = min(TOPK_MAX, seq_len // 4)
    alpha = float((2 * depth) ** 0.25)

    rw = rw_w2.shape[2]
    aw = at_w_uq.shape[2]
    idx_head = at_kidx_g.shape[1]
    idx_heads = at_w_iq.shape[2] // idx_head
    assert idx_head == LANE and aw % AT_HEAD == 0 and rw % LANE == 0
    n_groups = router_g_w.shape[2]
    n_experts = router_e_w.shape[2]
    per_group = n_experts // n_groups
    assert n_groups + n_experts <= LANE
    pool_width = pool_w.shape[1] * pool_w.shape[2]

    src = {"r": rw, "k": rw, "v": rw, "dw": rw_w2.shape[1], "da": rw_a2.shape[1], "dg": rw_g2.shape[1],
           "cq": at_w_uq.shape[1], "ka": aw, "va": aw, "kidx": idx_head, "widx": idx_heads, "pin": pool_width}
    off, pw, n_proj = _layout(src)
    n_proj = _round_up(n_proj, 512)
    rw_names = ("r", "k", "v", "dw", "da", "dg")
    rw_src = {n: src[n] for n in rw_names}
    mu_starts = np.cumsum([0] + [rw_src[n] for n in rw_names])

    pos = jnp.arange(p_len, dtype=F32)
    inv = ROPE_THETA ** (-jnp.arange(0, AT_HEAD, 2, dtype=F32) / AT_HEAD)
    ang = pos[:, None] * inv[None, :]
    cosf = jnp.concatenate([jnp.cos(ang), jnp.cos(ang)], axis=1)
    sinf = jnp.concatenate([-jnp.sin(ang), jnp.sin(ang)], axis=1)

    h = jnp.concatenate([jnp.broadcast_to(meta.astype(x.dtype)[None], (bsz, n_meta, d)), x,
                         jnp.zeros((bsz, p_len - t_len, d), x.dtype)], axis=1).reshape(rows, d)
    hb = h.astype(BF16)
    row1 = lambda a: a.reshape(1, -1)
    pad_rows = lambda a, n: jnp.concatenate([a, jnp.zeros((n - a.shape[0],) + a.shape[1:], a.dtype)], axis=0)
    tm_c = ROW_TILE

    w_in_p = _relayout_rows(jnp.swapaxes(w_in, 1, 2), src, off, n_proj)

    for l in range(depth):
        proj = _matmul(hb, w_in_p, l, F32)

        mu = {}
        for i, n in enumerate(rw_names):
            seg = rw_mu[l, int(mu_starts[i]):int(mu_starts[i + 1])]
            mu[n] = row1(jnp.concatenate([seg, jnp.zeros((pw[n] - src[n],), F32)]))
        g2 = pad_rows(rw_g2[l], pw["dg"]).astype(BF16)
        seqs = _rwkv_prep(proj, bsz, p_len, off, pw, mu, rw_w2[l].astype(BF16), row1(rw_w0[l]),
                          rw_a2[l].astype(BF16), row1(rw_a0[l]), g2, row1(rw_kk[l]), row1(rw_ka[l]),
                          row1(rw_rk[l]))
        mix = jnp.zeros((bsz, p_len, rw + aw + pool_width), BF16)
        mix = _rwkv_scan(seqs, row1(rw_gn_g[l]), row1(rw_gn_b[l]), mix)

        q, qi, k_at, v_at, ki, wq = _dsa_prep(proj, bsz, p_len, off, pw, cosf, sinf, row1(at_qnorm_g[l]),
                                              at_w_uq[l].astype(BF16), at_w_iq[l].astype(BF16),
                                              row1(at_kidx_g[l]), row1(at_kidx_b[l]), idx_heads)
        mix = _dsa_attn(q, qi, wq, k_at, v_at, ki, mix, rw, n_sel, idx_heads)
        mix = _pool_mix(proj, off, pw, pool_w[l].astype(BF16), row1(pool_scale[l]), mix, rw + aw)
        h, hb = _outproj_ln(mix, w_out[l].astype(BF16), h, row1(ln1_g[l]), row1(ln1_b[l]), alpha)

        n_r = n_groups + n_experts
        w_r = jnp.concatenate([router_g_w[l], router_e_w[l], jnp.zeros((d, LANE - n_r), F32)], axis=1).astype(BF16)
        b_r = row1(jnp.concatenate([router_g_b[l], router_e_b[l], jnp.zeros((LANE - n_r,), F32)]))
        eid, gates = _router(hb, w_r, b_r, n_groups, per_group)
        row_tok, blk_e, n_used, dest0, dest1 = _dispatch_plan(eid, n_experts, MOE_BM)
        yb = _moe_ffn(_to_slabs(h), exp_w1, exp_w3, exp_w2, l, blk_e, n_used, row_tok)
        h, hb = _combine_ln(yb, h, gates, dest0.reshape(rows // tm_c, 1, tm_c), dest1.reshape(rows // tm_c, 1, tm_c),
                            row1(ln2_g[l]), row1(ln2_b[l]), alpha)

    return h.reshape(bsz, p_len, d)[:, n_meta:t_len]
```

```python
import functools
import math

import jax
import jax.numpy as jnp
import numpy as np
from jax import lax
from jax.experimental import pallas as pl
from jax.experimental.pallas import tpu as pltpu

F32 = jnp.float32
BF16 = jnp.bfloat16
I32 = jnp.int32

LANE = 128
ROW_TILE = 128
RW_HEAD = 64
AT_HEAD = 128
TOPK_MAX = 256
ROPE_THETA = 10000.0
POOL_WINDOWS = (2, 4, 8, 16)
POOL_HALO = 16
GN_EPS = 64e-5
LN_EPS = 1e-5
RW_CHUNK = 64
MOE_BM = 384
MOE_KC = 512
DSA_TQ = 384
DSA_BIG_S = 3072
DSA_GROUPS = 8
DSA_TAIL_TQ = 192
DSA_TAIL_GROUPS = 3
Q_LOGIT_SCALE = AT_HEAD ** -0.5 * math.log2(math.e)
TOP_K = 2
INT_MIN = -(2 ** 31)
VMEM_LIMIT = 56 * 1024 * 1024
VMEM_LIMIT_WIDE = 60 * 1024 * 1024

HIGHEST = lax.Precision.HIGHEST
NN = (((1,), (0,)), ((), ()))
NT = (((1,), (1,)), ((), ()))
TN = (((0,), (0,)), ((), ()))


def _dot(a, b, dims=NN, precision=None):
    return lax.dot_general(a, b, dims, precision=precision, preferred_element_type=F32)


def _sigmoid(x):
    return 1.0 / (1.0 + jnp.exp(-x))


def _cparams(sem, vmem_limit=VMEM_LIMIT):
    return pltpu.CompilerParams(dimension_semantics=sem, vmem_limit_bytes=vmem_limit)


def _round_up(x, m):
    return -(-x // m) * m


def _pick_tile(n, candidates):
    for c in candidates:
        if n % c == 0:
            return c
    return n


def _layout(sizes):
    pw = {k: _round_up(v, LANE) for k, v in sizes.items()}
    order = sorted(sizes, key=lambda k: -pw[k])
    off, gaps, cur = {}, [], 0
    for k in order:
        w = pw[k]
        placed = False
        for gi, (g0, g1) in enumerate(gaps):
            s = _round_up(g0, w)
            if s + w <= g1:
                off[k] = s
                new = [(g0, s), (s + w, g1)]
                gaps[gi:gi + 1] = [g for g in new if g[1] > g[0]]
                placed = True
                break
        if not placed:
            s = _round_up(cur, w)
            if s > cur:
                gaps.append((cur, s))
            off[k] = s
            cur = s + w
    return off, pw, cur


def _relayout_kernel(w_ref, o_ref, *, moves):
    cur = 0
    total, cols = o_ref.shape
    for s0, w, d0 in moves:
        if d0 > cur:
            o_ref[cur:d0, :] = jnp.zeros((d0 - cur, cols), o_ref.dtype)
        end = _round_up(d0 + w, LANE)
        seg = w_ref[s0:s0 + w, :]
        if end > d0 + w:
            seg = jnp.concatenate([seg, jnp.zeros((end - d0 - w, cols), seg.dtype)], axis=0)
        o_ref[d0:end, :] = seg.astype(o_ref.dtype)
        cur = end
    if total > cur:
        o_ref[cur:total, :] = jnp.zeros((total - cur, cols), o_ref.dtype)


def _relayout_rows(w_t, src_sizes, off, total):
    depth, n_src, k = w_t.shape
    names = list(src_sizes)
    starts = np.cumsum([0] + [src_sizes[n] for n in names])
    moves = tuple(sorted(((int(starts[i]), src_sizes[n], off[n]) for i, n in enumerate(names)), key=lambda m: m[2]))
    assert all(s0 % 8 == 0 and w % 8 == 0 for s0, w, _ in moves)
    tc = _pick_tile(k, (256, 128))
    return pl.pallas_call(
        functools.partial(_relayout_kernel, moves=moves),
        grid=(depth, k // tc),
        in_specs=[pl.BlockSpec((None, n_src, tc), lambda l, i: (l, 0, i))],
        out_specs=pl.BlockSpec((None, total, tc), lambda l, i: (l, 0, i)),
        out_shape=jax.ShapeDtypeStruct((depth, total, k), BF16),
        compiler_params=_cparams(("arbitrary", "arbitrary")),
    )(w_t)


def _mm_kernel(x_ref, w_ref, o_ref):
    o_ref[...] = _dot(x_ref[...], w_ref[...], NT).astype(o_ref.dtype)


def _matmul(x, w, layer, out_dtype):
    r, k = x.shape
    n = w.shape[1]
    tm = _pick_tile(r, (768, 512, 384, 256, 128))
    tn = _pick_tile(n, (1024, 512, 256, 128))
    return pl.pallas_call(
        _mm_kernel,
        grid=(n // tn, r // tm),
        in_specs=[pl.BlockSpec((tm, k), lambda j, i: (i, 0)),
                  pl.BlockSpec((None, tn, k), lambda j, i: (layer, j, 0))],
        out_specs=pl.BlockSpec((tm, tn), lambda j, i: (i, j)),
        out_shape=jax.ShapeDtypeStruct((r, n), out_dtype),
        compiler_params=_cparams(("arbitrary", "arbitrary")),
    )(x, w)


def _head_sums(x, hsum):
    tiles = [_dot(x[:, c * LANE:(c + 1) * LANE], hsum, precision=HIGHEST) for c in range(x.shape[1] // LANE)]
    return jnp.concatenate(tiles, axis=1)


def _split3(x):
    hi = x.astype(BF16)
    r1 = x - hi.astype(F32)
    mid = r1.astype(BF16)
    lo = (r1 - mid.astype(F32)).astype(BF16)
    return hi, mid, lo


def _rwkv_prep_kernel(r_ref, k_ref, v_ref, dw_ref, da_ref, dg_ref,
                      rh_ref, kh_ref, vh_ref, dwh_ref, dah_ref, dgh_ref,
                      mur_ref, muk_ref, muv_ref, mudw_ref, muda_ref, mudg_ref,
                      w2_ref, w0_ref, a2_ref, a0_ref, g2_ref, kkp_ref, kap_ref, rk_ref, hsum_ref,
                      tri_ref, blk_ref,
                      at_o, bt_o, kt_o, rt_o, bc_o, kc_o, v_o, epc_o, bv_o, g_o):
    first = pl.program_id(1) == 0

    def shift(x_ref, h_ref, mu_ref):
        x = x_ref[...]
        prev_row = jnp.where(first, 0.0, h_ref[7:8, :])
        row = lax.broadcasted_iota(I32, x.shape, 0)
        prev = jnp.where(row == 0, prev_row, pltpu.roll(x, 1, axis=0))
        return x + (prev - x) * mu_ref[...]

    r = shift(r_ref, rh_ref, mur_ref)
    k = shift(k_ref, kh_ref, muk_ref)
    v = shift(v_ref, vh_ref, muv_ref)
    dw = shift(dw_ref, dwh_ref, mudw_ref)
    da = shift(da_ref, dah_ref, muda_ref)
    dg = shift(dg_ref, dgh_ref, mudg_ref)

    wl = w0_ref[...] + _dot(jnp.tanh(dw).astype(BF16), w2_ref[...])
    neg = -wl
    softplus = jnp.maximum(neg, 0.0) + jnp.log(1.0 + jnp.exp(-jnp.abs(neg)))
    lw = -jnp.exp(-softplus - 0.5)
    a = _sigmoid(a0_ref[...] + _dot(da.astype(BF16), a2_ref[...]))
    g_o[...] = _dot(_sigmoid(dg).astype(BF16), g2_ref[...])
    hsum = hsum_ref[...]
    kk = k * kkp_ref[...]
    kk = kk / jnp.maximum(jnp.sqrt(_head_sums(kk * kk, hsum)), 1e-12)
    k = k * (1.0 + (a - 1.0) * kap_ref[...])
    bv_o[...] = _head_sums(r * k * rk_ref[...], hsum) * v

    parts = _split3(lw)
    tri, blk = tri_ref[...], blk_ref[...]
    cl = _dot(tri, parts[0]) + _dot(tri, parts[1]) + _dot(tri, parts[2])
    clc = _dot(blk, parts[0]) + _dot(blk, parts[1]) + _dot(blk, parts[2])
    e_n = jnp.exp(-cl)
    e_nc = jnp.exp(clc - cl)
    kka = kk * a
    at_o[...] = (-kk * jnp.exp(cl - lw)).astype(at_o.dtype)
    bt_o[...] = (kka * e_n).astype(bt_o.dtype)
    kt_o[...] = (k * e_n).astype(kt_o.dtype)
    rt_o[...] = (r * jnp.exp(cl)).astype(rt_o.dtype)
    bc_o[...] = (kka * e_nc).astype(bc_o.dtype)
    kc_o[...] = (k * e_nc).astype(kc_o.dtype)
    v_o[...] = v.astype(v_o.dtype)
    epc_o[...] = jnp.exp(clc)


def _rwkv_prep(proj, bsz, p_len, off, pw, mu, w2, w0, a2, a0, g2, kkp, kap, rk):
    rw = w2.shape[1]
    tm = ROW_TILE
    nb = p_len // tm
    names = ("r", "k", "v", "dw", "da", "dg")

    def cur_spec(n):
        w, c = pw[n], off[n] // pw[n]
        return pl.BlockSpec((tm, w), lambda b, i, c=c: (b * nb + i, c))

    def halo_spec(n):
        w, c = pw[n], off[n] // pw[n]
        return pl.BlockSpec((8, w), lambda b, i, c=c: (jnp.maximum((b * nb + i) * (tm // 8) - 1, 0), c))

    def full(a):
        return pl.BlockSpec(a.shape, lambda b, i: (0,) * a.ndim)

    lane = np.arange(LANE)
    hsum = jnp.asarray((lane[:, None] // RW_HEAD == lane[None, :] // RW_HEAD).astype(np.float32))
    t = np.arange(tm)
    same = t[:, None] // RW_CHUNK == t[None, :] // RW_CHUNK
    tri = jnp.asarray((same & (t[None, :] <= t[:, None])).astype(np.float32)).astype(BF16)
    blk = jnp.asarray(same.astype(np.float32)).astype(BF16)
    params = [mu[n] for n in names] + [w2, w0, a2, a0, g2, kkp, kap, rk, hsum, tri, blk]
    out_spec = pl.BlockSpec((tm, rw), lambda b, i: (b * nb + i, 0))
    sd = lambda dt: jax.ShapeDtypeStruct((bsz * p_len, rw), dt)
    return pl.pallas_call(
        _rwkv_prep_kernel,
        grid=(bsz, nb),
        in_specs=[cur_spec(n) for n in names] + [halo_spec(n) for n in names] + [full(a) for a in params],
        out_specs=[out_spec] * 10,
        out_shape=[sd(BF16)] * 7 + [sd(F32)] * 3,
        compiler_params=_cparams(("arbitrary", "arbitrary")),
    )(*([proj] * 12), *params)


def _rwkv_heads(at, bt, kt, rt, bc, kc, v, p_row, zt):
    n = len(at)
    hs = range(n)
    c = at[0].shape[0]
    b16 = lambda x: x.astype(BF16)
    row = lax.broadcasted_iota(I32, (2 * c, c), 0)
    col = lax.broadcasted_iota(I32, (2 * c, c), 1)
    keep = col < (row & (c - 1)) + jnp.where(row < c, 0, 1)
    lhs2 = [jnp.concatenate([at[h], rt[h]], axis=0) for h in hs]
    x_b = [jnp.where(keep, _dot(lhs2[h], bt[h], NT), 0.0) for h in hs]
    x_k = [jnp.where(keep, _dot(lhs2[h], kt[h], NT), 0.0) for h in hs]
    lkmv = [_dot(b16(x_k[h]), v[h]) for h in hs]
    l_ba = [x[:c] for x in x_b]
    m_b = [b16(x[c:]) for x in x_b]

    ti = lax.broadcasted_iota(I32, (c, c), 0)
    si = lax.broadcasted_iota(I32, (c, c), 1)
    zero = jnp.zeros((c, c), F32)
    eye = (ti == si).astype(F32)
    same_lo = (ti >> 4) == (si >> 4)
    lp = [jnp.where(same_lo, l, zero) for l in l_ba]
    t = [eye + l for l in lp]
    for _ in range(3):
        lp = [_dot(b16(l), b16(l)) for l in lp]
        t = [t[h] + _dot(b16(lp[h]), b16(t[h])) for h in hs]
    size = 32
    while size <= c:
        shift = size.bit_length() - 1
        same_hi = (ti >> shift) == (si >> shift)
        l_off = [b16(jnp.where(same_hi, jnp.where(same_lo, zero, l), zero)) for l in l_ba]
        tl = [_dot(b16(t[h]), l_off[h]) for h in hs]
        t = [t[h] + _dot(b16(tl[h]), b16(t[h])) for h in hs]
        same_lo = same_hi
        size *= 2

    tb = [b16(x) for x in t]
    a_z = [b16(_dot(tb[h], at[h])) for h in hs]
    w_u = [b16(_dot(tb[h], b16(lkmv[h][:c]))) for h in hs]
    ztb = [b16(z) for z in zt]
    q_z = [b16(rt[h].astype(F32) + _dot(m_b[h], a_z[h])) for h in hs]
    y = [_dot(q_z[h], ztb[h], NT) + _dot(m_b[h], w_u[h]) + lkmv[h][c:] for h in hs]
    g_t = [b16(_dot(a_z[h], bc[h], TN)) for h in hs]
    h_t = [_dot(w_u[h], bc[h], TN) + _dot(v[h], kc[h], TN) for h in hs]
    zt_new = [p_row[h] * zt[h] + _dot(ztb[h], g_t[h]) + h_t[h] for h in hs]
    return y, zt_new


def _rwkv_scan_kernel(at_ref, bt_ref, kt_ref, rt_ref, bc_ref, kc_ref, v_ref, epc_ref, bv_ref, g_ref,
                      gg_ref, gb_ref, mix_ref, o_ref, z_ref):
    del mix_ref
    @pl.when(pl.program_id(2) == 0)
    def _():
        z_ref[...] = jnp.zeros_like(z_ref)

    n_heads = o_ref.shape[1] // RW_HEAD
    sls = [slice(h * RW_HEAD, (h + 1) * RW_HEAD) for h in range(n_heads)]
    cut = lambda ref: [ref[:, sl] for sl in sls]
    y, z_new = _rwkv_heads(cut(at_ref), cut(bt_ref), cut(kt_ref), cut(rt_ref), cut(bc_ref), cut(kc_ref),
                           cut(v_ref), [epc_ref[0:1, sl] for sl in sls], [z_ref[h] for h in range(n_heads)])
    for h in range(n_heads):
        z_ref[h] = z_new[h]
    outs = []
    for h, sl in enumerate(sls):
        mu = jnp.mean(y[h], axis=-1, keepdims=True)
        var = jnp.mean(jnp.square(y[h] - mu), axis=-1, keepdims=True)
        yn = (y[h] - mu) * lax.rsqrt(var + GN_EPS) * gg_ref[:, sl] + gb_ref[:, sl]
        outs.append((yn + bv_ref[:, sl]) * g_ref[:, sl])
    per = LANE // RW_HEAD
    for p in range(n_heads // per):
        o_ref[:, p * LANE:(p + 1) * LANE] = jnp.concatenate(outs[p * per:(p + 1) * per], axis=1).astype(o_ref.dtype)


def _rwkv_scan(seqs, gn_g, gn_b, mix):
    bsz, p_len, _ = mix.shape
    rw = gn_g.shape[1]
    width = _pick_tile(rw, (12 * LANE, 4 * LANE, 3 * LANE, 2 * LANE, LANE))
    nch = p_len // RW_CHUNK
    seqs = [s.reshape(bsz, p_len, rw) for s in seqs]
    seq_spec = pl.BlockSpec((None, RW_CHUNK, width), lambda b, p, c: (b, c, p))
    par_spec = pl.BlockSpec((1, width), lambda b, p, c: (0, p))
    return pl.pallas_call(
        _rwkv_scan_kernel,
        grid=(bsz, rw // width, nch),
        in_specs=[seq_spec] * len(seqs) + [par_spec] * 2 + [pl.BlockSpec(memory_space=pl.ANY)],
        out_specs=seq_spec,
        out_shape=jax.ShapeDtypeStruct(mix.shape, mix.dtype),
        input_output_aliases={len(seqs) + 2: 0},
        scratch_shapes=[pltpu.VMEM((width // RW_HEAD, RW_HEAD, RW_HEAD), F32)],
        compiler_params=_cparams(("arbitrary", "arbitrary", "arbitrary")),
    )(*seqs, gn_g, gn_b, mix)


def _rope_tiles(x, cosf, sinf):
    outs = []
    for c in range(x.shape[1] // LANE):
        xt = x[:, c * LANE:(c + 1) * LANE]
        outs.append(xt * cosf + pltpu.roll(xt, LANE // 2, axis=1) * sinf)
    return outs


def _dsa_prep_kernel(cq_ref, ka_ref, va_ref, kidx_ref, widx_ref, cos_ref, sin_ref, qg_ref, wuq_ref, wiq_ref,
                     kg_ref, kb_ref, q_o, qi_o, k_o, v_o, ki_o, wq_o, *, wq_scale):
    cosf, sinf = cos_ref[...], sin_ref[...]
    v_o[...] = va_ref[...].astype(v_o.dtype)
    cq = cq_ref[...]
    cqn = cq * lax.rsqrt(jnp.mean(jnp.square(cq), axis=-1, keepdims=True) + 1e-6) * qg_ref[...]
    cqb = cqn.astype(BF16)
    for c, t in enumerate(_rope_tiles(_dot(cqb, wuq_ref[...]), cosf, sinf)):
        q_o[:, c * LANE:(c + 1) * LANE] = (t * Q_LOGIT_SCALE).astype(q_o.dtype)
    for c, t in enumerate(_rope_tiles(_dot(cqb, wiq_ref[...]), cosf, sinf)):
        qi_o[:, c * LANE:(c + 1) * LANE] = t.astype(qi_o.dtype)
    for c, t in enumerate(_rope_tiles(ka_ref[...], cosf, sinf)):
        k_o[:, c * LANE:(c + 1) * LANE] = t.astype(k_o.dtype)
    kx = kidx_ref[...]
    mu = jnp.mean(kx, axis=-1, keepdims=True)
    var = jnp.mean(jnp.square(kx - mu), axis=-1, keepdims=True)
    kn = (kx - mu) * lax.rsqrt(var + LN_EPS) * kg_ref[...] + kb_ref[...]
    ki_o[...] = _rope_tiles(kn, cosf, sinf)[0].astype(ki_o.dtype)
    wq_o[...] = widx_ref[...] * wq_scale


def _dsa_prep(proj, bsz, p_len, off, pw, cosf, sinf, qg, wuq, wiq, kg, kb, idx_heads):
    tm = _pick_tile(p_len, (256, 128))
    nb = p_len // tm
    rows = bsz * p_len
    aw, iw = wuq.shape[1], wiq.shape[1]

    def seg(n):
        w, c = pw[n], off[n] // pw[n]
        return pl.BlockSpec((tm, w), lambda i, c=c: (i, c))

    def full(a):
        return pl.BlockSpec(a.shape, lambda i: (0,) * a.ndim)

    pos_spec = pl.BlockSpec((tm, LANE), lambda i: (i % nb, 0))
    row_spec = lambda w: pl.BlockSpec((tm, w), lambda i: (i, 0))
    kern = functools.partial(_dsa_prep_kernel, wq_scale=float(idx_heads ** -0.5 * AT_HEAD ** -0.5))
    return pl.pallas_call(
        kern,
        grid=(rows // tm,),
        in_specs=[seg("cq"), seg("ka"), seg("va"), seg("kidx"), seg("widx"), pos_spec, pos_spec,
                  full(qg), full(wuq), full(wiq), full(kg), full(kb)],
        out_specs=[row_spec(aw), row_spec(iw), row_spec(aw), row_spec(aw), row_spec(LANE), row_spec(LANE)],
        out_shape=[jax.ShapeDtypeStruct((rows, aw), BF16), jax.ShapeDtypeStruct((rows, iw), BF16),
                   jax.ShapeDtypeStruct((rows, aw), BF16), jax.ShapeDtypeStruct((rows, aw), BF16),
                   jax.ShapeDtypeStruct((rows, LANE), BF16), jax.ShapeDtypeStruct((rows, LANE), F32)],
        compiler_params=_cparams(("arbitrary",)),
    )(proj, proj, proj, proj, proj, cosf, sinf, qg, wuq, wiq, kg, kb)


def _topk_mask(sc, causal, n_sel):
    tq, s_len = sc.shape
    bits = lax.bitcast_convert_type(jnp.where(sc == 0.0, 0.0, sc), I32)
    key = jnp.where(bits < 0, bits ^ jnp.int32(0x7FFFFFFF), bits)
    key = jnp.where(causal, key, jnp.int32(INT_MIN))
    one = jnp.ones((tq, s_len), I32)
    zero = jnp.zeros((tq, s_len), I32)

    i16 = jnp.int16
    one16 = jnp.ones((tq, s_len), i16)
    zero16 = jnp.zeros((tq, s_len), i16)

    def count_ge16(x16, th):
        hit = jnp.where(x16 >= th.astype(i16), one16, zero16)
        acc = hit[:, :LANE]
        for j in range(1, s_len // LANE):
            acc = acc + hit[:, j * LANE:(j + 1) * LANE]
        return jnp.sum(acc.astype(I32), axis=1, keepdims=True)

    def kth_largest16(x16):
        t = jnp.where(count_ge16(x16, jnp.zeros((tq, 1), I32)) >= n_sel, jnp.int32(0), jnp.int32(-32768))

        def step(i, t):
            cand = t | jnp.left_shift(jnp.int32(1), 14 - i)
            return jnp.where(count_ge16(x16, cand) >= n_sel, cand, t)

        return lax.fori_loop(0, 15, step, t)

    hi = (key >> 16).astype(i16)
    lo = ((key & jnp.int32(0xFFFF)) - 32768).astype(i16)
    tau_hi = kth_largest16(hi)
    th16 = tau_hi.astype(i16)
    lo_sel = jnp.where(hi == th16, lo, jnp.where(hi > th16, i16(32767), i16(-32768)))
    tau = tau_hi * 65536 + (kth_largest16(lo_sel) + 32768)
    gt = key > tau
    eq = key == tau
    n_gt = jnp.sum(jnp.where(gt, one, zero), axis=1, keepdims=True)
    n_eq = jnp.sum(jnp.where(eq, one, zero), axis=1, keepdims=True)
    need = n_sel - n_gt
    idx = lax.broadcasted_iota(I32, (tq, s_len), 1)
    nbits = max(1, (s_len - 1).bit_length())

    def cut_search():
        def cut_step(i, lo):
            cand = lo | jnp.left_shift(jnp.int32(1), nbits - 1 - i)
            cnt = jnp.sum(jnp.where(eq, jnp.where(idx < cand, one, zero), zero), axis=1, keepdims=True)
            return jnp.where(cnt < need, cand, lo)
        return lax.fori_loop(0, nbits, cut_step, jnp.zeros((tq, 1), I32))

    surplus = jnp.where(tau > jnp.int32(INT_MIN), jnp.where(n_eq > need, 1, 0), 0)
    cut = lax.cond(jnp.max(surplus) > 0, cut_search, lambda: jnp.full((tq, 1), s_len, I32))
    sel = jnp.where(gt, one, jnp.where(eq, jnp.where(idx <= cut, one, zero), zero))
    return jnp.where(causal, sel, zero) > 0


def _dsa_attn_kernel(q_ref, qi_ref, wq_ref, k_ref, v_ref, ki_ref, mix_ref, o_ref, *, n_sel, idx_heads, q_lo):
    del mix_ref
    tq = q_ref.shape[0]
    s_len = k_ref.shape[0]
    t0 = (pl.program_id(1) + q_lo) * tq
    ki = ki_ref[...]
    wq = wq_ref[...]
    sc = jnp.zeros((tq, s_len), F32)
    for h in range(idx_heads):
        s_h = _dot(qi_ref[:, h * LANE:(h + 1) * LANE], ki, NT)
        sc = sc + jnp.maximum(s_h, 0.0) * wq[:, h:h + 1]
    qpos = t0 + lax.broadcasted_iota(I32, (tq, s_len), 0)
    kpos = lax.broadcasted_iota(I32, (tq, s_len), 1)
    mask = _topk_mask(sc, kpos <= qpos, n_sel)
    bias = jnp.where(mask, 0.0, -jnp.inf)
    for h in range(q_ref.shape[1] // AT_HEAD):
        sl = slice(h * AT_HEAD, (h + 1) * AT_HEAD)
        lg = _dot(q_ref[:, sl], k_ref[:, sl], NT) + bias
        p = jnp.exp2(lg - jnp.max(lg, axis=1, keepdims=True))
        den = jnp.sum(p, axis=1, keepdims=True)
        o_ref[:, sl] = (_dot(p.astype(BF16), v_ref[:, sl]) / den).astype(o_ref.dtype)


def _dsa_attn(q, qi, wq, k, v, ki, mix, col, n_sel, idx_heads):
    bsz, p_len, _ = mix.shape
    aw, iw = q.shape[1], qi.shape[1]
    assert col % aw == 0
    big_rows = min(p_len, DSA_BIG_S) // DSA_TQ * DSA_TQ
    calls = []
    for tq, row0, row1, n_groups in ((DSA_TQ, 0, big_rows, DSA_GROUPS),
                                     (_pick_tile(p_len - big_rows, (DSA_TAIL_TQ, ROW_TILE)), big_rows, p_len,
                                      DSA_TAIL_GROUPS)):
        nq = (row1 - row0) // tq
        n_groups = min(n_groups, nq)
        bounds = [row0 // tq + round(nq * g / n_groups) for g in range(n_groups + 1)] if nq else []
        calls += [(tq, lo, hi) for lo, hi in zip(bounds[:-1], bounds[1:])]
    r3 = lambda a: a.reshape(bsz, p_len, a.shape[1])
    args = (r3(q), r3(qi), r3(wq), r3(k), r3(v), r3(ki))
    for tq, lo, hi in calls:
        s_len = hi * tq
        assert s_len >= n_sel
        qspec = lambda w, lo=lo, tq=tq: pl.BlockSpec((None, tq, w), lambda b, i: (b, i + lo, 0))
        kspec = lambda w, s_len=s_len: pl.BlockSpec((None, s_len, w), lambda b, i: (b, 0, 0),
                                                    pipeline_mode=pl.Buffered(1))
        kern = functools.partial(_dsa_attn_kernel, n_sel=n_sel, idx_heads=idx_heads, q_lo=lo)
        mix = pl.pallas_call(
            kern,
            grid=(bsz, hi - lo),
            in_specs=[qspec(aw), qspec(iw), qspec(LANE), kspec(aw), kspec(aw), kspec(LANE),
                      pl.BlockSpec(memory_space=pl.ANY)],
            out_specs=pl.BlockSpec((None, tq, aw), lambda b, i, lo=lo: (b, i + lo, col // aw)),
            out_shape=jax.ShapeDtypeStruct(mix.shape, mix.dtype),
            input_output_aliases={6: 0},
            compiler_params=_cparams(("arbitrary", "arbitrary"),
                                     VMEM_LIMIT_WIDE if tq == DSA_TQ else VMEM_LIMIT),
        )(*args, mix)
    return mix


def _pool_kernel(p_ref, h_ref, w_ref, s_ref, mix_ref, o_ref):
    del mix_ref
    tm, width = p_ref.shape
    grp = width // len(POOL_WINDOWS)
    t0 = pl.program_id(1) * tm
    x = p_ref[...]
    halo = jnp.where(pl.program_id(1) == 0, 0.0, h_ref[...])
    ext = jnp.concatenate([halo, x], axis=0)
    tpos = t0 + lax.broadcasted_iota(I32, (tm, 1), 0)
    acc = ext
    have = 1
    for gi, win in enumerate(POOL_WINDOWS):
        while have < win:
            shifted = jnp.concatenate([jnp.zeros((have, width), F32), acc[:-have]], axis=0)
            acc = acc + shifted
            have *= 2
        sl = slice(gi * grp, (gi + 1) * grp)
        cnt = jnp.minimum(tpos + 1, win).astype(F32)
        pooled = acc[POOL_HALO:, sl] / cnt - x[:, sl]
        y = _dot(pooled.astype(BF16), w_ref[gi])
        o_ref[:, sl] = (y * s_ref[:, sl]).astype(o_ref.dtype)


def _pool_mix(proj, off, pw, w_pool, scale, mix, col):
    bsz, p_len, d_mix = mix.shape
    width = pw["pin"]
    assert col % width == 0
    tm = _pick_tile(p_len, (256, 128))
    nb = p_len // tm
    c = off["pin"] // width
    hb = tm // POOL_HALO
    mix2 = mix.reshape(bsz * p_len, d_mix)
    out = pl.pallas_call(
        _pool_kernel,
        grid=(bsz, nb),
        in_specs=[pl.BlockSpec((tm, width), lambda b, i: (b * nb + i, c)),
                  pl.BlockSpec((POOL_HALO, width), lambda b, i: (jnp.maximum((b * nb + i) * hb - 1, 0), c)),
                  pl.BlockSpec(w_pool.shape, lambda b, i: (0, 0, 0)),
                  pl.BlockSpec(scale.shape, lambda b, i: (0, 0)),
                  pl.BlockSpec(memory_space=pl.ANY)],
        out_specs=pl.BlockSpec((tm, width), lambda b, i: (b * nb + i, col // width)),
        out_shape=jax.ShapeDtypeStruct(mix2.shape, mix2.dtype),
        input_output_aliases={4: 0},
        compiler_params=_cparams(("arbitrary", "arbitrary")),
    )(proj, proj, w_pool, scale, mix2)
    return out


def _layernorm_rows(x, g, b):
    mu = jnp.mean(x, axis=-1, keepdims=True)
    var = jnp.mean(jnp.square(x - mu), axis=-1, keepdims=True)
    return (x - mu) * lax.rsqrt(var + LN_EPS) * g + b


def _outproj_res_kernel(x_ref, w_ref, h_ref, o_ref, *, alpha):
    o_ref[...] = alpha * h_ref[...] + _dot(x_ref[...], w_ref[...])


def _ln_kernel(x_ref, g_ref, b_ref, o_ref, ob_ref, os_ref):
    y = _layernorm_rows(x_ref[...], g_ref[...], b_ref[...])
    o_ref[...] = y
    ob_ref[...] = y.astype(ob_ref.dtype)
    tr, pitch, _ = os_ref.shape
    ns = y.shape[1] // LANE
    for c in range(ns):
        os_ref[:, c, :] = y[:, c * LANE:(c + 1) * LANE]
    if pitch > ns:
        os_ref[:, ns:, :] = jnp.zeros((tr, pitch - ns, LANE), os_ref.dtype)


def _outproj_ln(mix, w_out, h, g, b, alpha):
    r, k = mix.shape
    d = w_out.shape[1]
    tm = _pick_tile(r, (768, 512, 384, 256, 128))
    tn = _pick_tile(d, (1024, 512, 256, 128))
    pre = pl.pallas_call(
        functools.partial(_outproj_res_kernel, alpha=alpha),
        grid=(d // tn, r // tm),
        in_specs=[pl.BlockSpec((tm, k), lambda j, i: (i, 0)),
                  pl.BlockSpec((k, tn), lambda j, i: (0, j)),
                  pl.BlockSpec((tm, tn), lambda j, i: (i, j))],
        out_specs=pl.BlockSpec((tm, tn), lambda j, i: (i, j)),
        out_shape=jax.ShapeDtypeStruct((r, d), F32),
        compiler_params=_cparams(("arbitrary", "arbitrary")),
    )(mix, w_out, h)
    tr = _pick_tile(r, (256, 128))
    pitch = _slab_pitch(d // LANE)
    return pl.pallas_call(
        _ln_kernel,
        grid=(r // tr,),
        in_specs=[pl.BlockSpec((tr, d), lambda i: (i, 0)),
                  pl.BlockSpec((1, d), lambda i: (0, 0)),
                  pl.BlockSpec((1, d), lambda i: (0, 0))],
        out_specs=[pl.BlockSpec((tr, d), lambda i: (i, 0))] * 2
        + [pl.BlockSpec((tr, pitch, LANE), lambda i: (i, 0, 0))],
        out_shape=[jax.ShapeDtypeStruct((r, d), F32), jax.ShapeDtypeStruct((r, d), BF16),
                   jax.ShapeDtypeStruct((r, pitch, LANE), F32)],
        compiler_params=_cparams(("arbitrary",)),
    )(pre, g, b)


def _router_kernel(x_ref, w_ref, b_ref, e_o, g_o, *, n_groups, per_group):
    logits = _dot(x_ref[...], w_ref[...]) + b_ref[...]
    lane = lax.broadcasted_iota(I32, logits.shape, 1)
    big = jnp.int32(LANE)
    ninf = -jnp.inf
    gl = jnp.where(lane < n_groups, logits, ninf)
    ge = jnp.exp(gl - jnp.max(gl, axis=1, keepdims=True))
    gp = ge / jnp.sum(ge, axis=1, keepdims=True)
    g_val = jnp.max(gp, axis=1, keepdims=True)
    g_idx = jnp.min(jnp.where(gp == g_val, lane, big), axis=1, keepdims=True)
    lane_grp = jnp.where(lane >= n_groups, (lane - n_groups) // per_group, -1)
    in_grp = lane_grp == g_idx
    el = jnp.where(in_grp, logits, ninf)
    ee = jnp.exp(el - jnp.max(el, axis=1, keepdims=True))
    ep = jnp.where(in_grp, ee / jnp.sum(ee, axis=1, keepdims=True), -1.0)
    v1 = jnp.max(ep, axis=1, keepdims=True)
    i1 = jnp.min(jnp.where(ep == v1, lane, big), axis=1, keepdims=True)
    ep2 = jnp.where(lane == i1, -1.0, ep)
    v2 = jnp.max(ep2, axis=1, keepdims=True)
    i2 = jnp.min(jnp.where(ep2 == v2, lane, big), axis=1, keepdims=True)
    tot = v1 + v2
    e_o[...] = jnp.where(lane == 0, i1 - n_groups, jnp.where(lane == 1, i2 - n_groups, 0))
    g_o[...] = jnp.where(lane == 0, g_val * v1 / tot, jnp.where(lane == 1, g_val * v2 / tot, 0.0))


def _router(hb, w, b, n_groups, per_group):
    r, d = hb.shape
    tm = _pick_tile(r, (256, 128))
    kern = functools.partial(_router_kernel, n_groups=n_groups, per_group=per_group)
    return pl.pallas_call(
        kern,
        grid=(r // tm,),
        in_specs=[pl.BlockSpec((tm, d), lambda i: (i, 0)),
                  pl.BlockSpec((d, LANE), lambda i: (0, 0)),
                  pl.BlockSpec((1, LANE), lambda i: (0, 0))],
        out_specs=[pl.BlockSpec((tm, LANE), lambda i: (i, 0))] * 2,
        out_shape=[jax.ShapeDtypeStruct((r, LANE), I32), jax.ShapeDtypeStruct((r, LANE), F32)],
        compiler_params=_cparams(("arbitrary",)),
    )(hb, w, b)


def _slab_pitch(ns):
    tiles = -(-ns // 8)
    return 8 * (tiles + 1 - tiles % 2)


def _gather_start(src_hbm, idx_ref, dst, sem):
    def issue(j, carry):
        pltpu.make_async_copy(src_hbm.at[pl.ds(idx_ref[0, 0, j], 1)], dst.at[pl.ds(j, 1)], sem).start()
        return carry
    lax.fori_loop(0, dst.shape[0], issue, 0)


def _gather_wait(src_hbm, dst, sem):
    pltpu.make_async_copy(src_hbm.at[pl.ds(0, dst.shape[0])], dst, sem).wait()


def _slab_cols(ref3, lo, hi):
    return jnp.concatenate([ref3[:, c, :] for c in range(lo, hi)], axis=1)


def _moe_up_kernel(be_ref, nb_ref, tok_ref, tok_next_ref, h_hbm, w1_ref, w3_ref, act_ref, xbuf, sem):
    del be_ref
    i = pl.program_id(0)
    n_used = nb_ref[0]
    slot = i % 2
    d = w1_ref.shape[0]
    ns = d // LANE
    per = MOE_KC // LANE
    src = h_hbm.at[:, pl.ds(0, ns)]
    rows = lambda s: xbuf.at[s, :, pl.ds(0, ns)]

    @pl.when(i == 0)
    def _():
        _gather_start(src, tok_ref, rows(0), sem.at[0])

    @pl.when(i + 1 < n_used)
    def _():
        _gather_start(src, tok_next_ref, rows(1 - slot), sem.at[1 - slot])

    @pl.when(i < n_used)
    def _():
        _gather_wait(src, rows(slot), sem.at[slot])
        h1 = jnp.zeros(act_ref.shape, F32)
        h3 = jnp.zeros(act_ref.shape, F32)
        for c in range(d // MOE_KC):
            ks = slice(c * MOE_KC, (c + 1) * MOE_KC)
            x = _slab_cols(xbuf.at[slot], c * per, (c + 1) * per).astype(BF16)
            h1 = h1 + _dot(x, w1_ref[ks, :].astype(BF16))
            h3 = h3 + _dot(x, w3_ref[ks, :].astype(BF16))
        act_ref[...] = (h1 * _sigmoid(h1) * h3).astype(act_ref.dtype)

    @pl.when(i >= n_used)
    def _():
        act_ref[...] = jnp.zeros_like(act_ref)


def _moe_down_kernel(be_ref, nb_ref, act_ref, w2_ref, y_ref):
    del be_ref
    i = pl.program_id(0)

    @pl.when(i < nb_ref[0])
    def _():
        act = act_ref[...]
        for c in range(y_ref.shape[1] // MOE_KC):
            ks = slice(c * MOE_KC, (c + 1) * MOE_KC)
            y_ref[:, ks] = _dot(act, w2_ref[:, ks].astype(BF16))

    @pl.when(i >= nb_ref[0])
    def _():
        y_ref[...] = jnp.zeros_like(y_ref)


def _moe_ffn(hs, w1, w3, w2, layer, blk_e, n_used, row_tok):
    r, pitch, _ = hs.shape
    d = w1.shape[2]
    nblk = blk_e.shape[0]
    de = w1.shape[3]
    bm = row_tok.shape[2]
    wspec = lambda a, b: pl.BlockSpec((None, None, a, b), lambda i, be, nb: (layer, be[i], 0, 0))
    up_spec = pltpu.PrefetchScalarGridSpec(
        num_scalar_prefetch=2,
        grid=(nblk,),
        in_specs=[pl.BlockSpec((1, 1, bm), lambda i, be, nb: (i, 0, 0), memory_space=pltpu.SMEM),
                  pl.BlockSpec((1, 1, bm), lambda i, be, nb: (jnp.minimum(i + 1, nblk - 1), 0, 0),
                               memory_space=pltpu.SMEM),
                  pl.BlockSpec(memory_space=pl.ANY), wspec(d, de), wspec(d, de)],
        out_specs=pl.BlockSpec((bm, de), lambda i, be, nb: (i, 0)),
        scratch_shapes=[pltpu.VMEM((2, bm, pitch, LANE), F32), pltpu.SemaphoreType.DMA((2,))],
    )
    act = pl.pallas_call(
        _moe_up_kernel,
        grid_spec=up_spec,
        out_shape=jax.ShapeDtypeStruct((nblk * bm, de), BF16),
        compiler_params=_cparams(("arbitrary",)),
    )(blk_e, n_used, row_tok, row_tok, hs, w1, w3)
    down_spec = pltpu.PrefetchScalarGridSpec(
        num_scalar_prefetch=2,
        grid=(nblk,),
        in_specs=[pl.BlockSpec((bm, de), lambda i, be, nb: (i, 0)), wspec(de, d)],
        out_specs=pl.BlockSpec((bm, d), lambda i, be, nb: (i, 0)),
    )
    return pl.pallas_call(
        _moe_down_kernel,
        grid_spec=down_spec,
        out_shape=jax.ShapeDtypeStruct((nblk * bm, d), F32),
        compiler_params=_cparams(("arbitrary",)),
    )(blk_e, n_used, act, w2)


def _combine_ln_kernel(d0_ref, d1_ref, d0n_ref, d1n_ref, y_hbm, h_ref, gate_ref, g_ref, b_ref, o_ref, ob_ref,
                       ybuf, sem, *, alpha):
    i = pl.program_id(0)
    slot = i % 2

    @pl.when(i == 0)
    def _():
        _gather_start(y_hbm, d0_ref, ybuf.at[0, 0], sem.at[0, 0])
        _gather_start(y_hbm, d1_ref, ybuf.at[0, 1], sem.at[0, 1])

    @pl.when(i + 1 < pl.num_programs(0))
    def _():
        _gather_start(y_hbm, d0n_ref, ybuf.at[1 - slot, 0], sem.at[1 - slot, 0])
        _gather_start(y_hbm, d1n_ref, ybuf.at[1 - slot, 1], sem.at[1 - slot, 1])

    _gather_wait(y_hbm, ybuf.at[slot, 0], sem.at[slot, 0])
    _gather_wait(y_hbm, ybuf.at[slot, 1], sem.at[slot, 1])
    gate = gate_ref[...]
    ff = ybuf[slot, 0] * gate[:, 0:1] + ybuf[slot, 1] * gate[:, 1:2]
    y = _layernorm_rows(alpha * h_ref[...] + ff, g_ref[...], b_ref[...])
    o_ref[...] = y
    ob_ref[...] = y.astype(ob_ref.dtype)


def _combine_ln(yb, h, gates, dest0, dest1, g, b, alpha):
    r, d = h.shape
    tm = dest0.shape[2]
    nblk = r // tm
    kern = functools.partial(_combine_ln_kernel, alpha=alpha)
    idx_spec = pl.BlockSpec((1, 1, tm), lambda i: (i, 0, 0), memory_space=pltpu.SMEM)
    nxt_spec = pl.BlockSpec((1, 1, tm), lambda i: (jnp.minimum(i + 1, nblk - 1), 0, 0), memory_space=pltpu.SMEM)
    return pl.pallas_call(
        kern,
        grid=(nblk,),
        in_specs=[idx_spec, idx_spec, nxt_spec, nxt_spec,
                  pl.BlockSpec(memory_space=pl.ANY),
                  pl.BlockSpec((tm, d), lambda i: (i, 0)),
                  pl.BlockSpec((tm, LANE), lambda i: (i, 0)),
                  pl.BlockSpec((1, d), lambda i: (0, 0)),
                  pl.BlockSpec((1, d), lambda i: (0, 0))],
        out_specs=[pl.BlockSpec((tm, d), lambda i: (i, 0))] * 2,
        out_shape=[jax.ShapeDtypeStruct((r, d), F32), jax.ShapeDtypeStruct((r, d), BF16)],
        scratch_shapes=[pltpu.VMEM((2, 2, tm, d), F32), pltpu.SemaphoreType.DMA((2, 2))],
        compiler_params=_cparams(("arbitrary",)),
    )(dest0, dest1, dest0, dest1, yb, h, gates, g, b)


def _dispatch_plan(eid, n_experts, bm):
    r = eid.shape[0]
    a_tot = r * TOP_K
    e_flat = eid[:, :TOP_K].reshape(a_tot)
    onehot = (e_flat[:, None] == jnp.arange(n_experts, dtype=I32)[None, :]).astype(I32)
    before = jnp.cumsum(onehot, axis=0) - onehot
    rank = jnp.sum(before * onehot, axis=1)
    counts = jnp.sum(onehot, axis=0)
    pcounts = (counts + bm - 1) // bm * bm
    pends = jnp.cumsum(pcounts)
    pstarts = pends - pcounts
    dest = (pstarts[e_flat] + rank).astype(I32)
    nblk = -(-(a_tot + n_experts * (bm - 1)) // bm)
    row_tok = jnp.zeros((nblk * bm,), I32).at[dest].set(jnp.arange(a_tot, dtype=I32) // TOP_K)
    n_used = (pends[-1] // bm).astype(I32)
    blk = jnp.arange(nblk, dtype=I32)
    blk_e = jnp.searchsorted(pends, jnp.minimum(blk, n_used - 1) * bm, side="right").astype(I32)
    blk_e = jnp.minimum(blk_e, n_experts - 1)
    dest2 = dest.reshape(r, TOP_K)
    return row_tok.reshape(nblk, 1, bm), blk_e, n_used.reshape(1), dest2[:, 0], dest2[:, 1]


def kernel(x, meta, w_in, rw_mu, rw_w2, rw_w0, rw_a2, rw_a0, rw_g2, rw_kk, rw_ka, rw_rk, rw_gn_g, rw_gn_b,
           at_qnorm_g, at_w_uq, at_w_iq, at_kidx_g, at_kidx_b, pool_w, pool_scale, w_out, ln1_g, ln1_b,
           router_g_w, router_g_b, router_e_w, router_e_b, exp_w1, exp_w3, exp_w2, ln2_g, ln2_b):
    bsz, seq_len, d = x.shape
    depth = w_in.shape[0]
    n_meta = meta.shape[0]
    t_len = seq_len + n_meta
    p_len = _round_up(t_len, ROW_TILE)
    rows = bsz * p_len
    n_sel = min(TOPK_MAX, seq_len // 4)
    alpha = float((2 * depth) ** 0.25)

    rw = rw_w2.shape[2]
    aw = at_w_uq.shape[2]
    idx_head = at_kidx_g.shape[1]
    idx_heads = at_w_iq.shape[2] // idx_head
    assert idx_head == LANE and aw % AT_HEAD == 0 and rw % LANE == 0
    n_groups = router_g_w.shape[2]
    n_experts = router_e_w.shape[2]
    per_group = n_experts // n_groups
    assert n_groups + n_experts <= LANE
    pool_width = pool_w.shape[1] * pool_w.shape[2]

    src = {"r": rw, "k": rw, "v": rw, "dw": rw_w2.shape[1], "da": rw_a2.shape[1], "dg": rw_g2.shape[1],
           "cq": at_w_uq.shape[1], "ka": aw, "va": aw, "kidx": idx_head, "widx": idx_heads, "pin": pool_width}
    off, pw, n_proj = _layout(src)
    n_proj = _round_up(n_proj, 512)
    rw_names = ("r", "k", "v", "dw", "da", "dg")
    rw_src = {n: src[n] for n in rw_names}
    mu_starts = np.cumsum([0] + [rw_src[n] for n in rw_names])

    pos = jnp.arange(p_len, dtype=F32)
    inv = ROPE_THETA ** (-jnp.arange(0, AT_HEAD, 2, dtype=F32) / AT_HEAD)
    ang = pos[:, None] * inv[None, :]
    cosf = jnp.concatenate([jnp.cos(ang), jnp.cos(ang)], axis=1)
    sinf = jnp.concatenate([-jnp.sin(ang), jnp.sin(ang)], axis=1)

    h = jnp.concatenate([jnp.broadcast_to(meta.astype(x.dtype)[None], (bsz, n_meta, d)), x,
                         jnp.zeros((bsz, p_len - t_len, d), x.dtype)], axis=1).reshape(rows, d)
    hb = h.astype(BF16)
    row1 = lambda a: a.reshape(1, -1)
    pad_rows = lambda a, n: jnp.concatenate([a, jnp.zeros((n - a.shape[0],) + a.shape[1:], a.dtype)], axis=0)
    tm_c = ROW_TILE

    w_in_p = _relayout_rows(jnp.swapaxes(w_in, 1, 2), src, off, n_proj)

    for l in range(depth):
        proj = _matmul(hb, w_in_p, l, F32)

        mu = {}
        for i, n in enumerate(rw_names):
            seg = rw_mu[l, int(mu_starts[i]):int(mu_starts[i + 1])]
            mu[n] = row1(jnp.concatenate([seg, jnp.zeros((pw[n] - src[n],), F32)]))
        g2 = pad_rows(rw_g2[l], pw["dg"]).astype(BF16)
        seqs = _rwkv_prep(proj, bsz, p_len, off, pw, mu, rw_w2[l].astype(BF16), row1(rw_w0[l]),
                          rw_a2[l].astype(BF16), row1(rw_a0[l]), g2, row1(rw_kk[l]), row1(rw_ka[l]),
                          row1(rw_rk[l]))
        mix = jnp.zeros((bsz, p_len, rw + aw + pool_width), BF16)
        mix = _rwkv_scan(seqs, row1(rw_gn_g[l]), row1(rw_gn_b[l]), mix)

        q, qi, k_at, v_at, ki, wq = _dsa_prep(proj, bsz, p_len, off, pw, cosf, sinf, row1(at_qnorm_g[l]),
                                              at_w_uq[l].astype(BF16), at_w_iq[l].astype(BF16),
                                              row1(at_kidx_g[l]), row1(at_kidx_b[l]), idx_heads)
        mix = _dsa_attn(q, qi, wq, k_at, v_at, ki, mix, rw, n_sel, idx_heads)
        mix = _pool_mix(proj, off, pw, pool_w[l].astype(BF16), row1(pool_scale[l]), mix, rw + aw)
        h, hb, hs = _outproj_ln(mix, w_out[l].astype(BF16), h, row1(ln1_g[l]), row1(ln1_b[l]), alpha)

        n_r = n_groups + n_experts
        w_r = jnp.concatenate([router_g_w[l], router_e_w[l], jnp.zeros((d, LANE - n_r), F32)], axis=1).astype(BF16)
        b_r = row1(jnp.concatenate([router_g_b[l], router_e_b[l], jnp.zeros((LANE - n_r,), F32)]))
        eid, gates = _router(hb, w_r, b_r, n_groups, per_group)
        row_tok, blk_e, n_used, dest0, dest1 = _dispatch_plan(eid, n_experts, MOE_BM)
        yb = _moe_ffn(hs, exp_w1, exp_w3, exp_w2, l, blk_e, n_used, row_tok)
        h, hb = _combine_ln(yb, h, gates, dest0.reshape(rows // tm_c, 1, tm_c), dest1.reshape(rows // tm_c, 1, tm_c),
                            row1(ln2_g[l]), row1(ln2_b[l]), alpha)

    return h.reshape(bsz, p_len, d)[:, n_meta:t_len]
```
